```python
import math
import jax, jax.numpy as jnp
from jax import lax
import numpy as np

D_MODEL = 1024
BATCH = 2
SEQ = 8192
DEPTH = 2
DEC_BATCH = 128
DEC_SEQ = 8
PAST_LEN = 8192
PAGE_SIZE = 128

HEAD_DIM = 64
ATTN_HEADS = D_MODEL // HEAD_DIM
ATTN_KV_HEADS = ATTN_HEADS // 4
WINDOW = 128
D_RNN = D_MODEL
RNN_BLOCKS = 16
RNN_BLOCK = D_RNN // RNN_BLOCKS
CONV_W = 4
RG_C = 8.0
RWKV_HEADS = D_MODEL // HEAD_DIM
RW_D = RWKV_HEADS * HEAD_DIM
LORA_W = 64
LORA_A = 64
LORA_G = 160
GN_EPS = 64e-5
LN_EPS = 1e-5
_FF_RAW = (8 * D_MODEL + 2) // 3
D_FF = ((_FF_RAW + 255) // 256) * 256
ALPHA = (2 * DEPTH) ** 0.25
BETA = (8 * DEPTH) ** -0.25

Q_W = ATTN_HEADS * HEAD_DIM
KV_W = ATTN_KV_HEADS * HEAD_DIM
OFF_Q = 0
OFF_K = OFF_Q + Q_W
OFF_V = OFF_K + KV_W
OFF_RNN = OFF_V + KV_W
OFF_RW = OFF_RNN + D_RNN
SHIFT_W = 3 * RW_D + LORA_W + LORA_A + LORA_G
OFF_GATE = OFF_RW + SHIFT_W
N_BRANCH = 3
IN_COLS = OFF_GATE + N_BRANCH * D_MODEL

kernel_name = 'hybrid_swa_rglru_rwkv7_step'


def layer_norm(x, g, b):
    xf = x.astype(jnp.float32)
    mu = xf.mean(-1, keepdims=True)
    var = jnp.square(xf - mu).mean(-1, keepdims=True)
    return ((xf - mu) * lax.rsqrt(var + LN_EPS) * g.astype(jnp.float32) + b.astype(jnp.float32)).astype(x.dtype)


def alibi_slopes():
    h = jnp.arange(1, ATTN_HEADS + 1, dtype=jnp.float32)
    return 2.0 ** (-8.0 * h / ATTN_HEADS)


def sink_attend(q, k, v, sinks, rel, valid):
    n, tq, h, d = q.shape
    kvh = k.shape[2]
    g = h // kvh
    qg = q.reshape(n, tq, kvh, g, d)
    s = jnp.einsum('nqcgd,nscd->ncgqs', qg, k).astype(jnp.float32) * (d ** -0.5)
    s = s - alibi_slopes().reshape(kvh, g, 1, 1) * rel
    s = jnp.where(valid, s, -jnp.inf)
    sink = sinks.astype(jnp.float32).reshape(1, kvh, g, 1, 1)
    m = jnp.maximum(s.max(-1, keepdims=True), sink)
    e = jnp.exp(s - m)
    den = e.sum(-1, keepdims=True) + jnp.exp(sink - m)
    p = (e / den).astype(v.dtype)
    o = jnp.einsum('ncgqs,nscd->nqcgd', p, v)
    return o.reshape(n, tq, h * d)


def attn_prompt(q, k, v, sinks):
    b, t, h, d = q.shape
    w = WINDOW
    nb = t // w
    kvh = k.shape[2]
    qb = q.reshape(b * nb, w, h, d)
    kb = k.reshape(b, nb, w, kvh, d)
    vb = v.reshape(b, nb, w, kvh, d)
    kc = jnp.concatenate([jnp.concatenate([jnp.zeros_like(kb[:, :1]), kb[:, :-1]], axis=1), kb], axis=2)
    vc = jnp.concatenate([jnp.concatenate([jnp.zeros_like(vb[:, :1]), vb[:, :-1]], axis=1), vb], axis=2)
    kc = kc.reshape(b * nb, 2 * w, kvh, d)
    vc = vc.reshape(b * nb, 2 * w, kvh, d)
    tpos = w + jnp.arange(w)
    spos = jnp.arange(2 * w)
    rel_i = tpos[:, None] - spos[None, :]
    band = (rel_i >= 0) & (rel_i < WINDOW)
    has_prev = (jnp.arange(nb) > 0)[:, None, None] | (spos >= w)[None, None, :]
    valid = band[None] & has_prev
    valid = jnp.tile(valid, (b, 1, 1)).reshape(b * nb, 1, 1, w, 2 * w)
    o = sink_attend(qb, kc, vc, sinks, rel_i.astype(jnp.float32), valid)
    return o.reshape(b, t, h * d)


def attn_sample(q, k, v, sinks, win_k, win_v):
    t = q.shape[1]
    wb = win_k.shape[1]
    kc = jnp.concatenate([win_k.astype(k.dtype), k], axis=1)
    vc = jnp.concatenate([win_v.astype(v.dtype), v], axis=1)
    rel_i = (wb + jnp.arange(t))[:, None] - jnp.arange(wb + t)[None, :]
    valid = ((rel_i >= 0) & (rel_i < WINDOW))[None, None, None]
    o = sink_attend(q, kc, vc, sinks, rel_i.astype(jnp.float32), valid)
    return o, kc[:, -wb:], vc[:, -wb:]


def causal_conv(x, buf, w, b):
    t = x.shape[1]
    xp = jnp.concatenate([buf.astype(x.dtype), x], axis=1)
    y = b + sum(w[i] * xp[:, i:i + t] for i in range(CONV_W))
    return y, xp[:, -(CONV_W - 1):]


def rglru(x, conv_buf, h0, conv_w, conv_b, wa, ba, wx, bx, lam):
    xc, new_buf = causal_conv(x, conv_buf, conv_w, conv_b)
    n, t, _ = xc.shape
    xb = xc.reshape(n, t, RNN_BLOCKS, RNN_BLOCK)
    r = jax.nn.sigmoid(jnp.einsum('ntbi,bij->ntbj', xb, wa).reshape(n, t, D_RNN) + ba)
    i = jax.nn.sigmoid(jnp.einsum('ntbi,bij->ntbj', xb, wx).reshape(n, t, D_RNN) + bx)
    log_a = -RG_C * r.astype(jnp.float32) * jax.nn.softplus(-lam.astype(jnp.float32))
    a = jnp.exp(log_a)
    u = jnp.sqrt(-jnp.expm1(2.0 * log_a)) * (i * xc).astype(jnp.float32)
    u = u.at[:, 0].add(a[:, 0] * h0.astype(jnp.float32))

    def combine(c1, c2):
        a1, b1 = c1
        a2, b2 = c2
        return a1 * a2, a2 * b1 + b2

    _, h = lax.associative_scan(combine, (a, u), axis=1)
    return h.astype(x.dtype), new_buf, h[:, -1].astype(h0.dtype)


def wkv_scan(r, w, k, v, kk, a, s0):
    xs = tuple(jnp.moveaxis(z.astype(jnp.float32), 1, 0) for z in (r, w, k, v, kk, a))

    def step(s, inp):
        rt, wt, kt, vt, kkt, at = inp
        sa = jnp.einsum('nhvk,nhk->nhv', s, -kkt)
        s = s * wt[:, :, None, :] + sa[..., None] * (kkt * at)[:, :, None, :] + vt[..., None] * kt[:, :, None, :]
        return s, jnp.einsum('nhvk,nhk->nhv', s, rt)

    s_fin, y = lax.scan(step, s0.astype(jnp.float32), xs)
    return jnp.moveaxis(y, 0, 1), s_fin.astype(s0.dtype)


def rwkv_tmix(pc, shift0, s0, p):
    n, t, _ = pc.shape
    prev = jnp.concatenate([shift0[:, None].astype(pc.dtype), pc[:, :-1]], axis=1)
    ps = pc + p['rw_mu'] * (prev - pc)
    new_shift = pc[:, -1]
    o1 = 3 * RW_D
    r = ps[..., :RW_D]
    k = ps[..., RW_D:2 * RW_D]
    v = ps[..., 2 * RW_D:o1]
    lw = ps[..., o1:o1 + LORA_W]
    la = ps[..., o1 + LORA_W:o1 + LORA_W + LORA_A]
    lg = ps[..., o1 + LORA_W + LORA_A:]
    wlog = -jax.nn.softplus(-(p['rw_w0'] + jnp.tanh(lw) @ p['rw_wup'])) - 0.5
    decay = jnp.exp(-jnp.exp(wlog.astype(jnp.float32)))
    a = jax.nn.sigmoid(p['rw_a0'] + la @ p['rw_aup'])
    g = jax.nn.sigmoid(lg) @ p['rw_gup']

    def hs(z):
        return z.reshape(n, t, RWKV_HEADS, HEAD_DIM)

    kk = hs(k * p['rw_kk']).astype(jnp.float32)
    kk = kk / jnp.maximum(jnp.sqrt(jnp.sum(kk * kk, -1, keepdims=True)), 1e-12)
    k = k * (1.0 + (a - 1.0) * p['rw_ka'])
    y, s_new = wkv_scan(hs(r), hs(decay), hs(k), hs(v), kk, hs(a), s0)
    mu = y.mean(-1, keepdims=True)
    var = jnp.square(y - mu).mean(-1, keepdims=True)
    y = ((y - mu) * lax.rsqrt(var + GN_EPS)).reshape(n, t, RW_D)
    y = y * p['rw_lnw'].astype(jnp.float32) + p['rw_lnb'].astype(jnp.float32)
    bonus = (jnp.sum(hs(r).astype(jnp.float32) * hs(k).astype(jnp.float32) * p['rw_rk'].astype(jnp.float32), -1, keepdims=True)
             * hs(v).astype(jnp.float32)).reshape(n, t, RW_D)
    o = ((y + bonus) * g.astype(jnp.float32)).astype(pc.dtype)
    return o, new_shift, s_new


def hybrid_layer(x, win_k, win_v, conv_buf, h0, shift0, s0, win_rows, p):
    n, t, _ = x.shape
    proj = x @ p['w_in']
    q = proj[..., OFF_Q:OFF_K].reshape(n, t, ATTN_HEADS, HEAD_DIM)
    k = proj[..., OFF_K:OFF_V].reshape(n, t, ATTN_KV_HEADS, HEAD_DIM)
    v = proj[..., OFF_V:OFF_RNN].reshape(n, t, ATTN_KV_HEADS, HEAD_DIM)
    if win_k is None:
        o_a = attn_prompt(q, k, v, p['attn_sinks'])
        new_k, new_v = k[:, -win_rows:], v[:, -win_rows:]
    else:
        o_a, new_k, new_v = attn_sample(q, k, v, p['attn_sinks'], win_k, win_v)
    o_b, new_conv, new_h = rglru(proj[..., OFF_RNN:OFF_RW], conv_buf, h0, p['conv_w'], p['conv_b'],
                                 p['rg_wa'], p['rg_ba'], p['rg_wx'], p['rg_bx'], p['rg_lambda'])
    o_c, new_shift, new_s = rwkv_tmix(proj[..., OFF_RW:OFF_GATE], shift0, s0, p)
    gates = jax.nn.sigmoid(proj[..., OFF_GATE:]).reshape(n, t, N_BRANCH, D_MODEL)
    mixed = gates[..., 0, :] * o_a + gates[..., 1, :] * o_b + gates[..., 2, :] * o_c
    x = layer_norm(ALPHA * x + mixed @ p['w_out'], p['ln1_g'], p['ln1_b'])
    gu = x @ p['w_gu']
    ff = (jax.nn.silu(gu[..., :D_FF]) * gu[..., D_FF:]) @ p['w_down']
    x = layer_norm(ALPHA * x + ff, p['ln2_g'], p['ln2_b'])
    return x, (new_k, new_v, new_conv, new_h, new_shift, new_s)


def _nrm(key, shape, scale):
    return scale * jax.random.normal(key, shape, jnp.float32)


def setup_inputs(seed: int = 0) -> dict:
    key = jax.random.key(seed)
    ks = jax.random.split(key, 40)
    win_buf = min(WINDOW, PAST_LEN)
    col_scale = jnp.ones((IN_COLS,), jnp.float32).at[OFF_V:OFF_RNN].set(BETA).at[OFF_RW + 2 * RW_D:OFF_RW + 3 * RW_D].set(BETA)
    a_init = jax.random.uniform(ks[16], (DEPTH, D_RNN), jnp.float32, 0.9, 0.999)
    sig_l = a_init ** (1.0 / RG_C)
    ramp = -6.5 + 5.0 * jnp.linspace(0.0, 1.0, RW_D) ** 0.85
    return {
        'x_prompt': _nrm(ks[0], (BATCH, SEQ, D_MODEL), 1.0),
        'x_sample': _nrm(ks[1], (DEC_BATCH, DEC_SEQ, D_MODEL), 1.0),
        'cache_k': _nrm(ks[2], (DEPTH, DEC_BATCH, win_buf, ATTN_KV_HEADS, HEAD_DIM), 1.0),
        'cache_v': _nrm(ks[3], (DEPTH, DEC_BATCH, win_buf, ATTN_KV_HEADS, HEAD_DIM), BETA),
        'state_conv': _nrm(ks[4], (DEPTH, DEC_BATCH, CONV_W - 1, D_RNN), 1.0),
        'state_rglru': _nrm(ks[5], (DEPTH, DEC_BATCH, D_RNN), 0.5),
        'state_shift': _nrm(ks[6], (DEPTH, DEC_BATCH, SHIFT_W), 1.0),
        'state_wkv': _nrm(ks[7], (DEPTH, DEC_BATCH, RWKV_HEADS, HEAD_DIM, HEAD_DIM), 0.3),
        'w_in': _nrm(ks[8], (DEPTH, D_MODEL, IN_COLS), D_MODEL ** -0.5) * col_scale,
        'attn_sinks': _nrm(ks[9], (DEPTH, ATTN_HEADS), 0.5),
        'conv_w': _nrm(ks[10], (DEPTH, CONV_W, D_RNN), CONV_W ** -0.5),
        'conv_b': _nrm(ks[11], (DEPTH, D_RNN), 0.02),
        'rg_wa': _nrm(ks[12], (DEPTH, RNN_BLOCKS, RNN_BLOCK, RNN_BLOCK), RNN_BLOCK ** -0.5),
        'rg_ba': _nrm(ks[13], (DEPTH, D_RNN), 0.02),
        'rg_wx': _nrm(ks[14], (DEPTH, RNN_BLOCKS, RNN_BLOCK, RNN_BLOCK), RNN_BLOCK ** -0.5),
        'rg_bx': _nrm(ks[15], (DEPTH, D_RNN), 0.02),
        'rg_lambda': jnp.log(sig_l) - jnp.log1p(-sig_l),
        'rw_mu': jax.random.uniform(ks[17], (DEPTH, SHIFT_W), jnp.float32),
        'rw_w0': ramp[None, :] + _nrm(ks[18], (DEPTH, RW_D), 0.1),
        'rw_wup': _nrm(ks[19], (DEPTH, LORA_W, RW_D), 0.1 * LORA_W ** -0.5),
        'rw_a0': _nrm(ks[20], (DEPTH, RW_D), 0.1),
        'rw_aup': _nrm(ks[21], (DEPTH, LORA_A, RW_D), 0.5 * LORA_A ** -0.5),
        'rw_gup': _nrm(ks[22], (DEPTH, LORA_G, RW_D), LORA_G ** -0.5),
        'rw_kk': 0.85 + _nrm(ks[23], (DEPTH, RW_D), 0.05),
        'rw_ka': 1.0 + _nrm(ks[24], (DEPTH, RW_D), 0.05),
        'rw_rk': _nrm(ks[25], (DEPTH, RWKV_HEADS, HEAD_DIM), 0.1),
        'rw_lnw': 1.0 + _nrm(ks[26], (DEPTH, RW_D), 0.05),
        'rw_lnb': _nrm(ks[27], (DEPTH, RW_D), 0.02),
        'w_out': _nrm(ks[28], (DEPTH, D_MODEL, D_MODEL), BETA * D_MODEL ** -0.5),
        'ln1_g': 1.0 + _nrm(ks[29], (DEPTH, D_MODEL), 0.05),
        'ln1_b': _nrm(ks[30], (DEPTH, D_MODEL), 0.02),
        'w_gu': _nrm(ks[31], (DEPTH, D_MODEL, 2 * D_FF), D_MODEL ** -0.5),
        'w_down': _nrm(ks[32], (DEPTH, D_FF, D_MODEL), BETA * D_FF ** -0.5),
        'ln2_g': 1.0 + _nrm(ks[33], (DEPTH, D_MODEL), 0.05),
        'ln2_b': _nrm(ks[34], (DEPTH, D_MODEL), 0.02),
    }


def reference(x_prompt, x_sample, cache_k, cache_v, state_conv, state_rglru, state_shift, state_wkv,
              w_in, attn_sinks, conv_w, conv_b, rg_wa, rg_ba, rg_wx, rg_bx, rg_lambda,
              rw_mu, rw_w0, rw_wup, rw_a0, rw_aup, rw_gup, rw_kk, rw_ka, rw_rk, rw_lnw, rw_lnb,
              w_out, ln1_g, ln1_b, w_gu, w_down, ln2_g, ln2_b):
    win_rows = cache_k.shape[2]
    bp = x_prompt.shape[0]
    dt = x_prompt.dtype
    yp = x_prompt
    ys = x_sample
    st_p = []
    st_s = []
    for l in range(DEPTH):
        p = dict(w_in=w_in[l], attn_sinks=attn_sinks[l], conv_w=conv_w[l], conv_b=conv_b[l],
                 rg_wa=rg_wa[l], rg_ba=rg_ba[l], rg_wx=rg_wx[l], rg_bx=rg_bx[l], rg_lambda=rg_lambda[l],
                 rw_mu=rw_mu[l], rw_w0=rw_w0[l], rw_wup=rw_wup[l], rw_a0=rw_a0[l], rw_aup=rw_aup[l],
                 rw_gup=rw_gup[l], rw_kk=rw_kk[l], rw_ka=rw_ka[l], rw_rk=rw_rk[l], rw_lnw=rw_lnw[l],
                 rw_lnb=rw_lnb[l], w_out=w_out[l], ln1_g=ln1_g[l], ln1_b=ln1_b[l], w_gu=w_gu[l],
                 w_down=w_down[l], ln2_g=ln2_g[l], ln2_b=ln2_b[l])
        yp, sp = hybrid_layer(yp, None, None,
                              jnp.zeros((bp, CONV_W - 1, D_RNN), dt), jnp.zeros((bp, D_RNN), dt),
                              jnp.zeros((bp, SHIFT_W), dt),
                              jnp.zeros((bp, RWKV_HEADS, HEAD_DIM, HEAD_DIM), jnp.float32), win_rows, p)
        ys, ss = hybrid_layer(ys, cache_k[l], cache_v[l], state_conv[l], state_rglru[l], state_shift[l],
                              state_wkv[l], win_rows, p)
        st_p.append(sp)
        st_s.append(ss)
    p_k = jnp.stack([s[0] for s in st_p])
    p_v = jnp.stack([s[1] for s in st_p])
    p_conv = jnp.stack([s[2] for s in st_p])
    p_h = jnp.stack([s[3] for s in st_p])
    p_shift = jnp.stack([s[4] for s in st_p])
    p_wkv = jnp.stack([s[5] for s in st_p])
    s_k = jnp.stack([s[0] for s in st_s])
    s_v = jnp.stack([s[1] for s in st_s])
    s_conv = jnp.stack([s[2] for s in st_s])
    s_h = jnp.stack([s[3] for s in st_s])
    s_shift = jnp.stack([s[4] for s in st_s])
    s_wkv = jnp.stack([s[5] for s in st_s])
    return (yp, ys, p_k, p_v, p_conv, p_h, p_shift, p_wkv, s_k, s_v, s_conv, s_h, s_shift, s_wkv)
```

```python
import functools
import math

import jax
import jax.numpy as jnp
from jax import lax
from jax.experimental import pallas as pl
from jax.experimental.pallas import tpu as pltpu

F32 = jnp.float32
BF16 = jnp.bfloat16

D_MODEL = 1024
HEAD_DIM = 64
N_HEADS = 16
N_KV_HEADS = 4
KV_W = N_KV_HEADS * HEAD_DIM
WINDOW = 128
CONV_W = 4
RG_C = 8.0
LORA_W = 64
LORA_A = 64
LORA_G = 160
SHIFT_W = 3 * D_MODEL + LORA_W + LORA_A + LORA_G
LANES = 128
SHIFT_PAD = -(-SHIFT_W // LANES) * LANES
GN_EPS = 64e-5
LN_EPS = 1e-5
OFF_K = D_MODEL
OFF_V = OFF_K + KV_W
OFF_RNN = OFF_V + KV_W
OFF_RW = OFF_RNN + D_MODEL
OFF_GATE = OFF_RW + SHIFT_W
WKV_CHUNK = 64
VMEM_LIMIT = 48 * 1024 * 1024


def _cparams(*sem):
    return pltpu.CompilerParams(dimension_semantics=sem, vmem_limit_bytes=VMEM_LIMIT)


def _tile(n, pref):
    if n <= pref:
        return n
    for t in range(pref, 7, -1):
        if n % t == 0 and t % 8 == 0:
            return t
    return n


def _dot(a, b):
    return jnp.dot(a.astype(BF16), b.astype(BF16), preferred_element_type=F32)


def _split_terms(x, n):
    terms = []
    rem = x
    for _ in range(n):
        hi = rem.astype(BF16)
        terms.append(hi)
        rem = rem - hi.astype(F32)
    return terms


def _dot_lhs_split(x, w_bf16, n):
    acc = None
    for t in _split_terms(x, n):
        p = jnp.dot(t, w_bf16, preferred_element_type=F32)
        acc = p if acc is None else acc + p
    return acc


def _dot_rhs_split(w_bf16, x, n):
    acc = None
    for t in _split_terms(x, n):
        p = jnp.dot(w_bf16, t, preferred_element_type=F32)
        acc = p if acc is None else acc + p
    return acc


def _sigmoid(x):
    return 1.0 / (1.0 + jnp.exp(-x))


def _softplus(x):
    return jnp.maximum(x, 0.0) + jnp.log(1.0 + jnp.exp(-jnp.abs(x)))


def _layer_norm(z, g, b):
    mu = jnp.mean(z, axis=-1, keepdims=True)
    zc = z - mu
    var = jnp.mean(zc * zc, axis=-1, keepdims=True)
    return zc * lax.rsqrt(var + LN_EPS) * g + b


def _head_indicator():
    c = lax.broadcasted_iota(jnp.int32, (D_MODEL, LANES), 0) // HEAD_DIM
    h = lax.broadcasted_iota(jnp.int32, (D_MODEL, LANES), 1)
    e = jnp.where(c == h, 1.0, 0.0).astype(BF16)
    ct = lax.broadcasted_iota(jnp.int32, (LANES, D_MODEL), 1) // HEAD_DIM
    ht = lax.broadcasted_iota(jnp.int32, (LANES, D_MODEL), 0)
    et = jnp.where(ct == ht, 1.0, 0.0).astype(BF16)
    return e, et


def _head_sum(x, e, et):
    s = _dot_lhs_split(x, e, 2)
    return _dot_lhs_split(s, et, 3)


def _matmul_kernel(x_ref, w_ref, o_ref):
    o_ref[...] = jnp.dot(x_ref[...].astype(BF16), w_ref[...], preferred_element_type=F32)


def _matmul(x, w, tn):
    m, k = x.shape
    n = w.shape[1]
    tm = _tile(m, 512)
    return pl.pallas_call(
        _matmul_kernel,
        grid=(n // tn, m // tm),
        in_specs=[pl.BlockSpec((tm, k), lambda j, i: (i, 0)),
                  pl.BlockSpec((k, tn), lambda j, i: (0, j))],
        out_specs=pl.BlockSpec((tm, tn), lambda j, i: (i, j)),
        out_shape=jax.ShapeDtypeStruct((m, n), F32),
        compiler_params=_cparams("parallel", "parallel"),
        name="in_proj",
    )(x, w)


def _alibi_slope(h):
    return 2.0 ** (-8.0 * (h + 1) / N_HEADS)


def _attn_prompt_kernel(sink_ref, q_ref, kc_ref, vc_ref, kp_ref, vp_ref, o_ref):
    j = pl.program_id(1)
    w = WINDOW
    row = lax.broadcasted_iota(jnp.int32, (w, 2 * w), 0)
    col = lax.broadcasted_iota(jnp.int32, (w, 2 * w), 1)
    rel = w + row - col
    valid = (rel >= 0) & (rel < WINDOW) & ((col >= w) | (j > 0))
    relf = rel.astype(F32)
    q = q_ref[...]
    kcat = jnp.concatenate([kp_ref[...], kc_ref[...]], axis=0).astype(BF16)
    vcat = jnp.concatenate([vp_ref[...], vc_ref[...]], axis=0).astype(BF16)
    outs = []
    for h in range(N_HEADS):
        g = h // (N_HEADS // N_KV_HEADS)
        qh = q[:, h * HEAD_DIM:(h + 1) * HEAD_DIM].astype(BF16)
        kg = kcat[:, g * HEAD_DIM:(g + 1) * HEAD_DIM]
        vg = vcat[:, g * HEAD_DIM:(g + 1) * HEAD_DIM]
        s = lax.dot_general(qh, kg, (((1,), (1,)), ((), ())), preferred_element_type=F32)
        s = s * (HEAD_DIM ** -0.5) - _alibi_slope(h) * relf
        s = jnp.where(valid, s, -jnp.inf)
        sink = sink_ref[h]
        m = jnp.maximum(jnp.max(s, axis=-1, keepdims=True), sink)
        e = jnp.exp(s - m)
        den = jnp.sum(e, axis=-1, keepdims=True) + jnp.exp(sink - m)
        p = e / den
        outs.append(jnp.dot(p.astype(BF16), vg, preferred_element_type=F32))
    o_ref[...] = jnp.concatenate(outs, axis=1)


def _attn_prompt(qkv, sinks, nb_seq, seq):
    w = WINDOW
    nb = seq // w
    kcol = OFF_K // KV_W
    vcol = OFF_V // KV_W

    def cur(c):
        return lambda b, j: (b * nb + j, c)

    def prev(c):
        return lambda b, j: (jnp.maximum(b * nb + j - 1, 0), c)

    return pl.pallas_call(
        _attn_prompt_kernel,
        grid=(nb_seq, nb),
        in_specs=[pl.BlockSpec(memory_space=pltpu.SMEM),
                  pl.BlockSpec((w, D_MODEL), cur(0)),
                  pl.BlockSpec((w, KV_W), cur(kcol)),
                  pl.BlockSpec((w, KV_W), cur(vcol)),
                  pl.BlockSpec((w, KV_W), prev(kcol)),
                  pl.BlockSpec((w, KV_W), prev(vcol))],
        out_specs=pl.BlockSpec((w, D_MODEL), cur(0)),
        out_shape=jax.ShapeDtypeStruct((nb_seq * seq, D_MODEL), F32),
        compiler_params=_cparams("parallel", "parallel"),
        name="attn_prompt",
    )(sinks, qkv, qkv, qkv, qkv, qkv)


def _attn_sample_kernel(sink_ref, q_ref, kn_ref, vn_ref, ck_ref, cv_ref, o_ref, nk_ref, nv_ref, *, t):
    sn = q_ref.shape[0]
    wb = ck_ref.shape[1]
    gsz = N_HEADS // N_KV_HEADS
    rows = gsz * t
    q = q_ref[...]
    kn = kn_ref[...]
    vn = vn_ref[...]
    ck = ck_ref[...]
    cv = cv_ref[...]
    tq1 = lax.broadcasted_iota(jnp.int32, (rows, wb), 0) % t
    j1 = lax.broadcasted_iota(jnp.int32, (rows, wb), 1)
    rel1 = wb + tq1 - j1
    valid1 = (rel1 >= 0) & (rel1 < WINDOW)
    tq2 = lax.broadcasted_iota(jnp.int32, (rows, t), 0) % t
    s2i = lax.broadcasted_iota(jnp.int32, (rows, t), 1)
    rel2 = tq2 - s2i
    valid2 = (rel2 >= 0) & (rel2 < WINDOW)
    hrow = lax.broadcasted_iota(jnp.int32, (rows, 1), 0) // t
    outs = []
    for g in range(N_KV_HEADS):
        slope = jnp.zeros((rows, 1), F32)
        sink = jnp.zeros((rows, 1), F32)
        for hh in range(gsz):
            h = g * gsz + hh
            slope = jnp.where(hrow == hh, _alibi_slope(h), slope)
            sink = jnp.where(hrow == hh, sink_ref[h], sink)
        qg = jnp.concatenate([q[:, :, (g * gsz + hh) * HEAD_DIM:(g * gsz + hh + 1) * HEAD_DIM]
                              for hh in range(gsz)], axis=1).astype(BF16)
        lo, hi = g * HEAD_DIM, (g + 1) * HEAD_DIM
        ckg = ck[:, :, lo:hi].astype(BF16)
        cvg = cv[:, :, lo:hi].astype(BF16)
        kng = kn[:, :, lo:hi].astype(BF16)
        vng = vn[:, :, lo:hi].astype(BF16)
        nt = (((2,), (2,)), ((0,), (0,)))
        nn = (((2,), (1,)), ((0,), (0,)))
        s1 = lax.dot_general(qg, ckg, nt, preferred_element_type=F32) * (HEAD_DIM ** -0.5)
        s2 = lax.dot_general(qg, kng, nt, preferred_element_type=F32) * (HEAD_DIM ** -0.5)
        s1 = jnp.where(valid1[None], s1 - (slope * rel1.astype(F32))[None], -jnp.inf)
        s2 = jnp.where(valid2[None], s2 - (slope * rel2.astype(F32))[None], -jnp.inf)
        m = jnp.maximum(jnp.maximum(jnp.max(s1, axis=-1, keepdims=True),
                                    jnp.max(s2, axis=-1, keepdims=True)), sink[None])
        e1 = jnp.exp(s1 - m)
        e2 = jnp.exp(s2 - m)
        den = (jnp.sum(e1, axis=-1, keepdims=True) + jnp.sum(e2, axis=-1, keepdims=True)
               + jnp.exp(sink[None] - m))
        p1 = (e1 / den).astype(BF16)
        p2 = (e2 / den).astype(BF16)
        og = (lax.dot_general(p1, cvg, nn, preferred_element_type=F32)
              + lax.dot_general(p2, vng, nn, preferred_element_type=F32))
        outs.extend(og[:, hh * t:(hh + 1) * t, :] for hh in range(gsz))
    o_ref[...] = jnp.concatenate(outs, axis=2)
    nk_ref[...] = jnp.concatenate([ck[:, t:, :], kn], axis=1)
    nv_ref[...] = jnp.concatenate([cv[:, t:, :], vn], axis=1)


def _attn_sample(q, kn, vn, ck, cv, sinks):
    n, t, _ = q.shape
    wb = ck.shape[1]
    sn = _tile(n, 16)
    blk = lambda d1, d2: pl.BlockSpec((sn, d1, d2), lambda i: (i, 0, 0))
    return pl.pallas_call(
        functools.partial(_attn_sample_kernel, t=t),
        grid=(n // sn,),
        in_specs=[pl.BlockSpec(memory_space=pltpu.SMEM),
                  blk(t, D_MODEL), blk(t, KV_W), blk(t, KV_W), blk(wb, KV_W), blk(wb, KV_W)],
        out_specs=[blk(t, D_MODEL), blk(wb, KV_W), blk(wb, KV_W)],
        out_shape=[jax.ShapeDtypeStruct((n, t, D_MODEL), F32),
                   jax.ShapeDtypeStruct((n, wb, KV_W), F32),
                   jax.ShapeDtypeStruct((n, wb, KV_W), F32)],
        compiler_params=_cparams("parallel"),
        name="attn_sample",
    )(sinks, q, kn, vn, ck, cv)


def _rglru_gates(xc, wa_ref, ba, wx_ref, bx, lam):
    xb = xc.astype(BF16)
    npair = D_MODEL // LANES
    ga = jnp.concatenate([jnp.dot(xb[:, j * LANES:(j + 1) * LANES], wa_ref[j], preferred_element_type=F32)
                          for j in range(npair)], axis=1)
    gx = jnp.concatenate([jnp.dot(xb[:, j * LANES:(j + 1) * LANES], wx_ref[j], preferred_element_type=F32)
                          for j in range(npair)], axis=1)
    r = _sigmoid(ga + ba)
    ig = _sigmoid(gx + bx)
    log_a = -RG_C * r * _softplus(-lam)
    a = jnp.exp(log_a)
    u = jnp.sqrt(-jnp.tanh(log_a) * (a * a + 1.0)) * (ig * xc)
    return a, u


def _rglru_prompt_kernel(x_ref, cw_ref, cb_ref, wa_ref, ba_ref, wx_ref, bx_ref, lam_ref, h_ref,
                         tail_ref, hc_ref):
    i = pl.program_id(1)
    tc = x_ref.shape[0]

    @pl.when(i == 0)
    def _():
        tail_ref[...] = jnp.zeros_like(tail_ref)
        hc_ref[...] = jnp.zeros_like(hc_ref)

    x = x_ref[...]
    xp = jnp.concatenate([tail_ref[...], x], axis=0)
    cw = cw_ref[...]
    xc = cb_ref[...] + x * cw[CONV_W - 1:CONV_W]
    for d in range(1, CONV_W):
        xc = xc + xp[8 - d:8 - d + tc] * cw[CONV_W - 1 - d:CONV_W - d]
    a, u = _rglru_gates(xc, wa_ref, ba_ref[...], wx_ref, bx_ref[...], lam_ref[...])
    acc_a, acc_h = a, u
    d = 1
    while d < tc:
        sh_a = jnp.concatenate([jnp.ones((d, D_MODEL), F32), acc_a[:tc - d]], axis=0)
        sh_h = jnp.concatenate([jnp.zeros((d, D_MODEL), F32), acc_h[:tc - d]], axis=0)
        acc_h = acc_h + acc_a * sh_h
        acc_a = acc_a * sh_a
        d *= 2
    h = acc_h + acc_a * hc_ref[...]
    h_ref[...] = h
    hc_ref[...] = h[tc - 1:tc]
    tail_ref[...] = x[tc - 8:tc]


def _rglru_prompt(x, nb_seq, seq, cw, cb, wa2, ba, wx2, bx, lam):
    tc = _tile(seq, 256)
    nt = seq // tc
    row = lambda b, i: (b * nt + i, 0)
    full2 = lambda b, i: (0, 0)
    full3 = lambda b, i: (0, 0, 0)
    return pl.pallas_call(
        _rglru_prompt_kernel,
        grid=(nb_seq, nt),
        in_specs=[pl.BlockSpec((tc, D_MODEL), row),
                  pl.BlockSpec((CONV_W, D_MODEL), full2), pl.BlockSpec((1, D_MODEL), full2),
                  pl.BlockSpec(wa2.shape, full3), pl.BlockSpec((1, D_MODEL), full2),
                  pl.BlockSpec(wx2.shape, full3), pl.BlockSpec((1, D_MODEL), full2),
                  pl.BlockSpec((1, D_MODEL), full2)],
        out_specs=pl.BlockSpec((tc, D_MODEL), row),
        out_shape=jax.ShapeDtypeStruct((nb_seq * seq, D_MODEL), F32),
        scratch_shapes=[pltpu.VMEM((8, D_MODEL), F32), pltpu.VMEM((1, D_MODEL), F32)],
        compiler_params=_cparams("parallel", "arbitrary"),
        name="rglru_prompt",
    )(x, cw, cb, wa2, ba, wx2, bx, lam)


def _rglru_sample_kernel(x_ref, cs_ref, h0_ref, cw_ref, cb_ref, wa_ref, ba_ref, wx_ref, bx_ref, lam_ref,
                         h_ref):
    t, sn, _ = x_ref.shape
    cw = cw_ref[...]
    nst = CONV_W - 1

    def slab(s):
        return x_ref[s] if s >= 0 else cs_ref[nst + s]

    xcs = []
    for s in range(t):
        xc = cb_ref[...] + slab(s) * cw[CONV_W - 1:CONV_W]
        for d in range(1, CONV_W):
            xc = xc + slab(s - d) * cw[CONV_W - 1 - d:CONV_W - d]
        xcs.append(xc)
    xc_all = jnp.concatenate(xcs, axis=0)
    a, u = _rglru_gates(xc_all, wa_ref, ba_ref[...], wx_ref, bx_ref[...], lam_ref[...])
    h = h0_ref[...]
    for s in range(t):
        h = a[s * sn:(s + 1) * sn] * h + u[s * sn:(s + 1) * sn]
        h_ref[s] = h


def _rglru_sample(x, cs, h0, cw, cb, wa2, ba, wx2, bx, lam):
    t, n, _ = x.shape
    sn = _tile(n, 32)
    full2 = lambda i: (0, 0)
    full3 = lambda i: (0, 0, 0)
    return pl.pallas_call(
        _rglru_sample_kernel,
        grid=(n // sn,),
        in_specs=[pl.BlockSpec((t, sn, D_MODEL), lambda i: (0, i, 0)),
                  pl.BlockSpec((CONV_W - 1, sn, D_MODEL), lambda i: (0, i, 0)),
                  pl.BlockSpec((sn, D_MODEL), lambda i: (i, 0)),
                  pl.BlockSpec((CONV_W, D_MODEL), full2), pl.BlockSpec((1, D_MODEL), full2),
                  pl.BlockSpec(wa2.shape, full3), pl.BlockSpec((1, D_MODEL), full2),
                  pl.BlockSpec(wx2.shape, full3), pl.BlockSpec((1, D_MODEL), full2),
                  pl.BlockSpec((1, D_MODEL), full2)],
        out_specs=pl.BlockSpec((t, sn, D_MODEL), lambda i: (0, i, 0)),
        out_shape=jax.ShapeDtypeStruct((t, n, D_MODEL), F32),
        compiler_params=_cparams("parallel"),
        name="rglru_sample",
    )(x, cs, h0, cw, cb, wa2, ba, wx2, bx, lam)


def _rwkv_prep_math(pc, prev, mu, w0, a0, kkp, ka, wwa, gup):
    d = D_MODEL
    ps = pc + mu * (prev - pc)
    r = ps[:, 0:d]
    k = ps[:, d:2 * d]
    v = ps[:, 2 * d:3 * d]
    l01 = ps[:, 3 * d:3 * d + LANES]
    lane = lax.broadcasted_iota(jnp.int32, (1, LANES), 1)
    t01 = jnp.where(lane < LORA_W, jnp.tanh(l01), l01)
    wa = _dot(t01, wwa)
    g = _dot(_sigmoid(ps[:, 3 * d + LANES:]), gup)
    wlog = -_softplus(-(w0 + wa[:, :d])) - 0.5
    logw = -jnp.exp(wlog)
    a = _sigmoid(a0 + wa[:, d:])
    kk = k * kkp
    e, et = _head_indicator()
    nrm = jnp.maximum(jnp.sqrt(_head_sum(kk * kk, e, et)), 1e-12)
    kkn = kk / nrm
    kmod = k * (1.0 + (a - 1.0) * ka)
    return r, kmod, v, logw, kkn, kkn * a, g


def _rwkv_prep_prompt_kernel(pc_ref, pv_ref, mu_ref, w0_ref, a0_ref, kkp_ref, ka_ref, wwa_ref, gup_ref,
                             *out_refs):
    i = pl.program_id(1)
    pc = pc_ref[...]
    first = jnp.where(i == 0, 0.0, 1.0) * pv_ref[7:8, :]
    prev = jnp.concatenate([first, pc[:-1]], axis=0)
    outs = _rwkv_prep_math(pc, prev, mu_ref[...], w0_ref[...], a0_ref[...], kkp_ref[...], ka_ref[...],
                           wwa_ref[...], gup_ref[...])
    for o_ref, o in zip(out_refs, outs):
        o_ref[...] = o


def _rwkv_prep_sample_kernel(pc_ref, pv_ref, sh_ref, mu_ref, w0_ref, a0_ref, kkp_ref, ka_ref, wwa_ref,
                             gup_ref, *out_refs):
    s = pl.program_id(0)
    prev = jnp.where(s == 0, sh_ref[...], pv_ref[...])
    outs = _rwkv_prep_math(pc_ref[...], prev, mu_ref[...], w0_ref[...], a0_ref[...], kkp_ref[...],
                           ka_ref[...], wwa_ref[...], gup_ref[...])
    for o_ref, o in zip(out_refs, outs):
        o_ref[...] = o


_N_PREP_OUT = 7


def _rwkv_prep_prompt(pc, nb_seq, seq, params):
    tt = _tile(seq, 256)
    nt = seq // tt
    row = lambda b, i: (b * nt + i, 0)
    prev8 = lambda b, i: (jnp.maximum((b * nt + i) * (tt // 8) - 1, 0), 0)
    full = lambda b, i: (0, 0)
    return pl.pallas_call(
        _rwkv_prep_prompt_kernel,
        grid=(nb_seq, nt),
        in_specs=[pl.BlockSpec((tt, SHIFT_PAD), row), pl.BlockSpec((8, SHIFT_PAD), prev8)]
                 + [pl.BlockSpec(p.shape, full) for p in params],
        out_specs=[pl.BlockSpec((tt, D_MODEL), row)] * _N_PREP_OUT,
        out_shape=[jax.ShapeDtypeStruct((nb_seq * seq, D_MODEL), F32)] * _N_PREP_OUT,
        compiler_params=_cparams("parallel", "parallel"),
        name="rwkv_prep_prompt",
    )(pc, pc, *params)


def _rwkv_prep_sample(pc, shift, t, n, params):
    full = lambda s: (0, 0)
    return pl.pallas_call(
        _rwkv_prep_sample_kernel,
        grid=(t,),
        in_specs=[pl.BlockSpec((n, SHIFT_PAD), lambda s: (s, 0)),
                  pl.BlockSpec((n, SHIFT_PAD), lambda s: (jnp.maximum(s - 1, 0), 0)),
                  pl.BlockSpec((n, SHIFT_PAD), full)]
                 + [pl.BlockSpec(p.shape, full) for p in params],
        out_specs=[pl.BlockSpec((n, D_MODEL), lambda s: (s, 0))] * _N_PREP_OUT,
        out_shape=[jax.ShapeDtypeStruct((t * n, D_MODEL), F32)] * _N_PREP_OUT,
        compiler_params=_cparams("parallel"),
        name="rwkv_prep_sample",
    )(pc, pc, shift, *params)


_NT = (((2,), (2,)), ((0,), (0,)))
_NN = (((2,), (1,)), ((0,), (0,)))
_TN = (((1,), (1,)), ((0,), (0,)))


def _bmm(a, b, dn, passes=1):
    if passes == 1:
        return lax.dot_general(a.astype(BF16), b.astype(BF16), dn, preferred_element_type=F32)
    ah, al = _split_terms(a, 2)
    bh, bl = _split_terms(b, 2)
    return (lax.dot_general(ah, bh, dn, preferred_element_type=F32)
            + lax.dot_general(ah, bl, dn, preferred_element_type=F32)
            + lax.dot_general(al, bh, dn, preferred_element_type=F32))


def _to_heads(x):
    return jnp.stack([x[:, h * HEAD_DIM:(h + 1) * HEAD_DIM] for h in range(N_HEADS)], axis=0)


def _wkv_prompt_kernel(r_ref, k_ref, v_ref, lw_ref, kk_ref, b_ref, y_ref, s_ref, h_ref):
    i = pl.program_id(1)
    c = r_ref.shape[0]
    hd = HEAD_DIM

    @pl.when(i == 0)
    def _():
        h_ref[...] = jnp.zeros_like(h_ref)

    row = lax.broadcasted_iota(jnp.int32, (c, c), 0)
    col = lax.broadcasted_iota(jnp.int32, (c, c), 1)
    incl = row >= col
    strict = row > col
    lw = lw_ref[...]
    cum = _dot_rhs_split(jnp.where(incl, 1.0, 0.0).astype(BF16), lw, 3)
    last = cum[c - 1:c]
    g_out = jnp.exp(cum)
    g_in = jnp.exp(cum - lw)
    g_inv = jnp.exp(-cum)
    g_end = jnp.exp(last - cum)
    kk = kk_ref[...]
    b = b_ref[...]
    k = k_ref[...]
    at = _to_heads(-(kk * g_in))
    rt = _to_heads(r_ref[...] * g_out)
    bt = _to_heads(b * g_inv)
    kt = _to_heads(k * g_inv)
    bh = _to_heads(b * g_end)
    kh = _to_heads(k * g_end)
    v = _to_heads(v_ref[...])
    gc = _to_heads(jnp.exp(last))

    amat = _bmm(jnp.concatenate([at, rt], axis=1), jnp.concatenate([bt, kt], axis=1), _NT, 3)
    a_ab = jnp.where(strict[None], amat[:, :c, :c], 0.0)
    a_ak = jnp.where(strict[None], amat[:, :c, c:], 0.0)
    a_rb = jnp.where(incl[None], amat[:, c:, :c], 0.0)
    a_rk = jnp.where(incl[None], amat[:, c:, c:], 0.0)
    eye_c = jnp.where(row == col, 1.0, 0.0)
    tinv = eye_c[None] + a_ab
    npow = a_ab
    span = 2
    while span < c:
        npow = _bmm(npow, npow, _NN, 3)
        tinv = tinv + _bmm(tinv, npow, _NN, 3)
        span *= 2
    w12 = _bmm(jnp.concatenate([a_ak, a_rk], axis=1), v, _NN)
    pq = _bmm(tinv, jnp.concatenate([at, w12[:, :c]], axis=2), _NN, 3)
    y10 = jnp.concatenate([rt, w12[:, c:]], axis=2) + _bmm(a_rb, pq, _NN)
    mn = _bmm(bh, pq, _TN, 3)
    kv = _bmm(kh, v, _TN)
    er = lax.broadcasted_iota(jnp.int32, (hd, hd), 0)
    ec = lax.broadcasted_iota(jnp.int32, (hd, hd), 1)
    m = mn[:, :, :hd] + jnp.where((er == ec)[None], gc, 0.0)
    n = mn[:, :, hd:] + kv
    h0 = h_ref[...]
    my = _bmm(jnp.concatenate([m, y10[:, :, :hd]], axis=1), h0, _NN, 3)
    h_new = my[:, :hd] + n
    h_ref[...] = h_new
    y = my[:, hd:] + y10[:, :, hd:]
    y_ref[...] = jnp.concatenate([y[h] for h in range(N_HEADS)], axis=1)

    @pl.when(i == pl.num_programs(1) - 1)
    def _():
        s_ref[0] = h_new


def _wkv_prompt(r, k, v, lw, kk, b, nb_seq, seq):
    c = _tile(seq, WKV_CHUNK)
    nc = seq // c
    row = lambda bb, i: (bb * nc + i, 0)
    return pl.pallas_call(
        _wkv_prompt_kernel,
        grid=(nb_seq, nc),
        in_specs=[pl.BlockSpec((c, D_MODEL), row)] * 6,
        out_specs=[pl.BlockSpec((c, D_MODEL), row),
                   pl.BlockSpec((1, N_HEADS, HEAD_DIM, HEAD_DIM), lambda bb, i: (bb, 0, 0, 0))],
        out_shape=[jax.ShapeDtypeStruct((nb_seq * seq, D_MODEL), F32),
                   jax.ShapeDtypeStruct((nb_seq, N_HEADS, HEAD_DIM, HEAD_DIM), F32)],
        scratch_shapes=[pltpu.VMEM((N_HEADS, HEAD_DIM, HEAD_DIM), F32)],
        compiler_params=_cparams("parallel", "arbitrary"),
        name="wkv_prompt",
    )(r, k, v, lw, kk, b)


_WKV_VROWS = 4


def _wkv_sample_kernel(r_ref, lw_ref, k_ref, v_ref, kk_ref, b_ref, s0_ref, y_ref, s_ref):
    t = r_ref.shape[1]
    vr = _WKV_VROWS

    def body(vi, carry):
        v0 = pl.multiple_of(vi * vr, vr)
        s = s0_ref[0, pl.ds(v0, vr)]
        for st in range(t):
            sa = -jnp.sum(s * kk_ref[0, st][None], axis=1, keepdims=True)
            vv = v_ref[0, st, pl.ds(v0, vr)]
            s = s * jnp.exp(lw_ref[0, st])[None] + sa * b_ref[0, st][None] + vv * k_ref[0, st][None]
            y_ref[0, st, pl.ds(v0, vr)] = jnp.sum(s * r_ref[0, st][None], axis=1, keepdims=True)
        s_ref[0, pl.ds(v0, vr)] = s
        return carry

    lax.fori_loop(0, HEAD_DIM // vr, body, 0)


def _wkv_sample(r, lw, k, v, kk, b, s0):
    nh, t, hd, n = r.shape
    vec = pl.BlockSpec((1, t, hd, n), lambda h: (h, 0, 0, 0))
    vec1 = pl.BlockSpec((1, t, hd, 1, n), lambda h: (h, 0, 0, 0, 0))
    st = pl.BlockSpec((1, hd, hd, n), lambda h: (h, 0, 0, 0))
    return pl.pallas_call(
        _wkv_sample_kernel,
        grid=(nh,),
        in_specs=[vec, vec, vec, vec1, vec, vec, st],
        out_specs=[vec1, st],
        out_shape=[jax.ShapeDtypeStruct((nh, t, hd, 1, n), F32),
                   jax.ShapeDtypeStruct((nh, hd, hd, n), F32)],
        compiler_params=_cparams("parallel"),
        name="wkv_sample",
    )(r, lw, k, v, kk, b, s0)


def _mix_out_kernel(x_ref, oa_ref, ob_ref, y_ref, r_ref, k_ref, v_ref, g_ref, gate_ref,
                    lnw_ref, lnb_ref, rk_ref, wout_ref, g1_ref, b1_ref, o_ref, *, alpha):
    d = D_MODEL
    e, et = _head_indicator()
    y = y_ref[...]
    mu = _head_sum(y, e, et) * (1.0 / HEAD_DIM)
    yc = y - mu
    var = _head_sum(yc * yc, e, et) * (1.0 / HEAD_DIM)
    yn = yc * lax.rsqrt(var + GN_EPS) * lnw_ref[...] + lnb_ref[...]
    v = v_ref[...]
    bonus = _head_sum(r_ref[...] * k_ref[...] * rk_ref[...], e, et) * v
    oc = (yn + bonus) * g_ref[...]
    gate = gate_ref[...]
    mixed = (_sigmoid(gate[:, 0:d]) * oa_ref[...] + _sigmoid(gate[:, d:2 * d]) * ob_ref[...]
             + _sigmoid(gate[:, 2 * d:3 * d]) * oc)
    z = alpha * x_ref[...] + jnp.dot(mixed.astype(BF16), wout_ref[...], preferred_element_type=F32)
    o_ref[...] = _layer_norm(z, g1_ref[...], b1_ref[...])


def _mix_out(x, oa, ob, y, r, k, v, g, gate, lnw, lnb, rk, wout, g1, b1, alpha):
    m = x.shape[0]
    tm = _tile(m, 256)
    row = lambda i: (i, 0)
    full = lambda i: (0, 0)
    rows = pl.BlockSpec((tm, D_MODEL), row)
    vec = pl.BlockSpec((1, D_MODEL), full)
    return pl.pallas_call(
        functools.partial(_mix_out_kernel, alpha=alpha),
        grid=(m // tm,),
        in_specs=[rows] * 8 + [pl.BlockSpec((tm, 3 * D_MODEL), row), vec, vec, vec,
                               pl.BlockSpec((D_MODEL, D_MODEL), full), vec, vec],
        out_specs=rows,
        out_shape=jax.ShapeDtypeStruct((m, D_MODEL), F32),
        compiler_params=_cparams("parallel"),
        name="mix_out",
    )(x, oa, ob, y, r, k, v, g, gate, lnw, lnb, rk, wout, g1, b1)


def _ffn_kernel(x_ref, wg_ref, wu_ref, wd_ref, g2_ref, b2_ref, o_ref, acc_ref, *, alpha):
    f = pl.program_id(1)
    x = x_ref[...]
    xb = x.astype(BF16)
    gt = jnp.dot(xb, wg_ref[...], preferred_element_type=F32)
    up = jnp.dot(xb, wu_ref[...], preferred_element_type=F32)
    part = jnp.dot((gt * _sigmoid(gt) * up).astype(BF16), wd_ref[...], preferred_element_type=F32)

    @pl.when(f == 0)
    def _():
        acc_ref[...] = part

    @pl.when(f > 0)
    def _():
        acc_ref[...] += part

    @pl.when(f == pl.num_programs(1) - 1)
    def _():
        o_ref[...] = _layer_norm(alpha * x + acc_ref[...], g2_ref[...], b2_ref[...])


def _ffn(x, wg, wu, wd, g2, b2, alpha):
    m = x.shape[0]
    dff = wg.shape[1]
    tm = _tile(m, 512)
    tf = dff // 2 if (dff // 2) % LANES == 0 else dff
    row = lambda i, f: (i, 0)
    full = lambda i, f: (0, 0)
    return pl.pallas_call(
        functools.partial(_ffn_kernel, alpha=alpha),
        grid=(m // tm, dff // tf),
        in_specs=[pl.BlockSpec((tm, D_MODEL), row),
                  pl.BlockSpec((D_MODEL, tf), lambda i, f: (0, f)),
                  pl.BlockSpec((D_MODEL, tf), lambda i, f: (0, f)),
                  pl.BlockSpec((tf, D_MODEL), lambda i, f: (f, 0)),
                  pl.BlockSpec((1, D_MODEL), full), pl.BlockSpec((1, D_MODEL), full)],
        out_specs=pl.BlockSpec((tm, D_MODEL), row),
        out_shape=jax.ShapeDtypeStruct((m, D_MODEL), F32),
        scratch_shapes=[pltpu.VMEM((tm, D_MODEL), F32)],
        compiler_params=_cparams("parallel", "arbitrary"),
        name="ffn",
    )(x, wg, wu, wd, g2, b2)


def _blockdiag_pairs(w):
    nb, bs, _ = w.shape
    w = w.reshape(nb // 2, 2, bs, bs)
    z = jnp.zeros((nb // 2, bs, bs), w.dtype)
    top = jnp.concatenate([w[:, 0], z], axis=2)
    bot = jnp.concatenate([z, w[:, 1]], axis=2)
    return jnp.concatenate([top, bot], axis=1)


def _row(p):
    return p.reshape(1, -1)


def kernel(x_prompt, x_sample, cache_k, cache_v, state_conv, state_rglru, state_shift, state_wkv, w_in, attn_sinks, conv_w, conv_b, rg_wa, rg_ba, rg_wx, rg_bx, rg_lambda, rw_mu, rw_w0, rw_wup, rw_a0, rw_aup, rw_gup, rw_kk, rw_ka, rw_rk, rw_lnw, rw_lnb, w_out, ln1_g, ln1_b, w_gu, w_down, ln2_g, ln2_b):
    depth = w_in.shape[0]
    bp, seq, d = x_prompt.shape
    ns, ts, _ = x_sample.shape
    wb = cache_k.shape[2]
    alpha = (2 * depth) ** 0.25
    dff = w_down.shape[1]
    pad = SHIFT_PAD - SHIFT_W

    xp = x_prompt.reshape(bp * seq, d)
    xs = jnp.swapaxes(x_sample, 0, 1).reshape(ts * ns, d)
    st_p = [[] for _ in range(6)]
    st_s = [[] for _ in range(6)]

    for l in range(depth):
        wl = w_in[l]
        w_qkv = wl[:, :OFF_RNN].astype(BF16)
        w_rnn = wl[:, OFF_RNN:OFF_RW].astype(BF16)
        w_rw = jnp.pad(wl[:, OFF_RW:OFF_GATE], ((0, 0), (0, pad))).astype(BF16)
        w_gate = wl[:, OFF_GATE:].astype(BF16)
        wa2 = _blockdiag_pairs(rg_wa[l]).astype(BF16)
        wx2 = _blockdiag_pairs(rg_wx[l]).astype(BF16)
        rg_args = (conv_w[l], _row(conv_b[l]), wa2, _row(rg_ba[l]), wx2, _row(rg_bx[l]), _row(rg_lambda[l]))
        zw = jnp.zeros((LORA_W, d), F32)
        wwa = jnp.concatenate([jnp.concatenate([rw_wup[l], zw], axis=1),
                               jnp.concatenate([zw, rw_aup[l]], axis=1)], axis=0).astype(BF16)
        gup = jnp.pad(rw_gup[l], ((0, 2 * LANES - LORA_G), (0, 0))).astype(BF16)
        prep_params = (_row(jnp.pad(rw_mu[l], (0, pad))), _row(rw_w0[l]), _row(rw_a0[l]), _row(rw_kk[l]),
                       _row(rw_ka[l]), wwa, gup)
        mix_params = (_row(rw_lnw[l]), _row(rw_lnb[l]), _row(rw_rk[l]), w_out[l].astype(BF16),
                      _row(ln1_g[l]), _row(ln1_b[l]))
        wg = w_gu[l][:, :dff].astype(BF16)
        wu = w_gu[l][:, dff:].astype(BF16)
        wd = w_down[l].astype(BF16)
        ffn_params = (wg, wu, wd, _row(ln2_g[l]), _row(ln2_b[l]))
        sinks = attn_sinks[l]

        def project(x):
            return (_matmul(x, w_qkv, OFF_RNN), _matmul(x, w_rnn, D_MODEL),
                    _matmul(x, w_rw, SHIFT_PAD // 3), _matmul(x, w_gate, 3 * D_MODEL // 2))

        qkv, rnn, rw, gate = project(xp)
        oa = _attn_prompt(qkv, sinks, bp, seq)
        ob = _rglru_prompt(rnn, bp, seq, *rg_args)
        r, k, v, lw, kk, b, g = _rwkv_prep_prompt(rw, bp, seq, prep_params)
        y, hfin = _wkv_prompt(r, k, v, lw, kk, b, bp, seq)
        x1 = _mix_out(xp, oa, ob, y, r, k, v, g, gate, *mix_params, alpha)
        xp = _ffn(x1, *ffn_params, alpha)
        qkv3 = qkv.reshape(bp, seq, OFF_RNN)
        st_p[0].append(qkv3[:, seq - wb:, OFF_K:OFF_V].reshape(bp, wb, N_KV_HEADS, HEAD_DIM))
        st_p[1].append(qkv3[:, seq - wb:, OFF_V:OFF_RNN].reshape(bp, wb, N_KV_HEADS, HEAD_DIM))
        st_p[2].append(rnn.reshape(bp, seq, d)[:, seq - (CONV_W - 1):])
        st_p[3].append(ob.reshape(bp, seq, d)[:, seq - 1])
        st_p[4].append(rw.reshape(bp, seq, SHIFT_PAD)[:, seq - 1, :SHIFT_W])
        st_p[5].append(jnp.swapaxes(hfin, 2, 3))

        qkv, rnn, rw, gate = project(xs)
        qkv_n = jnp.swapaxes(qkv.reshape(ts, ns, OFF_RNN), 0, 1)
        oa_n, new_k, new_v = _attn_sample(qkv_n[:, :, :OFF_K], qkv_n[:, :, OFF_K:OFF_V], qkv_n[:, :, OFF_V:],
                                          cache_k[l].reshape(ns, wb, KV_W), cache_v[l].reshape(ns, wb, KV_W),
                                          sinks)
        oa = jnp.swapaxes(oa_n, 0, 1).reshape(ts * ns, d)
        rnn3 = rnn.reshape(ts, ns, d)
        ob3 = _rglru_sample(rnn3, jnp.swapaxes(state_conv[l], 0, 1), state_rglru[l], *rg_args)
        ob = ob3.reshape(ts * ns, d)
        shift = jnp.pad(state_shift[l], ((0, 0), (0, pad)))
        r, k, v, lw, kk, b, g = _rwkv_prep_sample(rw, shift, ts, ns, prep_params)

        def lanes(z):
            return jnp.transpose(z.reshape(ts, ns, N_HEADS, HEAD_DIM), (2, 0, 3, 1))

        s0 = jnp.transpose(state_wkv[l], (1, 2, 3, 0))
        y5, s_new = _wkv_sample(lanes(r), lanes(lw), lanes(k),
                                lanes(v).reshape(N_HEADS, ts, HEAD_DIM, 1, ns), lanes(kk), lanes(b), s0)
        y = jnp.transpose(y5.reshape(N_HEADS, ts, HEAD_DIM, ns), (1, 3, 0, 2)).reshape(ts * ns, d)
        x1 = _mix_out(xs, oa, ob, y, r, k, v, g, gate, *mix_params, alpha)
        xs = _ffn(x1, *ffn_params, alpha)
        st_s[0].append(new_k.reshape(ns, wb, N_KV_HEADS, HEAD_DIM))
        st_s[1].append(new_v.reshape(ns, wb, N_KV_HEADS, HEAD_DIM))
        st_s[2].append(jnp.swapaxes(rnn3[ts - (CONV_W - 1):], 0, 1))
        st_s[3].append(ob3[ts - 1])
        st_s[4].append(rw.reshape(ts, ns, SHIFT_PAD)[ts - 1, :, :SHIFT_W])
        st_s[5].append(jnp.transpose(s_new, (3, 0, 1, 2)))

    yp = xp.reshape(bp, seq, d)
    ys = jnp.swapaxes(xs.reshape(ts, ns, d), 0, 1)
    return (yp, ys, *(jnp.stack(s) for s in st_p), *(jnp.stack(s) for s in st_s))
```

```python
import functools
import math

import jax
import jax.numpy as jnp
from jax import lax
from jax.experimental import pallas as pl
from jax.experimental.pallas import tpu as pltpu

F32 = jnp.float32
BF16 = jnp.bfloat16

D_MODEL = 1024
HEAD_DIM = 64
N_HEADS = 16
N_KV_HEADS = 4
KV_W = N_KV_HEADS * HEAD_DIM
WINDOW = 128
CONV_W = 4
RG_C = 8.0
LORA_W = 64
LORA_A = 64
LORA_G = 160
SHIFT_W = 3 * D_MODEL + LORA_W + LORA_A + LORA_G
LANES = 128
SHIFT_PAD = -(-SHIFT_W // LANES) * LANES
GN_EPS = 64e-5
LN_EPS = 1e-5
OFF_K = D_MODEL
OFF_V = OFF_K + KV_W
OFF_RNN = OFF_V + KV_W
OFF_RW = OFF_RNN + D_MODEL
OFF_GATE = OFF_RW + SHIFT_W
WKV_CHUNK = 64
VMEM_LIMIT = 48 * 1024 * 1024
VMEM_LIMIT_PROJ = 56 * 1024 * 1024


def _cparams(*sem):
    return pltpu.CompilerParams(dimension_semantics=sem, vmem_limit_bytes=VMEM_LIMIT)


def _tile(n, pref):
    if n <= pref:
        return n
    for t in range(pref, 7, -1):
        if n % t == 0 and t % 8 == 0:
            return t
    return n


def _dot(a, b):
    return jnp.dot(a.astype(BF16), b.astype(BF16), preferred_element_type=F32)


def _split_terms(x, n):
    terms = []
    rem = x
    for _ in range(n):
        hi = rem.astype(BF16)
        terms.append(hi)
        rem = rem - hi.astype(F32)
    return terms


def _dot_lhs_split(x, w_bf16, n):
    acc = None
    for t in _split_terms(x, n):
        p = jnp.dot(t, w_bf16, preferred_element_type=F32)
        acc = p if acc is None else acc + p
    return acc


def _dot_rhs_split(w_bf16, x, n):
    acc = None
    for t in _split_terms(x, n):
        p = jnp.dot(w_bf16, t, preferred_element_type=F32)
        acc = p if acc is None else acc + p
    return acc


def _sigmoid(x):
    return 0.5 * jnp.tanh(0.5 * x) + 0.5


def _softplus(x):
    return jnp.maximum(x, 0.0) + jnp.log(1.0 + jnp.exp(-jnp.abs(x)))


def _layer_norm(z, g, b):
    mu = jnp.mean(z, axis=-1, keepdims=True)
    zc = z - mu
    var = jnp.mean(zc * zc, axis=-1, keepdims=True)
    return zc * lax.rsqrt(var + LN_EPS) * g + b


def _head_indicator():
    c = lax.broadcasted_iota(jnp.int32, (D_MODEL, LANES), 0) // HEAD_DIM
    h = lax.broadcasted_iota(jnp.int32, (D_MODEL, LANES), 1)
    e = jnp.where(c == h, 1.0, 0.0).astype(BF16)
    ct = lax.broadcasted_iota(jnp.int32, (LANES, D_MODEL), 1) // HEAD_DIM
    ht = lax.broadcasted_iota(jnp.int32, (LANES, D_MODEL), 0)
    et = jnp.where(ct == ht, 1.0, 0.0).astype(BF16)
    return e, et


def _head_sum(x, e, et):
    s = _dot_lhs_split(x, e, 2)
    return _dot_lhs_split(s, et, 2)


_PROJ_WIDTHS = (OFF_RNN, D_MODEL, SHIFT_PAD, 3 * D_MODEL)


def _in_proj_kernel(x_ref, w_ref, *o_refs):
    xb = x_ref[...].astype(BF16)
    off = 0
    for o_ref in o_refs:
        n = o_ref.shape[1]
        o_ref[...] = jnp.dot(xb, w_ref[:, off:off + n], preferred_element_type=F32)
        off += n


def _in_proj(x, w):
    m, k = x.shape
    tm = _tile(m, 256)
    return pl.pallas_call(
        _in_proj_kernel,
        grid=(m // tm,),
        in_specs=[pl.BlockSpec((tm, k), lambda i: (i, 0)),
                  pl.BlockSpec(w.shape, lambda i: (0, 0), pipeline_mode=pl.Buffered(1))],
        out_specs=[pl.BlockSpec((tm, n), lambda i: (i, 0)) for n in _PROJ_WIDTHS],
        out_shape=[jax.ShapeDtypeStruct((m, n), F32) for n in _PROJ_WIDTHS],
        compiler_params=pltpu.CompilerParams(dimension_semantics=("parallel",),
                                             vmem_limit_bytes=VMEM_LIMIT_PROJ),
        name="in_proj",
    )(x, w)


def _alibi_slope(h):
    return 2.0 ** (-8.0 * (h + 1) / N_HEADS)


def _attn_prompt_bias():
    w = WINDOW
    rel = w + jnp.arange(w)[:, None] - jnp.arange(2 * w)[None, :]
    band = (rel >= 0) & (rel < WINDOW)
    has_prev = jnp.arange(2)[:, None, None] > 0
    valid = band[None] & (has_prev | (jnp.arange(2 * w) >= w)[None, None, :])
    slopes = jnp.array([_alibi_slope(h) for h in range(N_HEADS)], F32)
    bias = -slopes[None, :, None, None] * rel.astype(F32)[None, None]
    return jnp.where(valid[:, None], bias, -jnp.inf)


def _attn_prompt_kernel(sink_ref, bias_ref, q_ref, kc_ref, vc_ref, kp_ref, vp_ref, o_ref):
    q = q_ref[...] * (HEAD_DIM ** -0.5)
    kcat = jnp.concatenate([kp_ref[...], kc_ref[...]], axis=0).astype(BF16)
    vcat = jnp.concatenate([vp_ref[...], vc_ref[...]], axis=0).astype(BF16)
    outs = []
    for h in range(N_HEADS):
        g = h // (N_HEADS // N_KV_HEADS)
        qh = q[:, h * HEAD_DIM:(h + 1) * HEAD_DIM].astype(BF16)
        kg = kcat[:, g * HEAD_DIM:(g + 1) * HEAD_DIM]
        vg = vcat[:, g * HEAD_DIM:(g + 1) * HEAD_DIM]
        s = lax.dot_general(qh, kg, (((1,), (1,)), ((), ())), preferred_element_type=F32) + bias_ref[0, h]
        sink = sink_ref[h]
        m = jnp.maximum(jnp.max(s, axis=-1, keepdims=True), sink)
        e = jnp.exp(s - m)
        den = jnp.sum(e, axis=-1, keepdims=True) + jnp.exp(sink - m)
        p = e / den
        outs.append(jnp.dot(p.astype(BF16), vg, preferred_element_type=F32))
    o_ref[...] = jnp.concatenate(outs, axis=1)


def _attn_prompt(qkv, sinks, nb_seq, seq):
    w = WINDOW
    nb = seq // w
    kcol = OFF_K // KV_W
    vcol = OFF_V // KV_W

    def cur(c):
        return lambda b, j: (b * nb + j, c)

    def prev(c):
        return lambda b, j: (jnp.maximum(b * nb + j - 1, 0), c)

    return pl.pallas_call(
        _attn_prompt_kernel,
        grid=(nb_seq, nb),
        in_specs=[pl.BlockSpec(memory_space=pltpu.SMEM),
                  pl.BlockSpec((1, N_HEADS, w, 2 * w), lambda b, j: (jnp.minimum(j, 1), 0, 0, 0)),
                  pl.BlockSpec((w, D_MODEL), cur(0)),
                  pl.BlockSpec((w, KV_W), cur(kcol)),
                  pl.BlockSpec((w, KV_W), cur(vcol)),
                  pl.BlockSpec((w, KV_W), prev(kcol)),
                  pl.BlockSpec((w, KV_W), prev(vcol))],
        out_specs=pl.BlockSpec((w, D_MODEL), cur(0)),
        out_shape=jax.ShapeDtypeStruct((nb_seq * seq, D_MODEL), F32),
        compiler_params=_cparams("parallel", "parallel"),
        name="attn_prompt",
    )(sinks, _attn_prompt_bias(), qkv, qkv, qkv, qkv, qkv)


def _attn_sample_kernel(sink_ref, q_ref, kn_ref, vn_ref, ck_ref, cv_ref, o_ref, nk_ref, nv_ref, *, t):
    sn = q_ref.shape[0]
    wb = ck_ref.shape[1]
    gsz = N_HEADS // N_KV_HEADS
    rows = gsz * t
    q = q_ref[...]
    kn = kn_ref[...]
    vn = vn_ref[...]
    ck = ck_ref[...]
    cv = cv_ref[...]
    tq1 = lax.broadcasted_iota(jnp.int32, (rows, wb), 0) % t
    j1 = lax.broadcasted_iota(jnp.int32, (rows, wb), 1)
    rel1 = wb + tq1 - j1
    valid1 = (rel1 >= 0) & (rel1 < WINDOW)
    tq2 = lax.broadcasted_iota(jnp.int32, (rows, t), 0) % t
    s2i = lax.broadcasted_iota(jnp.int32, (rows, t), 1)
    rel2 = tq2 - s2i
    valid2 = (rel2 >= 0) & (rel2 < WINDOW)
    hrow = lax.broadcasted_iota(jnp.int32, (rows, 1), 0) // t
    outs = []
    for g in range(N_KV_HEADS):
        slope = jnp.zeros((rows, 1), F32)
        sink = jnp.zeros((rows, 1), F32)
        for hh in range(gsz):
            h = g * gsz + hh
            slope = jnp.where(hrow == hh, _alibi_slope(h), slope)
            sink = jnp.where(hrow == hh, sink_ref[h], sink)
        qg = jnp.concatenate([q[:, :, (g * gsz + hh) * HEAD_DIM:(g * gsz + hh + 1) * HEAD_DIM]
                              for hh in range(gsz)], axis=1).astype(BF16)
        lo, hi = g * HEAD_DIM, (g + 1) * HEAD_DIM
        ckg = ck[:, :, lo:hi].astype(BF16)
        cvg = cv[:, :, lo:hi].astype(BF16)
        kng = kn[:, :, lo:hi].astype(BF16)
        vng = vn[:, :, lo:hi].astype(BF16)
        nt = (((2,), (2,)), ((0,), (0,)))
        nn = (((2,), (1,)), ((0,), (0,)))
        s1 = lax.dot_general(qg, ckg, nt, preferred_element_type=F32) * (HEAD_DIM ** -0.5)
        s2 = lax.dot_general(qg, kng, nt, preferred_element_type=F32) * (HEAD_DIM ** -0.5)
        s1 = jnp.where(valid1[None], s1 - (slope * rel1.astype(F32))[None], -jnp.inf)
        s2 = jnp.where(valid2[None], s2 - (slope * rel2.astype(F32))[None], -jnp.inf)
        m = jnp.maximum(jnp.maximum(jnp.max(s1, axis=-1, keepdims=True),
                                    jnp.max(s2, axis=-1, keepdims=True)), sink[None])
        e1 = jnp.exp(s1 - m)
        e2 = jnp.exp(s2 - m)
        den = (jnp.sum(e1, axis=-1, keepdims=True) + jnp.sum(e2, axis=-1, keepdims=True)
               + jnp.exp(sink[None] - m))
        p1 = (e1 / den).astype(BF16)
        p2 = (e2 / den).astype(BF16)
        og = (lax.dot_general(p1, cvg, nn, preferred_element_type=F32)
              + lax.dot_general(p2, vng, nn, preferred_element_type=F32))
        outs.extend(og[:, hh * t:(hh + 1) * t, :] for hh in range(gsz))
    o_ref[...] = jnp.concatenate(outs, axis=2)
    nk_ref[...] = jnp.concatenate([ck[:, t:, :], kn], axis=1)
    nv_ref[...] = jnp.concatenate([cv[:, t:, :], vn], axis=1)


def _attn_sample(q, kn, vn, ck, cv, sinks):
    n, t, _ = q.shape
    wb = ck.shape[1]
    sn = _tile(n, 16)
    blk = lambda d1, d2: pl.BlockSpec((sn, d1, d2), lambda i: (i, 0, 0))
    return pl.pallas_call(
        functools.partial(_attn_sample_kernel, t=t),
        grid=(n // sn,),
        in_specs=[pl.BlockSpec(memory_space=pltpu.SMEM),
                  blk(t, D_MODEL), blk(t, KV_W), blk(t, KV_W), blk(wb, KV_W), blk(wb, KV_W)],
        out_specs=[blk(t, D_MODEL), blk(wb, KV_W), blk(wb, KV_W)],
        out_shape=[jax.ShapeDtypeStruct((n, t, D_MODEL), F32),
                   jax.ShapeDtypeStruct((n, wb, KV_W), F32),
                   jax.ShapeDtypeStruct((n, wb, KV_W), F32)],
        compiler_params=_cparams("parallel"),
        name="attn_sample",
    )(sinks, q, kn, vn, ck, cv)


def _rglru_gates(xc, wa_ref, ba, wx_ref, bx, lam):
    xb = xc.astype(BF16)
    npair = D_MODEL // LANES
    ga = jnp.concatenate([jnp.dot(xb[:, j * LANES:(j + 1) * LANES], wa_ref[j], preferred_element_type=F32)
                          for j in range(npair)], axis=1)
    gx = jnp.concatenate([jnp.dot(xb[:, j * LANES:(j + 1) * LANES], wx_ref[j], preferred_element_type=F32)
                          for j in range(npair)], axis=1)
    r = _sigmoid(ga + ba)
    ig = _sigmoid(gx + bx)
    log_a = -RG_C * r * _softplus(-lam)
    a = jnp.exp(log_a)
    u = jnp.sqrt(-jnp.tanh(log_a) * (a * a + 1.0)) * (ig * xc)
    return a, u


def _rglru_prompt_kernel(x_ref, cw_ref, cb_ref, wa_ref, ba_ref, wx_ref, bx_ref, lam_ref, h_ref,
                         tail_ref, hc_ref):
    i = pl.program_id(1)
    tc = x_ref.shape[0]

    @pl.when(i == 0)
    def _():
        tail_ref[...] = jnp.zeros_like(tail_ref)
        hc_ref[...] = jnp.zeros_like(hc_ref)

    x = x_ref[...]
    xp = jnp.concatenate([tail_ref[...], x], axis=0)
    cw = cw_ref[...]
    xc = cb_ref[...] + x * cw[CONV_W - 1:CONV_W]
    for d in range(1, CONV_W):
        xc = xc + xp[8 - d:8 - d + tc] * cw[CONV_W - 1 - d:CONV_W - d]
    a, u = _rglru_gates(xc, wa_ref, ba_ref[...], wx_ref, bx_ref[...], lam_ref[...])
    ng = tc // 8
    acc_a = a.reshape(ng, 8, D_MODEL)
    acc_h = u.reshape(ng, 8, D_MODEL)
    sub = lax.broadcasted_iota(jnp.int32, (1, 8, D_MODEL), 1)
    for d in (1, 2, 4):
        keep = sub >= d
        sh_a = jnp.where(keep, pltpu.roll(acc_a, d, axis=1), 1.0)
        sh_h = jnp.where(keep, pltpu.roll(acc_h, d, axis=1), 0.0)
        acc_h = acc_h + acc_a * sh_h
        acc_a = acc_a * sh_a
    carry = hc_ref[...]
    for g in range(ng):
        hg = acc_h[g] + acc_a[g] * carry
        h_ref[g * 8:(g + 1) * 8, :] = hg
        carry = hg[7:8]
    hc_ref[...] = carry
    tail_ref[...] = x[tc - 8:tc]


def _rglru_prompt(x, nb_seq, seq, cw, cb, wa2, ba, wx2, bx, lam):
    tc = _tile(seq, 256)
    nt = seq // tc
    row = lambda b, i: (b * nt + i, 0)
    full2 = lambda b, i: (0, 0)
    full3 = lambda b, i: (0, 0, 0)
    return pl.pallas_call(
        _rglru_prompt_kernel,
        grid=(nb_seq, nt),
        in_specs=[pl.BlockSpec((tc, D_MODEL), row),
                  pl.BlockSpec((CONV_W, D_MODEL), full2), pl.BlockSpec((1, D_MODEL), full2),
                  pl.BlockSpec(wa2.shape, full3), pl.BlockSpec((1, D_MODEL), full2),
                  pl.BlockSpec(wx2.shape, full3), pl.BlockSpec((1, D_MODEL), full2),
                  pl.BlockSpec((1, D_MODEL), full2)],
        out_specs=pl.BlockSpec((tc, D_MODEL), row),
        out_shape=jax.ShapeDtypeStruct((nb_seq * seq, D_MODEL), F32),
        scratch_shapes=[pltpu.VMEM((8, D_MODEL), F32), pltpu.VMEM((1, D_MODEL), F32)],
        compiler_params=_cparams("parallel", "arbitrary"),
        name="rglru_prompt",
    )(x, cw, cb, wa2, ba, wx2, bx, lam)


def _rglru_sample_kernel(x_ref, cs_ref, h0_ref, cw_ref, cb_ref, wa_ref, ba_ref, wx_ref, bx_ref, lam_ref,
                         h_ref):
    t, sn, _ = x_ref.shape
    cw = cw_ref[...]
    nst = CONV_W - 1

    def slab(s):
        return x_ref[s] if s >= 0 else cs_ref[nst + s]

    xcs = []
    for s in range(t):
        xc = cb_ref[...] + slab(s) * cw[CONV_W - 1:CONV_W]
        for d in range(1, CONV_W):
            xc = xc + slab(s - d) * cw[CONV_W - 1 - d:CONV_W - d]
        xcs.append(xc)
    xc_all = jnp.concatenate(xcs, axis=0)
    a, u = _rglru_gates(xc_all, wa_ref, ba_ref[...], wx_ref, bx_ref[...], lam_ref[...])
    h = h0_ref[...]
    for s in range(t):
        h = a[s * sn:(s + 1) * sn] * h + u[s * sn:(s + 1) * sn]
        h_ref[s] = h


def _rglru_sample(x, cs, h0, cw, cb, wa2, ba, wx2, bx, lam):
    t, n, _ = x.shape
    sn = _tile(n, 32)
    full2 = lambda i: (0, 0)
    full3 = lambda i: (0, 0, 0)
    return pl.pallas_call(
        _rglru_sample_kernel,
        grid=(n // sn,),
        in_specs=[pl.BlockSpec((t, sn, D_MODEL), lambda i: (0, i, 0)),
                  pl.BlockSpec((CONV_W - 1, sn, D_MODEL), lambda i: (0, i, 0)),
                  pl.BlockSpec((sn, D_MODEL), lambda i: (i, 0)),
                  pl.BlockSpec((CONV_W, D_MODEL), full2), pl.BlockSpec((1, D_MODEL), full2),
                  pl.BlockSpec(wa2.shape, full3), pl.BlockSpec((1, D_MODEL), full2),
                  pl.BlockSpec(wx2.shape, full3), pl.BlockSpec((1, D_MODEL), full2),
                  pl.BlockSpec((1, D_MODEL), full2)],
        out_specs=pl.BlockSpec((t, sn, D_MODEL), lambda i: (0, i, 0)),
        out_shape=jax.ShapeDtypeStruct((t, n, D_MODEL), F32),
        compiler_params=_cparams("parallel"),
        name="rglru_sample",
    )(x, cs, h0, cw, cb, wa2, ba, wx2, bx, lam)


def _rwkv_prep_math(pc, prev, mu, w0, a0, kkp, ka, wwa, gup):
    d = D_MODEL
    ps = pc + mu * (prev - pc)
    r = ps[:, 0:d]
    k = ps[:, d:2 * d]
    v = ps[:, 2 * d:3 * d]
    l01 = ps[:, 3 * d:3 * d + LANES]
    lane = lax.broadcasted_iota(jnp.int32, (1, LANES), 1)
    t01 = jnp.where(lane < LORA_W, jnp.tanh(l01), l01)
    wa = _dot(t01, wwa)
    g = _dot(_sigmoid(ps[:, 3 * d + LANES:]), gup)
    wlog = -_softplus(-(w0 + wa[:, :d])) - 0.5
    logw = -jnp.exp(wlog)
    a = _sigmoid(a0 + wa[:, d:])
    kk = k * kkp
    e, et = _head_indicator()
    nrm = jnp.maximum(jnp.sqrt(_head_sum(kk * kk, e, et)), 1e-12)
    kkn = kk / nrm
    kmod = k * (1.0 + (a - 1.0) * ka)
    return r, kmod, v, logw, kkn, kkn * a, g


def _rwkv_prep_prompt_kernel(pc_ref, pv_ref, mu_ref, w0_ref, a0_ref, kkp_ref, ka_ref, wwa_ref, gup_ref,
                             *out_refs):
    i = pl.program_id(1)
    pc = pc_ref[...]
    first = jnp.where(i == 0, 0.0, 1.0) * pv_ref[7:8, :]
    prev = jnp.concatenate([first, pc[:-1]], axis=0)
    outs = _rwkv_prep_math(pc, prev, mu_ref[...], w0_ref[...], a0_ref[...], kkp_ref[...], ka_ref[...],
                           wwa_ref[...], gup_ref[...])
    for o_ref, o in zip(out_refs, outs):
        o_ref[...] = o


def _rwkv_prep_sample_kernel(pc_ref, pv_ref, sh_ref, mu_ref, w0_ref, a0_ref, kkp_ref, ka_ref, wwa_ref,
                             gup_ref, *out_refs):
    s = pl.program_id(0)
    prev = jnp.where(s == 0, sh_ref[...], pv_ref[...])
    outs = _rwkv_prep_math(pc_ref[...], prev, mu_ref[...], w0_ref[...], a0_ref[...], kkp_ref[...],
                           ka_ref[...], wwa_ref[...], gup_ref[...])
    for o_ref, o in zip(out_refs, outs):
        o_ref[...] = o


_N_PREP_OUT = 7


def _rwkv_prep_prompt(pc, nb_seq, seq, params):
    tt = _tile(seq, 256)
    nt = seq // tt
    row = lambda b, i: (b * nt + i, 0)
    prev8 = lambda b, i: (jnp.maximum((b * nt + i) * (tt // 8) - 1, 0), 0)
    full = lambda b, i: (0, 0)
    return pl.pallas_call(
        _rwkv_prep_prompt_kernel,
        grid=(nb_seq, nt),
        in_specs=[pl.BlockSpec((tt, SHIFT_PAD), row), pl.BlockSpec((8, SHIFT_PAD), prev8)]
                 + [pl.BlockSpec(p.shape, full) for p in params],
        out_specs=[pl.BlockSpec((tt, D_MODEL), row)] * _N_PREP_OUT,
        out_shape=[jax.ShapeDtypeStruct((nb_seq * seq, D_MODEL), F32)] * _N_PREP_OUT,
        compiler_params=_cparams("parallel", "parallel"),
        name="rwkv_prep_prompt",
    )(pc, pc, *params)


def _rwkv_prep_sample(pc, shift, t, n, params):
    full = lambda s: (0, 0)
    return pl.pallas_call(
        _rwkv_prep_sample_kernel,
        grid=(t,),
        in_specs=[pl.BlockSpec((n, SHIFT_PAD), lambda s: (s, 0)),
                  pl.BlockSpec((n, SHIFT_PAD), lambda s: (jnp.maximum(s - 1, 0), 0)),
                  pl.BlockSpec((n, SHIFT_PAD), full)]
                 + [pl.BlockSpec(p.shape, full) for p in params],
        out_specs=[pl.BlockSpec((n, D_MODEL), lambda s: (s, 0))] * _N_PREP_OUT,
        out_shape=[jax.ShapeDtypeStruct((t * n, D_MODEL), F32)] * _N_PREP_OUT,
        compiler_params=_cparams("parallel"),
        name="rwkv_prep_sample",
    )(pc, pc, shift, *params)


_NT = (((2,), (2,)), ((0,), (0,)))
_NN = (((2,), (1,)), ((0,), (0,)))
_TN = (((1,), (1,)), ((0,), (0,)))


def _bmm(a, b, dn, passes=1):
    if passes == 1:
        return lax.dot_general(a.astype(BF16), b.astype(BF16), dn, preferred_element_type=F32)
    ah, al = _split_terms(a, 2)
    bh, bl = _split_terms(b, 2)
    return (lax.dot_general(ah, bh, dn, preferred_element_type=F32)
            + lax.dot_general(ah, bl, dn, preferred_element_type=F32)
            + lax.dot_general(al, bh, dn, preferred_element_type=F32))


def _to_heads(x):
    return jnp.stack([x[:, h * HEAD_DIM:(h + 1) * HEAD_DIM] for h in range(N_HEADS)], axis=0)


def _wkv_prompt_kernel(r_ref, k_ref, v_ref, lw_ref, kk_ref, b_ref, y_ref, s_ref, h_ref):
    i = pl.program_id(1)
    c = r_ref.shape[0]
    hd = HEAD_DIM

    @pl.when(i == 0)
    def _():
        h_ref[...] = jnp.zeros_like(h_ref)

    row = lax.broadcasted_iota(jnp.int32, (c, c), 0)
    col = lax.broadcasted_iota(jnp.int32, (c, c), 1)
    incl = row >= col
    strict = row > col
    lw = lw_ref[...]
    cum = _dot_rhs_split(jnp.where(incl, 1.0, 0.0).astype(BF16), lw, 3)
    last = cum[c - 1:c]
    g_out = jnp.exp(cum)
    g_in = jnp.exp(cum - lw)
    g_inv = jnp.exp(-cum)
    g_end = jnp.exp(last - cum)
    kk = kk_ref[...]
    b = b_ref[...]
    k = k_ref[...]
    at = _to_heads(-(kk * g_in))
    rt = _to_heads(r_ref[...] * g_out)
    bt = _to_heads(b * g_inv)
    kt = _to_heads(k * g_inv)
    bh = _to_heads(b * g_end)
    kh = _to_heads(k * g_end)
    v = _to_heads(v_ref[...])
    gc = _to_heads(jnp.exp(last))

    amat = _bmm(jnp.concatenate([at, rt], axis=1), jnp.concatenate([bt, kt], axis=1), _NT)
    a_ab = jnp.where(strict[None], amat[:, :c, :c], 0.0)
    a_ak = jnp.where(strict[None], amat[:, :c, c:], 0.0)
    a_rb = jnp.where(incl[None], amat[:, c:, :c], 0.0)
    a_rk = jnp.where(incl[None], amat[:, c:, c:], 0.0)
    eye_c = jnp.where(row == col, 1.0, 0.0)
    tinv = eye_c[None] + a_ab
    npow = _bmm(a_ab, a_ab, _NN)
    span = 4
    while span < c:
        both = _bmm(jnp.concatenate([tinv, npow], axis=1), npow, _NN)
        tinv = tinv + both[:, :c]
        npow = both[:, c:]
        span *= 2
    tinv = tinv + _bmm(tinv, npow, _NN)
    w12 = _bmm(jnp.concatenate([a_ak, a_rk], axis=1), v, _NN)
    pq = _bmm(tinv, jnp.concatenate([at, w12[:, :c]], axis=2), _NN)
    y10 = jnp.concatenate([rt, w12[:, c:]], axis=2) + _bmm(a_rb, pq, _NN)
    mn = _bmm(bh, pq, _TN)
    kv = _bmm(kh, v, _TN)
    er = lax.broadcasted_iota(jnp.int32, (hd, hd), 0)
    ec = lax.broadcasted_iota(jnp.int32, (hd, hd), 1)
    m = mn[:, :, :hd] + jnp.where((er == ec)[None], gc, 0.0)
    n = mn[:, :, hd:] + kv
    h0 = h_ref[...]
    my = _bmm(jnp.concatenate([m, y10[:, :, :hd]], axis=1), h0, _NN, 3)
    h_new = my[:, :hd] + n
    h_ref[...] = h_new
    y = my[:, hd:] + y10[:, :, hd:]
    y_ref[...] = jnp.concatenate([y[h] for h in range(N_HEADS)], axis=1)

    @pl.when(i == pl.num_programs(1) - 1)
    def _():
        s_ref[0] = h_new


def _wkv_prompt(r, k, v, lw, kk, b, nb_seq, seq):
    c = _tile(seq, WKV_CHUNK)
    nc = seq // c
    row = lambda bb, i: (bb * nc + i, 0)
    return pl.pallas_call(
        _wkv_prompt_kernel,
        grid=(nb_seq, nc),
        in_specs=[pl.BlockSpec((c, D_MODEL), row)] * 6,
        out_specs=[pl.BlockSpec((c, D_MODEL), row),
                   pl.BlockSpec((1, N_HEADS, HEAD_DIM, HEAD_DIM), lambda bb, i: (bb, 0, 0, 0))],
        out_shape=[jax.ShapeDtypeStruct((nb_seq * seq, D_MODEL), F32),
                   jax.ShapeDtypeStruct((nb_seq, N_HEADS, HEAD_DIM, HEAD_DIM), F32)],
        scratch_shapes=[pltpu.VMEM((N_HEADS, HEAD_DIM, HEAD_DIM), F32)],
        compiler_params=_cparams("parallel", "arbitrary"),
        name="wkv_prompt",
    )(r, k, v, lw, kk, b)


_WKV_VROWS = 4


def _wkv_sample_kernel(r_ref, lw_ref, k_ref, v_ref, kk_ref, b_ref, s0_ref, y_ref, s_ref):
    t = r_ref.shape[1]
    vr = _WKV_VROWS

    def body(vi, carry):
        v0 = pl.multiple_of(vi * vr, vr)
        s = s0_ref[0, pl.ds(v0, vr)]
        for st in range(t):
            sa = -jnp.sum(s * kk_ref[0, st][None], axis=1, keepdims=True)
            vv = v_ref[0, st, pl.ds(v0, vr)]
            s = s * jnp.exp(lw_ref[0, st])[None] + sa * b_ref[0, st][None] + vv * k_ref[0, st][None]
            y_ref[0, st, pl.ds(v0, vr)] = jnp.sum(s * r_ref[0, st][None], axis=1, keepdims=True)
        s_ref[0, pl.ds(v0, vr)] = s
        return carry

    lax.fori_loop(0, HEAD_DIM // vr, body, 0)


def _wkv_sample(r, lw, k, v, kk, b, s0):
    nh, t, hd, n = r.shape
    vec = pl.BlockSpec((1, t, hd, n), lambda h: (h, 0, 0, 0))
    vec1 = pl.BlockSpec((1, t, hd, 1, n), lambda h: (h, 0, 0, 0, 0))
    st = pl.BlockSpec((1, hd, hd, n), lambda h: (h, 0, 0, 0))
    return pl.pallas_call(
        _wkv_sample_kernel,
        grid=(nh,),
        in_specs=[vec, vec, vec, vec1, vec, vec, st],
        out_specs=[vec1, st],
        out_shape=[jax.ShapeDtypeStruct((nh, t, hd, 1, n), F32),
                   jax.ShapeDtypeStruct((nh, hd, hd, n), F32)],
        compiler_params=_cparams("parallel"),
        name="wkv_sample",
    )(r, lw, k, v, kk, b, s0)


def _mix_out_kernel(x_ref, oa_ref, ob_ref, y_ref, r_ref, k_ref, v_ref, g_ref, gate_ref,
                    lnw_ref, lnb_ref, rk_ref, wout_ref, g1_ref, b1_ref, o_ref, *, alpha):
    d = D_MODEL
    e, et = _head_indicator()
    y = y_ref[...]
    mu = _head_sum(y, e, et) * (1.0 / HEAD_DIM)
    yc = y - mu
    var = _head_sum(yc * yc, e, et) * (1.0 / HEAD_DIM)
    yn = yc * lax.rsqrt(var + GN_EPS) * lnw_ref[...] + lnb_ref[...]
    v = v_ref[...]
    bonus = _head_sum(r_ref[...] * k_ref[...] * rk_ref[...], e, et) * v
    oc = (yn + bonus) * g_ref[...]
    gate = gate_ref[...]
    mixed = (_sigmoid(gate[:, 0:d]) * oa_ref[...] + _sigmoid(gate[:, d:2 * d]) * ob_ref[...]
             + _sigmoid(gate[:, 2 * d:3 * d]) * oc)
    z = alpha * x_ref[...] + jnp.dot(mixed.astype(BF16), wout_ref[...], preferred_element_type=F32)
    o_ref[...] = _layer_norm(z, g1_ref[...], b1_ref[...])


def _mix_out(x, oa, ob, y, r, k, v, g, gate, lnw, lnb, rk, wout, g1, b1, alpha):
    m = x.shape[0]
    tm = _tile(m, 256)
    row = lambda i: (i, 0)
    full = lambda i: (0, 0)
    rows = pl.BlockSpec((tm, D_MODEL), row)
    vec = pl.BlockSpec((1, D_MODEL), full)
    return pl.pallas_call(
        functools.partial(_mix_out_kernel, alpha=alpha),
        grid=(m // tm,),
        in_specs=[rows] * 8 + [pl.BlockSpec((tm, 3 * D_MODEL), row), vec, vec, vec,
                               pl.BlockSpec((D_MODEL, D_MODEL), full), vec, vec],
        out_specs=rows,
        out_shape=jax.ShapeDtypeStruct((m, D_MODEL), F32),
        compiler_params=_cparams("parallel"),
        name="mix_out",
    )(x, oa, ob, y, r, k, v, g, gate, lnw, lnb, rk, wout, g1, b1)


def _ffn_kernel(x_ref, wg_ref, wu_ref, wd_ref, g2_ref, b2_ref, o_ref, acc_ref, *, alpha):
    f = pl.program_id(1)
    x = x_ref[...]
    xb = x.astype(BF16)
    gt = jnp.dot(xb, wg_ref[...], preferred_element_type=F32)
    up = jnp.dot(xb, wu_ref[...], preferred_element_type=F32)
    part = jnp.dot((gt * _sigmoid(gt) * up).astype(BF16), wd_ref[...], preferred_element_type=F32)

    @pl.when(f == 0)
    def _():
        acc_ref[...] = part

    @pl.when(f > 0)
    def _():
        acc_ref[...] += part

    @pl.when(f == pl.num_programs(1) - 1)
    def _():
        o_ref[...] = _layer_norm(alpha * x + acc_ref[...], g2_ref[...], b2_ref[...])


def _ffn(x, wg, wu, wd, g2, b2, alpha):
    m = x.shape[0]
    dff = wg.shape[1]
    tm = _tile(m, 512)
    tf = dff // 2 if (dff // 2) % LANES == 0 else dff
    row = lambda i, f: (i, 0)
    full = lambda i, f: (0, 0)
    return pl.pallas_call(
        functools.partial(_ffn_kernel, alpha=alpha),
        grid=(m // tm, dff // tf),
        in_specs=[pl.BlockSpec((tm, D_MODEL), row),
                  pl.BlockSpec((D_MODEL, tf), lambda i, f: (0, f)),
                  pl.BlockSpec((D_MODEL, tf), lambda i, f: (0, f)),
                  pl.BlockSpec((tf, D_MODEL), lambda i, f: (f, 0)),
                  pl.BlockSpec((1, D_MODEL), full), pl.BlockSpec((1, D_MODEL), full)],
        out_specs=pl.BlockSpec((tm, D_MODEL), row),
        out_shape=jax.ShapeDtypeStruct((m, D_MODEL), F32),
        scratch_shapes=[pltpu.VMEM((tm, D_MODEL), F32)],
        compiler_params=_cparams("parallel", "arbitrary"),
        name="ffn",
    )(x, wg, wu, wd, g2, b2)


def _blockdiag_pairs(w):
    nb, bs, _ = w.shape
    w = w.reshape(nb // 2, 2, bs, bs)
    z = jnp.zeros((nb // 2, bs, bs), w.dtype)
    top = jnp.concatenate([w[:, 0], z], axis=2)
    bot = jnp.concatenate([z, w[:, 1]], axis=2)
    return jnp.concatenate([top, bot], axis=1)


def _row(p):
    return p.reshape(1, -1)


def kernel(x_prompt, x_sample, cache_k, cache_v, state_conv, state_rglru, state_shift, state_wkv, w_in, attn_sinks, conv_w, conv_b, rg_wa, rg_ba, rg_wx, rg_bx, rg_lambda, rw_mu, rw_w0, rw_wup, rw_a0, rw_aup, rw_gup, rw_kk, rw_ka, rw_rk, rw_lnw, rw_lnb, w_out, ln1_g, ln1_b, w_gu, w_down, ln2_g, ln2_b):
    depth = w_in.shape[0]
    bp, seq, d = x_prompt.shape
    ns, ts, _ = x_sample.shape
    wb = cache_k.shape[2]
    alpha = (2 * depth) ** 0.25
    dff = w_down.shape[1]
    pad = SHIFT_PAD - SHIFT_W

    xp = x_prompt.reshape(bp * seq, d)
    xs = jnp.swapaxes(x_sample, 0, 1).reshape(ts * ns, d)
    st_p = [[] for _ in range(6)]
    st_s = [[] for _ in range(6)]

    for l in range(depth):
        wl = w_in[l]
        w_all = jnp.concatenate([wl[:, :OFF_GATE], jnp.zeros((d, pad), F32), wl[:, OFF_GATE:]],
                                axis=1).astype(BF16)
        wa2 = _blockdiag_pairs(rg_wa[l]).astype(BF16)
        wx2 = _blockdiag_pairs(rg_wx[l]).astype(BF16)
        rg_args = (conv_w[l], _row(conv_b[l]), wa2, _row(rg_ba[l]), wx2, _row(rg_bx[l]), _row(rg_lambda[l]))
        zw = jnp.zeros((LORA_W, d), F32)
        wwa = jnp.concatenate([jnp.concatenate([rw_wup[l], zw], axis=1),
                               jnp.concatenate([zw, rw_aup[l]], axis=1)], axis=0).astype(BF16)
        gup = jnp.pad(rw_gup[l], ((0, 2 * LANES - LORA_G), (0, 0))).astype(BF16)
        prep_params = (_row(jnp.pad(rw_mu[l], (0, pad))), _row(rw_w0[l]), _row(rw_a0[l]), _row(rw_kk[l]),
                       _row(rw_ka[l]), wwa, gup)
        mix_params = (_row(rw_lnw[l]), _row(rw_lnb[l]), _row(rw_rk[l]), w_out[l].astype(BF16),
                      _row(ln1_g[l]), _row(ln1_b[l]))
        wg = w_gu[l][:, :dff].astype(BF16)
        wu = w_gu[l][:, dff:].astype(BF16)
        wd = w_down[l].astype(BF16)
        ffn_params = (wg, wu, wd, _row(ln2_g[l]), _row(ln2_b[l]))
        sinks = attn_sinks[l]

        qkv, rnn, rw, gate = _in_proj(xp, w_all)
        oa = _attn_prompt(qkv, sinks, bp, seq)
        ob = _rglru_prompt(rnn, bp, seq, *rg_args)
        r, k, v, lw, kk, b, g = _rwkv_prep_prompt(rw, bp, seq, prep_params)
        y, hfin = _wkv_prompt(r, k, v, lw, kk, b, bp, seq)
        x1 = _mix_out(xp, oa, ob, y, r, k, v, g, gate, *mix_params, alpha)
        xp = _ffn(x1, *ffn_params, alpha)
        qkv3 = qkv.reshape(bp, seq, OFF_RNN)
        st_p[0].append(qkv3[:, seq - wb:, OFF_K:OFF_V].reshape(bp, wb, N_KV_HEADS, HEAD_DIM))
        st_p[1].append(qkv3[:, seq - wb:, OFF_V:OFF_RNN].reshape(bp, wb, N_KV_HEADS, HEAD_DIM))
        st_p[2].append(rnn.reshape(bp, seq, d)[:, seq - (CONV_W - 1):])
        st_p[3].append(ob.reshape(bp, seq, d)[:, seq - 1])
        st_p[4].append(rw.reshape(bp, seq, SHIFT_PAD)[:, seq - 1, :SHIFT_W])
        st_p[5].append(jnp.swapaxes(hfin, 2, 3))

        qkv, rnn, rw, gate = _in_proj(xs, w_all)
        qkv_n = jnp.swapaxes(qkv.reshape(ts, ns, OFF_RNN), 0, 1)
        oa_n, new_k, new_v = _attn_sample(qkv_n[:, :, :OFF_K], qkv_n[:, :, OFF_K:OFF_V], qkv_n[:, :, OFF_V:],
                                          cache_k[l].reshape(ns, wb, KV_W), cache_v[l].reshape(ns, wb, KV_W),
                                          sinks)
        oa = jnp.swapaxes(oa_n, 0, 1).reshape(ts * ns, d)
        rnn3 = rnn.reshape(ts, ns, d)
        ob3 = _rglru_sample(rnn3, jnp.swapaxes(state_conv[l], 0, 1), state_rglru[l], *rg_args)
        ob = ob3.reshape(ts * ns, d)
        shift = jnp.pad(state_shift[l], ((0, 0), (0, pad)))
        r, k, v, lw, kk, b, g = _rwkv_prep_sample(rw, shift, ts, ns, prep_params)

        def lanes(z):
            return jnp.transpose(z.reshape(ts, ns, N_HEADS, HEAD_DIM), (2, 0, 3, 1))

        s0 = jnp.transpose(state_wkv[l], (1, 2, 3, 0))
        y5, s_new = _wkv_sample(lanes(r), lanes(lw), lanes(k),
                                lanes(v).reshape(N_HEADS, ts, HEAD_DIM, 1, ns), lanes(kk), lanes(b), s0)
        y = jnp.transpose(y5.reshape(N_HEADS, ts, HEAD_DIM, ns), (1, 3, 0, 2)).reshape(ts * ns, d)
        x1 = _mix_out(xs, oa, ob, y, r, k, v, g, gate, *mix_params, alpha)
        xs = _ffn(x1, *ffn_params, alpha)
        st_s[0].append(new_k.reshape(ns, wb, N_KV_HEADS, HEAD_DIM))
        st_s[1].append(new_v.reshape(ns, wb, N_KV_HEADS, HEAD_DIM))
        st_s[2].append(jnp.swapaxes(rnn3[ts - (CONV_W - 1):], 0, 1))
        st_s[3].append(ob3[ts - 1])
        st_s[4].append(rw.reshape(ts, ns, SHIFT_PAD)[ts - 1, :, :SHIFT_W])
        st_s[5].append(jnp.transpose(s_new, (3, 0, 1, 2)))

    yp = xp.reshape(bp, seq, d)
    ys = jnp.swapaxes(xs.reshape(ts, ns, d), 0, 1)
    return (yp, ys, *(jnp.stack(s) for s in st_p), *(jnp.stack(s) for s in st_s))
```

```python
import functools
import math

import jax
import jax.numpy as jnp
from jax import lax
from jax.experimental import pallas as pl
from jax.experimental.pallas import tpu as pltpu

F32 = jnp.float32
BF16 = jnp.bfloat16

D_MODEL = 1024
HEAD_DIM = 64
N_HEADS = 16
N_KV_HEADS = 4
KV_W = N_KV_HEADS * HEAD_DIM
WINDOW = 128
CONV_W = 4
RG_C = 8.0
LORA_W = 64
LORA_A = 64
LORA_G = 160
SHIFT_W = 3 * D_MODEL + LORA_W + LORA_A + LORA_G
LANES = 128
SHIFT_PAD = -(-SHIFT_W // LANES) * LANES
GN_EPS = 64e-5
LN_EPS = 1e-5
OFF_K = D_MODEL
OFF_V = OFF_K + KV_W
OFF_RNN = OFF_V + KV_W
OFF_RW = OFF_RNN + D_MODEL
OFF_GATE = OFF_RW + SHIFT_W
WKV_CHUNK = 64
VMEM_LIMIT = 48 * 1024 * 1024


def _cparams(*sem):
    return pltpu.CompilerParams(dimension_semantics=sem, vmem_limit_bytes=VMEM_LIMIT)


def _tile(n, pref):
    if n <= pref:
        return n
    for t in range(pref, 7, -1):
        if n % t == 0 and t % 8 == 0:
            return t
    return n


def _dot(a, b):
    return jnp.dot(a.astype(BF16), b.astype(BF16), preferred_element_type=F32)


def _split_terms(x, n):
    terms = []
    rem = x
    for _ in range(n):
        hi = rem.astype(BF16)
        terms.append(hi)
        rem = rem - hi.astype(F32)
    return terms


def _dot_lhs_split(x, w_bf16, n):
    acc = None
    for t in _split_terms(x, n):
        p = jnp.dot(t, w_bf16, preferred_element_type=F32)
        acc = p if acc is None else acc + p
    return acc


def _dot_rhs_split(w_bf16, x, n):
    acc = None
    for t in _split_terms(x, n):
        p = jnp.dot(w_bf16, t, preferred_element_type=F32)
        acc = p if acc is None else acc + p
    return acc


def _sigmoid(x):
    return 0.5 * jnp.tanh(0.5 * x) + 0.5


def _softplus(x):
    return jnp.maximum(x, 0.0) + jnp.log(1.0 + jnp.exp(-jnp.abs(x)))


def _layer_norm(z, g, b):
    mu = jnp.mean(z, axis=-1, keepdims=True)
    zc = z - mu
    var = jnp.mean(zc * zc, axis=-1, keepdims=True)
    return zc * lax.rsqrt(var + LN_EPS) * g + b


def _head_indicator():
    c = lax.broadcasted_iota(jnp.int32, (D_MODEL, LANES), 0) // HEAD_DIM
    h = lax.broadcasted_iota(jnp.int32, (D_MODEL, LANES), 1)
    e = jnp.where(c == h, 1.0, 0.0).astype(BF16)
    ct = lax.broadcasted_iota(jnp.int32, (LANES, D_MODEL), 1) // HEAD_DIM
    ht = lax.broadcasted_iota(jnp.int32, (LANES, D_MODEL), 0)
    et = jnp.where(ct == ht, 1.0, 0.0).astype(BF16)
    return e, et


def _head_sum(x, e, et):
    s = _dot_lhs_split(x, e, 2)
    return _dot_lhs_split(s, et, 2)


_PROJ_WIDTHS = (OFF_RNN, D_MODEL, SHIFT_PAD)


def _in_proj_kernel(x_ref, w_ref, *o_refs):
    xb = x_ref[...].astype(BF16)
    off = 0
    for o_ref in o_refs:
        n = o_ref.shape[1]
        o_ref[...] = jnp.dot(xb, w_ref[:, off:off + n], preferred_element_type=F32)
        off += n


def _in_proj(x, w):
    m, k = x.shape
    tm = _tile(m, 256)
    return pl.pallas_call(
        _in_proj_kernel,
        grid=(m // tm,),
        in_specs=[pl.BlockSpec((tm, k), lambda i: (i, 0)),
                  pl.BlockSpec(w.shape, lambda i: (0, 0), pipeline_mode=pl.Buffered(1))],
        out_specs=[pl.BlockSpec((tm, n), lambda i: (i, 0)) for n in _PROJ_WIDTHS],
        out_shape=[jax.ShapeDtypeStruct((m, n), F32) for n in _PROJ_WIDTHS],
        compiler_params=_cparams("parallel"),
        name="in_proj",
    )(x, w)


def _alibi_slope(h):
    return 2.0 ** (-8.0 * (h + 1) / N_HEADS)


def _attn_prompt_bias():
    w = WINDOW
    rel = w + jnp.arange(w)[:, None] - jnp.arange(2 * w)[None, :]
    band = (rel >= 0) & (rel < WINDOW)
    has_prev = jnp.arange(2)[:, None, None] > 0
    valid = band[None] & (has_prev | (jnp.arange(2 * w) >= w)[None, None, :])
    slopes = jnp.array([_alibi_slope(h) for h in range(N_HEADS)], F32)
    bias = -slopes[None, :, None, None] * rel.astype(F32)[None, None]
    return jnp.where(valid[:, None], bias, -jnp.inf)


def _attn_prompt_kernel(sink_ref, bias_ref, q_ref, kc_ref, vc_ref, kp_ref, vp_ref, o_ref):
    q = q_ref[...] * (HEAD_DIM ** -0.5)
    kcat = jnp.concatenate([kp_ref[...], kc_ref[...]], axis=0).astype(BF16)
    vcat = jnp.concatenate([vp_ref[...], vc_ref[...]], axis=0).astype(BF16)
    outs = []
    for h in range(N_HEADS):
        g = h // (N_HEADS // N_KV_HEADS)
        qh = q[:, h * HEAD_DIM:(h + 1) * HEAD_DIM].astype(BF16)
        kg = kcat[:, g * HEAD_DIM:(g + 1) * HEAD_DIM]
        vg = vcat[:, g * HEAD_DIM:(g + 1) * HEAD_DIM]
        s = lax.dot_general(qh, kg, (((1,), (1,)), ((), ())), preferred_element_type=F32) + bias_ref[0, h]
        sink = sink_ref[h]
        m = jnp.maximum(jnp.max(s, axis=-1, keepdims=True), sink)
        e = jnp.exp(s - m)
        den = jnp.sum(e, axis=-1, keepdims=True) + jnp.exp(sink - m)
        p = e / den
        outs.append(jnp.dot(p.astype(BF16), vg, preferred_element_type=F32))
    o_ref[...] = jnp.concatenate(outs, axis=1)


def _attn_prompt(qkv, sinks, nb_seq, seq):
    w = WINDOW
    nb = seq // w
    kcol = OFF_K // KV_W
    vcol = OFF_V // KV_W

    def cur(c):
        return lambda b, j: (b * nb + j, c)

    def prev(c):
        return lambda b, j: (jnp.maximum(b * nb + j - 1, 0), c)

    return pl.pallas_call(
        _attn_prompt_kernel,
        grid=(nb_seq, nb),
        in_specs=[pl.BlockSpec(memory_space=pltpu.SMEM),
                  pl.BlockSpec((1, N_HEADS, w, 2 * w), lambda b, j: (jnp.minimum(j, 1), 0, 0, 0)),
                  pl.BlockSpec((w, D_MODEL), cur(0)),
                  pl.BlockSpec((w, KV_W), cur(kcol)),
                  pl.BlockSpec((w, KV_W), cur(vcol)),
                  pl.BlockSpec((w, KV_W), prev(kcol)),
                  pl.BlockSpec((w, KV_W), prev(vcol))],
        out_specs=pl.BlockSpec((w, D_MODEL), cur(0)),
        out_shape=jax.ShapeDtypeStruct((nb_seq * seq, D_MODEL), F32),
        compiler_params=_cparams("parallel", "parallel"),
        name="attn_prompt",
    )(sinks, _attn_prompt_bias(), qkv, qkv, qkv, qkv, qkv)


def _attn_sample_kernel(sink_ref, q_ref, kn_ref, vn_ref, ck_ref, cv_ref, o_ref, nk_ref, nv_ref, *, t):
    sn = q_ref.shape[0]
    wb = ck_ref.shape[1]
    gsz = N_HEADS // N_KV_HEADS
    rows = gsz * t
    q = q_ref[...]
    kn = kn_ref[...]
    vn = vn_ref[...]
    ck = ck_ref[...]
    cv = cv_ref[...]
    tq1 = lax.broadcasted_iota(jnp.int32, (rows, wb), 0) % t
    j1 = lax.broadcasted_iota(jnp.int32, (rows, wb), 1)
    rel1 = wb + tq1 - j1
    valid1 = (rel1 >= 0) & (rel1 < WINDOW)
    tq2 = lax.broadcasted_iota(jnp.int32, (rows, t), 0) % t
    s2i = lax.broadcasted_iota(jnp.int32, (rows, t), 1)
    rel2 = tq2 - s2i
    valid2 = (rel2 >= 0) & (rel2 < WINDOW)
    hrow = lax.broadcasted_iota(jnp.int32, (rows, 1), 0) // t
    outs = []
    for g in range(N_KV_HEADS):
        slope = jnp.zeros((rows, 1), F32)
        sink = jnp.zeros((rows, 1), F32)
        for hh in range(gsz):
            h = g * gsz + hh
            slope = jnp.where(hrow == hh, _alibi_slope(h), slope)
            sink = jnp.where(hrow == hh, sink_ref[h], sink)
        qg = jnp.concatenate([q[:, :, (g * gsz + hh) * HEAD_DIM:(g * gsz + hh + 1) * HEAD_DIM]
                              for hh in range(gsz)], axis=1).astype(BF16)
        lo, hi = g * HEAD_DIM, (g + 1) * HEAD_DIM
        ckg = ck[:, :, lo:hi].astype(BF16)
        cvg = cv[:, :, lo:hi].astype(BF16)
        kng = kn[:, :, lo:hi].astype(BF16)
        vng = vn[:, :, lo:hi].astype(BF16)
        nt = (((2,), (2,)), ((0,), (0,)))
        nn = (((2,), (1,)), ((0,), (0,)))
        s1 = lax.dot_general(qg, ckg, nt, preferred_element_type=F32) * (HEAD_DIM ** -0.5)
        s2 = lax.dot_general(qg, kng, nt, preferred_element_type=F32) * (HEAD_DIM ** -0.5)
        s1 = jnp.where(valid1[None], s1 - (slope * rel1.astype(F32))[None], -jnp.inf)
        s2 = jnp.where(valid2[None], s2 - (slope * rel2.astype(F32))[None], -jnp.inf)
        m = jnp.maximum(jnp.maximum(jnp.max(s1, axis=-1, keepdims=True),
                                    jnp.max(s2, axis=-1, keepdims=True)), sink[None])
        e1 = jnp.exp(s1 - m)
        e2 = jnp.exp(s2 - m)
        den = (jnp.sum(e1, axis=-1, keepdims=True) + jnp.sum(e2, axis=-1, keepdims=True)
               + jnp.exp(sink[None] - m))
        p1 = (e1 / den).astype(BF16)
        p2 = (e2 / den).astype(BF16)
        og = (lax.dot_general(p1, cvg, nn, preferred_element_type=F32)
              + lax.dot_general(p2, vng, nn, preferred_element_type=F32))
        outs.extend(og[:, hh * t:(hh + 1) * t, :] for hh in range(gsz))
    o_ref[...] = jnp.concatenate(outs, axis=2)
    nk_ref[...] = jnp.concatenate([ck[:, t:, :], kn], axis=1)
    nv_ref[...] = jnp.concatenate([cv[:, t:, :], vn], axis=1)


def _attn_sample(q, kn, vn, ck, cv, sinks):
    n, t, _ = q.shape
    wb = ck.shape[1]
    sn = _tile(n, 16)
    blk = lambda d1, d2: pl.BlockSpec((sn, d1, d2), lambda i: (i, 0, 0))
    return pl.pallas_call(
        functools.partial(_attn_sample_kernel, t=t),
        grid=(n // sn,),
        in_specs=[pl.BlockSpec(memory_space=pltpu.SMEM),
                  blk(t, D_MODEL), blk(t, KV_W), blk(t, KV_W), blk(wb, KV_W), blk(wb, KV_W)],
        out_specs=[blk(t, D_MODEL), blk(wb, KV_W), blk(wb, KV_W)],
        out_shape=[jax.ShapeDtypeStruct((n, t, D_MODEL), F32),
                   jax.ShapeDtypeStruct((n, wb, KV_W), F32),
                   jax.ShapeDtypeStruct((n, wb, KV_W), F32)],
        compiler_params=_cparams("parallel"),
        name="attn_sample",
    )(sinks, q, kn, vn, ck, cv)


def _rglru_gates(xc, wa_ref, ba, wx_ref, bx, lam):
    xb = xc.astype(BF16)
    npair = D_MODEL // LANES
    ga = jnp.concatenate([jnp.dot(xb[:, j * LANES:(j + 1) * LANES], wa_ref[j], preferred_element_type=F32)
                          for j in range(npair)], axis=1)
    gx = jnp.concatenate([jnp.dot(xb[:, j * LANES:(j + 1) * LANES], wx_ref[j], preferred_element_type=F32)
                          for j in range(npair)], axis=1)
    r = _sigmoid(ga + ba)
    ig = _sigmoid(gx + bx)
    log_a = -RG_C * r * _softplus(-lam)
    a = jnp.exp(log_a)
    u = jnp.sqrt(-jnp.tanh(log_a) * (a * a + 1.0)) * (ig * xc)
    return a, u


def _rglru_prompt_kernel(x_ref, cw_ref, cb_ref, wa_ref, ba_ref, wx_ref, bx_ref, lam_ref, h_ref,
                         tail_ref, hc_ref):
    i = pl.program_id(1)
    tc = x_ref.shape[0]

    @pl.when(i == 0)
    def _():
        tail_ref[...] = jnp.zeros_like(tail_ref)
        hc_ref[...] = jnp.zeros_like(hc_ref)

    x = x_ref[...]
    xp = jnp.concatenate([tail_ref[...], x], axis=0)
    cw = cw_ref[...]
    xc = cb_ref[...] + x * cw[CONV_W - 1:CONV_W]
    for d in range(1, CONV_W):
        xc = xc + xp[8 - d:8 - d + tc] * cw[CONV_W - 1 - d:CONV_W - d]
    a, u = _rglru_gates(xc, wa_ref, ba_ref[...], wx_ref, bx_ref[...], lam_ref[...])
    ng = tc // 8
    acc_a = a.reshape(ng, 8, D_MODEL)
    acc_h = u.reshape(ng, 8, D_MODEL)
    sub = lax.broadcasted_iota(jnp.int32, (1, 8, D_MODEL), 1)
    for d in (1, 2, 4):
        keep = sub >= d
        sh_a = jnp.where(keep, pltpu.roll(acc_a, d, axis=1), 1.0)
        sh_h = jnp.where(keep, pltpu.roll(acc_h, d, axis=1), 0.0)
        acc_h = acc_h + acc_a * sh_h
        acc_a = acc_a * sh_a
    carry = hc_ref[...]
    for g in range(ng):
        hg = acc_h[g] + acc_a[g] * carry
        h_ref[g * 8:(g + 1) * 8, :] = hg
        carry = hg[7:8]
    hc_ref[...] = carry
    tail_ref[...] = x[tc - 8:tc]


def _rglru_prompt(x, nb_seq, seq, cw, cb, wa2, ba, wx2, bx, lam):
    tc = _tile(seq, 256)
    nt = seq // tc
    row = lambda b, i: (b * nt + i, 0)
    full2 = lambda b, i: (0, 0)
    full3 = lambda b, i: (0, 0, 0)
    return pl.pallas_call(
        _rglru_prompt_kernel,
        grid=(nb_seq, nt),
        in_specs=[pl.BlockSpec((tc, D_MODEL), row),
                  pl.BlockSpec((CONV_W, D_MODEL), full2), pl.BlockSpec((1, D_MODEL), full2),
                  pl.BlockSpec(wa2.shape, full3), pl.BlockSpec((1, D_MODEL), full2),
                  pl.BlockSpec(wx2.shape, full3), pl.BlockSpec((1, D_MODEL), full2),
                  pl.BlockSpec((1, D_MODEL), full2)],
        out_specs=pl.BlockSpec((tc, D_MODEL), row),
        out_shape=jax.ShapeDtypeStruct((nb_seq * seq, D_MODEL), F32),
        scratch_shapes=[pltpu.VMEM((8, D_MODEL), F32), pltpu.VMEM((1, D_MODEL), F32)],
        compiler_params=_cparams("parallel", "arbitrary"),
        name="rglru_prompt",
    )(x, cw, cb, wa2, ba, wx2, bx, lam)


def _rglru_sample_kernel(x_ref, cs_ref, h0_ref, cw_ref, cb_ref, wa_ref, ba_ref, wx_ref, bx_ref, lam_ref,
                         h_ref):
    t, sn, _ = x_ref.shape
    cw = cw_ref[...]
    nst = CONV_W - 1

    def slab(s):
        return x_ref[s] if s >= 0 else cs_ref[nst + s]

    xcs = []
    for s in range(t):
        xc = cb_ref[...] + slab(s) * cw[CONV_W - 1:CONV_W]
        for d in range(1, CONV_W):
            xc = xc + slab(s - d) * cw[CONV_W - 1 - d:CONV_W - d]
        xcs.append(xc)
    xc_all = jnp.concatenate(xcs, axis=0)
    a, u = _rglru_gates(xc_all, wa_ref, ba_ref[...], wx_ref, bx_ref[...], lam_ref[...])
    h = h0_ref[...]
    for s in range(t):
        h = a[s * sn:(s + 1) * sn] * h + u[s * sn:(s + 1) * sn]
        h_ref[s] = h


def _rglru_sample(x, cs, h0, cw, cb, wa2, ba, wx2, bx, lam):
    t, n, _ = x.shape
    sn = _tile(n, 32)
    full2 = lambda i: (0, 0)
    full3 = lambda i: (0, 0, 0)
    return pl.pallas_call(
        _rglru_sample_kernel,
        grid=(n // sn,),
        in_specs=[pl.BlockSpec((t, sn, D_MODEL), lambda i: (0, i, 0)),
                  pl.BlockSpec((CONV_W - 1, sn, D_MODEL), lambda i: (0, i, 0)),
                  pl.BlockSpec((sn, D_MODEL), lambda i: (i, 0)),
                  pl.BlockSpec((CONV_W, D_MODEL), full2), pl.BlockSpec((1, D_MODEL), full2),
                  pl.BlockSpec(wa2.shape, full3), pl.BlockSpec((1, D_MODEL), full2),
                  pl.BlockSpec(wx2.shape, full3), pl.BlockSpec((1, D_MODEL), full2),
                  pl.BlockSpec((1, D_MODEL), full2)],
        out_specs=pl.BlockSpec((t, sn, D_MODEL), lambda i: (0, i, 0)),
        out_shape=jax.ShapeDtypeStruct((t, n, D_MODEL), F32),
        compiler_params=_cparams("parallel"),
        name="rglru_sample",
    )(x, cs, h0, cw, cb, wa2, ba, wx2, bx, lam)


def _rwkv_prep_math(pc, prev, mu, w0, a0, kkp, ka, wwa, gup):
    d = D_MODEL
    ps = pc + mu * (prev - pc)
    r = ps[:, 0:d]
    k = ps[:, d:2 * d]
    v = ps[:, 2 * d:3 * d]
    l01 = ps[:, 3 * d:3 * d + LANES]
    lane = lax.broadcasted_iota(jnp.int32, (1, LANES), 1)
    t01 = jnp.where(lane < LORA_W, jnp.tanh(l01), l01)
    wa = _dot(t01, wwa)
    g = _dot(_sigmoid(ps[:, 3 * d + LANES:]), gup)
    wlog = -_softplus(-(w0 + wa[:, :d])) - 0.5
    logw = -jnp.exp(wlog)
    a = _sigmoid(a0 + wa[:, d:])
    kk = k * kkp
    e, et = _head_indicator()
    nrm = jnp.maximum(jnp.sqrt(_head_sum(kk * kk, e, et)), 1e-12)
    kkn = kk / nrm
    kmod = k * (1.0 + (a - 1.0) * ka)
    return r, kmod, v, logw, kkn, kkn * a, g


def _rwkv_prep_prompt_kernel(pc_ref, pv_ref, mu_ref, w0_ref, a0_ref, kkp_ref, ka_ref, wwa_ref, gup_ref,
                             *out_refs):
    i = pl.program_id(1)
    pc = pc_ref[...]
    first = jnp.where(i == 0, 0.0, 1.0) * pv_ref[7:8, :]
    prev = jnp.concatenate([first, pc[:-1]], axis=0)
    outs = _rwkv_prep_math(pc, prev, mu_ref[...], w0_ref[...], a0_ref[...], kkp_ref[...], ka_ref[...],
                           wwa_ref[...], gup_ref[...])
    for o_ref, o in zip(out_refs, outs):
        o_ref[...] = o


def _rwkv_prep_sample_kernel(pc_ref, pv_ref, sh_ref, mu_ref, w0_ref, a0_ref, kkp_ref, ka_ref, wwa_ref,
                             gup_ref, *out_refs):
    s = pl.program_id(0)
    prev = jnp.where(s == 0, sh_ref[...], pv_ref[...])
    outs = _rwkv_prep_math(pc_ref[...], prev, mu_ref[...], w0_ref[...], a0_ref[...], kkp_ref[...],
                           ka_ref[...], wwa_ref[...], gup_ref[...])
    for o_ref, o in zip(out_refs, outs):
        o_ref[...] = o


_N_PREP_OUT = 7


def _rwkv_prep_prompt(pc, nb_seq, seq, params):
    tt = _tile(seq, 256)
    nt = seq // tt
    row = lambda b, i: (b * nt + i, 0)
    prev8 = lambda b, i: (jnp.maximum((b * nt + i) * (tt // 8) - 1, 0), 0)
    full = lambda b, i: (0, 0)
    return pl.pallas_call(
        _rwkv_prep_prompt_kernel,
        grid=(nb_seq, nt),
        in_specs=[pl.BlockSpec((tt, SHIFT_PAD), row), pl.BlockSpec((8, SHIFT_PAD), prev8)]
                 + [pl.BlockSpec(p.shape, full) for p in params],
        out_specs=[pl.BlockSpec((tt, D_MODEL), row)] * _N_PREP_OUT,
        out_shape=[jax.ShapeDtypeStruct((nb_seq * seq, D_MODEL), F32)] * _N_PREP_OUT,
        compiler_params=_cparams("parallel", "parallel"),
        name="rwkv_prep_prompt",
    )(pc, pc, *params)


def _rwkv_prep_sample(pc, shift, t, n, params):
    full = lambda s: (0, 0)
    return pl.pallas_call(
        _rwkv_prep_sample_kernel,
        grid=(t,),
        in_specs=[pl.BlockSpec((n, SHIFT_PAD), lambda s: (s, 0)),
                  pl.BlockSpec((n, SHIFT_PAD), lambda s: (jnp.maximum(s - 1, 0), 0)),
                  pl.BlockSpec((n, SHIFT_PAD), full)]
                 + [pl.BlockSpec(p.shape, full) for p in params],
        out_specs=[pl.BlockSpec((n, D_MODEL), lambda s: (s, 0))] * _N_PREP_OUT,
        out_shape=[jax.ShapeDtypeStruct((t * n, D_MODEL), F32)] * _N_PREP_OUT,
        compiler_params=_cparams("parallel"),
        name="rwkv_prep_sample",
    )(pc, pc, shift, *params)


_PAIR_W = 2 * HEAD_DIM
_N_PAIRS = D_MODEL // _PAIR_W
_WKV_STATE_PASSES = 1


def _dot_tn(a, b):
    return lax.dot_general(a.astype(BF16), b.astype(BF16), (((0,), (0,)), ((), ())),
                           preferred_element_type=F32)


def _dot_nt(a, b):
    return lax.dot_general(a.astype(BF16), b.astype(BF16), (((1,), (1,)), ((), ())),
                           preferred_element_type=F32)


def _dot_passes(a, b, passes):
    if passes == 1:
        return _dot(a, b)
    ah, al = _split_terms(a, 2)
    bh, bl = _split_terms(b, 2)
    return (jnp.dot(ah, bh, preferred_element_type=F32) + jnp.dot(ah, bl, preferred_element_type=F32)
            + jnp.dot(al, bh, preferred_element_type=F32))


def _wkv_prompt_kernel(r_ref, k_ref, v_ref, lw_ref, kk_ref, b_ref, y_ref, s_ref, h_ref):
    i = pl.program_id(1)
    c = r_ref.shape[0]
    hd = HEAD_DIM
    pw = _PAIR_W

    @pl.when(i == 0)
    def _():
        h_ref[...] = jnp.zeros_like(h_ref)

    row_c = lax.broadcasted_iota(jnp.int32, (c, c), 0)
    col_c = lax.broadcasted_iota(jnp.int32, (c, c), 1)
    lw = lw_ref[...]
    cum = _dot_rhs_split(jnp.where(row_c >= col_c, 1.0, 0.0).astype(BF16), lw, 3)
    last = cum[c - 1:c]
    g_inv = jnp.exp(-cum)
    g_end = jnp.exp(last - cum)
    kk = kk_ref[...]
    b = b_ref[...]
    k = k_ref[...]
    at_all = -(kk * jnp.exp(cum - lw))
    rt_all = r_ref[...] * jnp.exp(cum)
    bt_all = b * g_inv
    kt_all = k * g_inv
    bh_all = b * g_end
    kh_all = k * g_end
    gc_all = jnp.exp(last)

    rowp = lax.broadcasted_iota(jnp.int32, (c, pw), 0)
    colp = lax.broadcasted_iota(jnp.int32, (c, pw), 1) % hd
    strict = rowp > colp
    incl = rowp >= colp
    eye_p = jnp.where(rowp == colp, 1.0, 0.0)
    left = lax.broadcasted_iota(jnp.int32, (1, pw), 1) < hd
    r2 = lax.broadcasted_iota(jnp.int32, (pw, pw), 0)
    c2 = lax.broadcasted_iota(jnp.int32, (pw, pw), 1)
    same_head = (r2 < hd) == (c2 < hd)
    eye2 = r2 == c2

    def bd(w):
        return jnp.concatenate([jnp.where(left, w, 0.0), jnp.where(left, 0.0, w)], axis=0)

    pairs = range(_N_PAIRS)
    cols = [slice(p * pw, (p + 1) * pw) for p in pairs]
    v_all = v_ref[...]
    h0 = [h_ref[p] for p in pairs]
    at = [at_all[:, s] for s in cols]
    rt = [rt_all[:, s] for s in cols]
    v = [v_all[:, s] for s in cols]
    amat = [_dot_nt(jnp.concatenate([at[p], rt[p]], axis=0),
                    jnp.concatenate([bd(bt_all[:, cols[p]]), bd(kt_all[:, cols[p]])], axis=0))
            for p in pairs]
    a_ab = [jnp.where(strict, a[:c, :pw], 0.0) for a in amat]
    a_ak = [jnp.where(strict, a[:c, pw:], 0.0) for a in amat]
    a_rb = [jnp.where(incl, a[c:, :pw], 0.0) for a in amat]
    a_rk = [jnp.where(incl, a[c:, pw:], 0.0) for a in amat]
    tinv = [eye_p + a for a in a_ab]
    npow = [_dot(a, bd(a)) for a in a_ab]
    span = 4
    while span < c:
        both = [_dot(jnp.concatenate([tinv[p], npow[p]], axis=0), bd(npow[p])) for p in pairs]
        tinv = [tinv[p] + both[p][:c] for p in pairs]
        npow = [x[c:] for x in both]
        span *= 2
    tinv = [tinv[p] + _dot(tinv[p], bd(npow[p])) for p in pairs]
    w12 = [_dot(jnp.concatenate([a_ak[p], a_rk[p]], axis=0), bd(v[p])) for p in pairs]
    pq = [_dot(tinv[p], jnp.concatenate([bd(at[p]), bd(w12[p][:c])], axis=1)) for p in pairs]
    y10 = [jnp.concatenate([rt[p], w12[p][c:]], axis=1)
           + _dot(a_rb[p], jnp.concatenate([bd(pq[p][:, :pw]), bd(pq[p][:, pw:])], axis=1))
           for p in pairs]
    mn = [_dot_tn(bh_all[:, cols[p]], pq[p]) for p in pairs]
    kv = [_dot_tn(kh_all[:, cols[p]], v[p]) for p in pairs]
    m = [jnp.where(same_head, mn[p][:, :pw], 0.0) + jnp.where(eye2, gc_all[:, cols[p]], 0.0) for p in pairs]
    n = [jnp.where(same_head, mn[p][:, pw:] + kv[p], 0.0) for p in pairs]
    my = [_dot_passes(jnp.concatenate([m[p], y10[p][:, :pw]], axis=0), h0[p], _WKV_STATE_PASSES)
          for p in pairs]
    h_new = [my[p][:pw] + n[p] for p in pairs]
    for p in pairs:
        h_ref[p] = h_new[p]
    y_ref[...] = jnp.concatenate([my[p][pw:] + y10[p][:, pw:] for p in pairs], axis=1)

    @pl.when(i == pl.num_programs(1) - 1)
    def _():
        for p in pairs:
            s_ref[0, p] = h_new[p]


def _wkv_prompt(r, k, v, lw, kk, b, nb_seq, seq):
    c = WKV_CHUNK
    assert c == HEAD_DIM and seq % c == 0
    nc = seq // c
    row = lambda bb, i: (bb * nc + i, 0)
    return pl.pallas_call(
        _wkv_prompt_kernel,
        grid=(nb_seq, nc),
        in_specs=[pl.BlockSpec((c, D_MODEL), row)] * 6,
        out_specs=[pl.BlockSpec((c, D_MODEL), row),
                   pl.BlockSpec((1, _N_PAIRS, _PAIR_W, _PAIR_W), lambda bb, i: (bb, 0, 0, 0))],
        out_shape=[jax.ShapeDtypeStruct((nb_seq * seq, D_MODEL), F32),
                   jax.ShapeDtypeStruct((nb_seq, _N_PAIRS, _PAIR_W, _PAIR_W), F32)],
        scratch_shapes=[pltpu.VMEM((_N_PAIRS, _PAIR_W, _PAIR_W), F32)],
        compiler_params=_cparams("parallel", "arbitrary"),
        name="wkv_prompt",
    )(r, k, v, lw, kk, b)


def _unpair_state(hb):
    n = hb.shape[0]
    h6 = hb.reshape(n, _N_PAIRS, 2, HEAD_DIM, 2, HEAD_DIM)
    blocks = jnp.stack([h6[:, :, 0, :, 0, :], h6[:, :, 1, :, 1, :]], axis=2)
    return jnp.swapaxes(blocks, -1, -2).reshape(n, N_HEADS, HEAD_DIM, HEAD_DIM)


_WKV_VROWS = 4


def _wkv_sample_kernel(r_ref, lw_ref, k_ref, v_ref, kk_ref, b_ref, s0_ref, y_ref, s_ref):
    t = r_ref.shape[1]
    vr = _WKV_VROWS

    def body(vi, carry):
        v0 = pl.multiple_of(vi * vr, vr)
        s = s0_ref[0, pl.ds(v0, vr)]
        for st in range(t):
            sa = -jnp.sum(s * kk_ref[0, st][None], axis=1, keepdims=True)
            vv = v_ref[0, st, pl.ds(v0, vr)]
            s = s * jnp.exp(lw_ref[0, st])[None] + sa * b_ref[0, st][None] + vv * k_ref[0, st][None]
            y_ref[0, st, pl.ds(v0, vr)] = jnp.sum(s * r_ref[0, st][None], axis=1, keepdims=True)
        s_ref[0, pl.ds(v0, vr)] = s
        return carry

    lax.fori_loop(0, HEAD_DIM // vr, body, 0)


def _wkv_sample(r, lw, k, v, kk, b, s0):
    nh, t, hd, n = r.shape
    vec = pl.BlockSpec((1, t, hd, n), lambda h: (h, 0, 0, 0))
    vec1 = pl.BlockSpec((1, t, hd, 1, n), lambda h: (h, 0, 0, 0, 0))
    st = pl.BlockSpec((1, hd, hd, n), lambda h: (h, 0, 0, 0))
    return pl.pallas_call(
        _wkv_sample_kernel,
        grid=(nh,),
        in_specs=[vec, vec, vec, vec1, vec, vec, st],
        out_specs=[vec1, st],
        out_shape=[jax.ShapeDtypeStruct((nh, t, hd, 1, n), F32),
                   jax.ShapeDtypeStruct((nh, hd, hd, n), F32)],
        compiler_params=_cparams("parallel"),
        name="wkv_sample",
    )(r, lw, k, v, kk, b, s0)


def _mix_out_kernel(x_ref, oa_ref, ob_ref, y_ref, r_ref, k_ref, v_ref, g_ref, wgate_ref,
                    lnw_ref, lnb_ref, rk_ref, wout_ref, g1_ref, b1_ref, o_ref, *, alpha):
    d = D_MODEL
    x = x_ref[...]
    gate = jnp.dot(x.astype(BF16), wgate_ref[...], preferred_element_type=F32)
    e, et = _head_indicator()
    y = y_ref[...]
    mu = _head_sum(y, e, et) * (1.0 / HEAD_DIM)
    yc = y - mu
    var = _head_sum(yc * yc, e, et) * (1.0 / HEAD_DIM)
    yn = yc * lax.rsqrt(var + GN_EPS) * lnw_ref[...] + lnb_ref[...]
    v = v_ref[...]
    bonus = _head_sum(r_ref[...] * k_ref[...] * rk_ref[...], e, et) * v
    oc = (yn + bonus) * g_ref[...]
    mixed = (_sigmoid(gate[:, 0:d]) * oa_ref[...] + _sigmoid(gate[:, d:2 * d]) * ob_ref[...]
             + _sigmoid(gate[:, 2 * d:3 * d]) * oc)
    z = alpha * x + jnp.dot(mixed.astype(BF16), wout_ref[...], preferred_element_type=F32)
    o_ref[...] = _layer_norm(z, g1_ref[...], b1_ref[...])


def _mix_out(x, oa, ob, y, r, k, v, g, wgate, lnw, lnb, rk, wout, g1, b1, alpha):
    m = x.shape[0]
    tm = _tile(m, 256)
    row = lambda i: (i, 0)
    full = lambda i: (0, 0)
    rows = pl.BlockSpec((tm, D_MODEL), row)
    vec = pl.BlockSpec((1, D_MODEL), full)
    resident = lambda w: pl.BlockSpec(w.shape, full, pipeline_mode=pl.Buffered(1))
    return pl.pallas_call(
        functools.partial(_mix_out_kernel, alpha=alpha),
        grid=(m // tm,),
        in_specs=[rows] * 8 + [resident(wgate), vec, vec, vec, resident(wout), vec, vec],
        out_specs=rows,
        out_shape=jax.ShapeDtypeStruct((m, D_MODEL), F32),
        compiler_params=_cparams("parallel"),
        name="mix_out",
    )(x, oa, ob, y, r, k, v, g, wgate, lnw, lnb, rk, wout, g1, b1)


def _ffn_kernel(x_ref, wg_ref, wu_ref, wd_ref, g2_ref, b2_ref, o_ref, *, alpha):
    x = x_ref[...]
    xb = x.astype(BF16)
    gt = jnp.dot(xb, wg_ref[...], preferred_element_type=F32)
    up = jnp.dot(xb, wu_ref[...], preferred_element_type=F32)
    ff = jnp.dot((gt * _sigmoid(gt) * up).astype(BF16), wd_ref[...], preferred_element_type=F32)
    o_ref[...] = _layer_norm(alpha * x + ff, g2_ref[...], b2_ref[...])


def _ffn(x, wg, wu, wd, g2, b2, alpha):
    m = x.shape[0]
    tm = _tile(m, 256)
    row = lambda i: (i, 0)
    full = lambda i: (0, 0)
    resident = lambda w: pl.BlockSpec(w.shape, full, pipeline_mode=pl.Buffered(1))
    return pl.pallas_call(
        functools.partial(_ffn_kernel, alpha=alpha),
        grid=(m // tm,),
        in_specs=[pl.BlockSpec((tm, D_MODEL), row), resident(wg), resident(wu), resident(wd),
                  pl.BlockSpec((1, D_MODEL), full), pl.BlockSpec((1, D_MODEL), full)],
        out_specs=pl.BlockSpec((tm, D_MODEL), row),
        out_shape=jax.ShapeDtypeStruct((m, D_MODEL), F32),
        compiler_params=_cparams("parallel"),
        name="ffn",
    )(x, wg, wu, wd, g2, b2)


def _blockdiag_pairs(w):
    nb, bs, _ = w.shape
    w = w.reshape(nb // 2, 2, bs, bs)
    z = jnp.zeros((nb // 2, bs, bs), w.dtype)
    top = jnp.concatenate([w[:, 0], z], axis=2)
    bot = jnp.concatenate([z, w[:, 1]], axis=2)
    return jnp.concatenate([top, bot], axis=1)


def _row(p):
    return p.reshape(1, -1)


def kernel(x_prompt, x_sample, cache_k, cache_v, state_conv, state_rglru, state_shift, state_wkv, w_in, attn_sinks, conv_w, conv_b, rg_wa, rg_ba, rg_wx, rg_bx, rg_lambda, rw_mu, rw_w0, rw_wup, rw_a0, rw_aup, rw_gup, rw_kk, rw_ka, rw_rk, rw_lnw, rw_lnb, w_out, ln1_g, ln1_b, w_gu, w_down, ln2_g, ln2_b):
    depth = w_in.shape[0]
    bp, seq, d = x_prompt.shape
    ns, ts, _ = x_sample.shape
    wb = cache_k.shape[2]
    alpha = (2 * depth) ** 0.25
    dff = w_down.shape[1]
    pad = SHIFT_PAD - SHIFT_W

    xp = x_prompt.reshape(bp * seq, d)
    xs = jnp.swapaxes(x_sample, 0, 1).reshape(ts * ns, d)
    st_p = [[] for _ in range(6)]
    st_s = [[] for _ in range(6)]

    for l in range(depth):
        wl = w_in[l]
        w_all = jnp.pad(wl[:, :OFF_GATE], ((0, 0), (0, pad))).astype(BF16)
        w_gate = wl[:, OFF_GATE:].astype(BF16)
        wa2 = _blockdiag_pairs(rg_wa[l]).astype(BF16)
        wx2 = _blockdiag_pairs(rg_wx[l]).astype(BF16)
        rg_args = (conv_w[l], _row(conv_b[l]), wa2, _row(rg_ba[l]), wx2, _row(rg_bx[l]), _row(rg_lambda[l]))
        zw = jnp.zeros((LORA_W, d), F32)
        wwa = jnp.concatenate([jnp.concatenate([rw_wup[l], zw], axis=1),
                               jnp.concatenate([zw, rw_aup[l]], axis=1)], axis=0).astype(BF16)
        gup = jnp.pad(rw_gup[l], ((0, 2 * LANES - LORA_G), (0, 0))).astype(BF16)
        prep_params = (_row(jnp.pad(rw_mu[l], (0, pad))), _row(rw_w0[l]), _row(rw_a0[l]), _row(rw_kk[l]),
                       _row(rw_ka[l]), wwa, gup)
        mix_params = (_row(rw_lnw[l]), _row(rw_lnb[l]), _row(rw_rk[l]), w_out[l].astype(BF16),
                      _row(ln1_g[l]), _row(ln1_b[l]))
        wg = w_gu[l][:, :dff].astype(BF16)
        wu = w_gu[l][:, dff:].astype(BF16)
        wd = w_down[l].astype(BF16)
        ffn_params = (wg, wu, wd, _row(ln2_g[l]), _row(ln2_b[l]))
        sinks = attn_sinks[l]

        qkv, rnn, rw = _in_proj(xp, w_all)
        oa = _attn_prompt(qkv, sinks, bp, seq)
        ob = _rglru_prompt(rnn, bp, seq, *rg_args)
        r, k, v, lw, kk, b, g = _rwkv_prep_prompt(rw, bp, seq, prep_params)
        y, hfin = _wkv_prompt(r, k, v, lw, kk, b, bp, seq)
        x1 = _mix_out(xp, oa, ob, y, r, k, v, g, w_gate, *mix_params, alpha)
        xp = _ffn(x1, *ffn_params, alpha)
        qkv3 = qkv.reshape(bp, seq, OFF_RNN)
        st_p[0].append(qkv3[:, seq - wb:, OFF_K:OFF_V].reshape(bp, wb, N_KV_HEADS, HEAD_DIM))
        st_p[1].append(qkv3[:, seq - wb:, OFF_V:OFF_RNN].reshape(bp, wb, N_KV_HEADS, HEAD_DIM))
        st_p[2].append(rnn.reshape(bp, seq, d)[:, seq - (CONV_W - 1):])
        st_p[3].append(ob.reshape(bp, seq, d)[:, seq - 1])
        st_p[4].append(rw.reshape(bp, seq, SHIFT_PAD)[:, seq - 1, :SHIFT_W])
        st_p[5].append(_unpair_state(hfin))

        qkv, rnn, rw = _in_proj(xs, w_all)
        qkv_n = jnp.swapaxes(qkv.reshape(ts, ns, OFF_RNN), 0, 1)
        oa_n, new_k, new_v = _attn_sample(qkv_n[:, :, :OFF_K], qkv_n[:, :, OFF_K:OFF_V], qkv_n[:, :, OFF_V:],
                                          cache_k[l].reshape(ns, wb, KV_W), cache_v[l].reshape(ns, wb, KV_W),
                                          sinks)
        oa = jnp.swapaxes(oa_n, 0, 1).reshape(ts * ns, d)
        rnn3 = rnn.reshape(ts, ns, d)
        ob3 = _rglru_sample(rnn3, jnp.swapaxes(state_conv[l], 0, 1), state_rglru[l], *rg_args)
        ob = ob3.reshape(ts * ns, d)
        shift = jnp.pad(state_shift[l], ((0, 0), (0, pad)))
        r, k, v, lw, kk, b, g = _rwkv_prep_sample(rw, shift, ts, ns, prep_params)

        def lanes(z):
            return jnp.transpose(z.reshape(ts, ns, N_HEADS, HEAD_DIM), (2, 0, 3, 1))

        s0 = jnp.transpose(state_wkv[l], (1, 2, 3, 0))
        y5, s_new = _wkv_sample(lanes(r), lanes(lw), lanes(k),
                                lanes(v).reshape(N_HEADS, ts, HEAD_DIM, 1, ns), lanes(kk), lanes(b), s0)
        y = jnp.transpose(y5.reshape(N_HEADS, ts, HEAD_DIM, ns), (1, 3, 0, 2)).reshape(ts * ns, d)
        x1 = _mix_out(xs, oa, ob, y, r, k, v, g, w_gate, *mix_params, alpha)
        xs = _ffn(x1, *ffn_params, alpha)
        st_s[0].append(new_k.reshape(ns, wb, N_KV_HEADS, HEAD_DIM))
        st_s[1].append(new_v.reshape(ns, wb, N_KV_HEADS, HEAD_DIM))
        st_s[2].append(jnp.swapaxes(rnn3[ts - (CONV_W - 1):], 0, 1))
        st_s[3].append(ob3[ts - 1])
        st_s[4].append(rw.reshape(ts, ns, SHIFT_PAD)[ts - 1, :, :SHIFT_W])
        st_s[5].append(jnp.transpose(s_new, (3, 0, 1, 2)))

    yp = xp.reshape(bp, seq, d)
    ys = jnp.swapaxes(xs.reshape(ts, ns, d), 0, 1)
    return (yp, ys, *(jnp.stack(s) for s in st_p), *(jnp.stack(s) for s in st_s))
```

```python
import functools
import math

import jax
import jax.numpy as jnp
from jax import lax
from jax.experimental import pallas as pl
from jax.experimental.pallas import tpu as pltpu

F32 = jnp.float32
BF16 = jnp.bfloat16

D_MODEL = 1024
HEAD_DIM = 64
N_HEADS = 16
N_KV_HEADS = 4
KV_W = N_KV_HEADS * HEAD_DIM
WINDOW = 128
CONV_W = 4
RG_C = 8.0
LORA_W = 64
LORA_A = 64
LORA_G = 160
SHIFT_W = 3 * D_MODEL + LORA_W + LORA_A + LORA_G
LANES = 128
SHIFT_PAD = -(-SHIFT_W // LANES) * LANES
GN_EPS = 64e-5
LN_EPS = 1e-5
OFF_K = D_MODEL
OFF_V = OFF_K + KV_W
OFF_RNN = OFF_V + KV_W
OFF_RW = OFF_RNN + D_MODEL
OFF_GATE = OFF_RW + SHIFT_W
WKV_CHUNK = 64
WKV_CHUNKS_PER_TILE = 4
VMEM_LIMIT = 48 * 1024 * 1024


def _cparams(*sem):
    return pltpu.CompilerParams(dimension_semantics=sem, vmem_limit_bytes=VMEM_LIMIT)


def _tile(n, pref):
    if n <= pref:
        return n
    for t in range(pref, 7, -1):
        if n % t == 0 and t % 8 == 0:
            return t
    return n


def _dot(a, b):
    return jnp.dot(a.astype(BF16), b.astype(BF16), preferred_element_type=F32)


def _split_terms(x, n):
    terms = []
    rem = x
    for _ in range(n):
        hi = rem.astype(BF16)
        terms.append(hi)
        rem = rem - hi.astype(F32)
    return terms


def _dot_lhs_split(x, w_bf16, n):
    acc = None
    for t in _split_terms(x, n):
        p = jnp.dot(t, w_bf16, preferred_element_type=F32)
        acc = p if acc is None else acc + p
    return acc


def _dot_rhs_split(w_bf16, x, n):
    acc = None
    for t in _split_terms(x, n):
        p = jnp.dot(w_bf16, t, preferred_element_type=F32)
        acc = p if acc is None else acc + p
    return acc


def _sigmoid(x):
    return 0.5 * jnp.tanh(0.5 * x) + 0.5


def _softplus(x):
    return jnp.maximum(x, 0.0) + jnp.log(1.0 + jnp.exp(-jnp.abs(x)))


def _layer_norm(z, g, b):
    mu = jnp.mean(z, axis=-1, keepdims=True)
    zc = z - mu
    var = jnp.mean(zc * zc, axis=-1, keepdims=True)
    return zc * lax.rsqrt(var + LN_EPS) * g + b


def _head_indicator():
    c = lax.broadcasted_iota(jnp.int32, (D_MODEL, LANES), 0) // HEAD_DIM
    h = lax.broadcasted_iota(jnp.int32, (D_MODEL, LANES), 1)
    e = jnp.where(c == h, 1.0, 0.0).astype(BF16)
    ct = lax.broadcasted_iota(jnp.int32, (LANES, D_MODEL), 1) // HEAD_DIM
    ht = lax.broadcasted_iota(jnp.int32, (LANES, D_MODEL), 0)
    et = jnp.where(ct == ht, 1.0, 0.0).astype(BF16)
    return e, et


def _head_sum(x, e, et):
    s = _dot_lhs_split(x, e, 2)
    return _dot_lhs_split(s, et, 2)


_PROJ_WIDTHS = (OFF_RNN, D_MODEL)


def _in_proj_kernel(x_ref, w_ref, *o_refs):
    xb = x_ref[...].astype(BF16)
    off = 0
    for o_ref in o_refs:
        n = o_ref.shape[1]
        o_ref[...] = jnp.dot(xb, w_ref[:, off:off + n], preferred_element_type=F32)
        off += n


def _in_proj(x, w):
    m, k = x.shape
    tm = _tile(m, 256)
    return pl.pallas_call(
        _in_proj_kernel,
        grid=(m // tm,),
        in_specs=[pl.BlockSpec((tm, k), lambda i: (i, 0)),
                  pl.BlockSpec(w.shape, lambda i: (0, 0), pipeline_mode=pl.Buffered(1))],
        out_specs=[pl.BlockSpec((tm, n), lambda i: (i, 0)) for n in _PROJ_WIDTHS],
        out_shape=[jax.ShapeDtypeStruct((m, n), F32) for n in _PROJ_WIDTHS],
        compiler_params=_cparams("parallel"),
        name="in_proj",
    )(x, w)


def _alibi_slope(h):
    return 2.0 ** (-8.0 * (h + 1) / N_HEADS)


def _attn_prompt_bias():
    w = WINDOW
    rel = w + jnp.arange(w)[:, None] - jnp.arange(2 * w)[None, :]
    band = (rel >= 0) & (rel < WINDOW)
    has_prev = jnp.arange(2)[:, None, None] > 0
    valid = band[None] & (has_prev | (jnp.arange(2 * w) >= w)[None, None, :])
    slopes = jnp.array([_alibi_slope(h) for h in range(N_HEADS)], F32)
    bias = -slopes[None, :, None, None] * rel.astype(F32)[None, None]
    return jnp.where(valid[:, None], bias, -jnp.inf)


def _attn_prompt_kernel(sink_ref, bias_ref, q_ref, kc_ref, vc_ref, kp_ref, vp_ref, o_ref):
    q = q_ref[...] * (HEAD_DIM ** -0.5)
    kcat = jnp.concatenate([kp_ref[...], kc_ref[...]], axis=0).astype(BF16)
    vcat = jnp.concatenate([vp_ref[...], vc_ref[...]], axis=0).astype(BF16)
    heads = range(N_HEADS)
    gsz = N_HEADS // N_KV_HEADS
    kg = [kcat[:, g * HEAD_DIM:(g + 1) * HEAD_DIM] for g in range(N_KV_HEADS)]
    vg = [vcat[:, g * HEAD_DIM:(g + 1) * HEAD_DIM] for g in range(N_KV_HEADS)]
    qh = [q[:, h * HEAD_DIM:(h + 1) * HEAD_DIM].astype(BF16) for h in heads]
    s = [lax.dot_general(qh[h], kg[h // gsz], (((1,), (1,)), ((), ())), preferred_element_type=F32)
         + bias_ref[0, h] for h in heads]
    m = [jnp.maximum(jnp.max(s[h], axis=-1, keepdims=True), sink_ref[h]) for h in heads]
    e = [jnp.exp(s[h] - m[h]) for h in heads]
    den = [jnp.sum(e[h], axis=-1, keepdims=True) + jnp.exp(sink_ref[h] - m[h]) for h in heads]
    p = [(e[h] / den[h]).astype(BF16) for h in heads]
    outs = [jnp.dot(p[h], vg[h // gsz], preferred_element_type=F32) for h in heads]
    o_ref[...] = jnp.concatenate(outs, axis=1)


def _attn_prompt(qkv, sinks, nb_seq, seq):
    w = WINDOW
    nb = seq // w
    kcol = OFF_K // KV_W
    vcol = OFF_V // KV_W

    def cur(c):
        return lambda b, j: (b * nb + j, c)

    def prev(c):
        return lambda b, j: (jnp.maximum(b * nb + j - 1, 0), c)

    return pl.pallas_call(
        _attn_prompt_kernel,
        grid=(nb_seq, nb),
        in_specs=[pl.BlockSpec(memory_space=pltpu.SMEM),
                  pl.BlockSpec((1, N_HEADS, w, 2 * w), lambda b, j: (jnp.minimum(j, 1), 0, 0, 0)),
                  pl.BlockSpec((w, D_MODEL), cur(0)),
                  pl.BlockSpec((w, KV_W), cur(kcol)),
                  pl.BlockSpec((w, KV_W), cur(vcol)),
                  pl.BlockSpec((w, KV_W), prev(kcol)),
                  pl.BlockSpec((w, KV_W), prev(vcol))],
        out_specs=pl.BlockSpec((w, D_MODEL), cur(0)),
        out_shape=jax.ShapeDtypeStruct((nb_seq * seq, D_MODEL), F32),
        compiler_params=_cparams("parallel", "parallel"),
        name="attn_prompt",
    )(sinks, _attn_prompt_bias(), qkv, qkv, qkv, qkv, qkv)


def _attn_sample_kernel(sink_ref, q_ref, kn_ref, vn_ref, ck_ref, cv_ref, o_ref, nk_ref, nv_ref, *, t):
    sn = q_ref.shape[0]
    wb = ck_ref.shape[1]
    gsz = N_HEADS // N_KV_HEADS
    rows = gsz * t
    q = q_ref[...]
    kn = kn_ref[...]
    vn = vn_ref[...]
    ck = ck_ref[...]
    cv = cv_ref[...]
    tq1 = lax.broadcasted_iota(jnp.int32, (rows, wb), 0) % t
    j1 = lax.broadcasted_iota(jnp.int32, (rows, wb), 1)
    rel1 = wb + tq1 - j1
    valid1 = (rel1 >= 0) & (rel1 < WINDOW)
    tq2 = lax.broadcasted_iota(jnp.int32, (rows, t), 0) % t
    s2i = lax.broadcasted_iota(jnp.int32, (rows, t), 1)
    rel2 = tq2 - s2i
    valid2 = (rel2 >= 0) & (rel2 < WINDOW)
    hrow = lax.broadcasted_iota(jnp.int32, (rows, 1), 0) // t
    outs = []
    for g in range(N_KV_HEADS):
        slope = jnp.zeros((rows, 1), F32)
        sink = jnp.zeros((rows, 1), F32)
        for hh in range(gsz):
            h = g * gsz + hh
            slope = jnp.where(hrow == hh, _alibi_slope(h), slope)
            sink = jnp.where(hrow == hh, sink_ref[h], sink)
        qg = jnp.concatenate([q[:, :, (g * gsz + hh) * HEAD_DIM:(g * gsz + hh + 1) * HEAD_DIM]
                              for hh in range(gsz)], axis=1).astype(BF16)
        lo, hi = g * HEAD_DIM, (g + 1) * HEAD_DIM
        ckg = ck[:, :, lo:hi].astype(BF16)
        cvg = cv[:, :, lo:hi].astype(BF16)
        kng = kn[:, :, lo:hi].astype(BF16)
        vng = vn[:, :, lo:hi].astype(BF16)
        nt = (((2,), (2,)), ((0,), (0,)))
        nn = (((2,), (1,)), ((0,), (0,)))
        s1 = lax.dot_general(qg, ckg, nt, preferred_element_type=F32) * (HEAD_DIM ** -0.5)
        s2 = lax.dot_general(qg, kng, nt, preferred_element_type=F32) * (HEAD_DIM ** -0.5)
        s1 = jnp.where(valid1[None], s1 - (slope * rel1.astype(F32))[None], -jnp.inf)
        s2 = jnp.where(valid2[None], s2 - (slope * rel2.astype(F32))[None], -jnp.inf)
        m = jnp.maximum(jnp.maximum(jnp.max(s1, axis=-1, keepdims=True),
                                    jnp.max(s2, axis=-1, keepdims=True)), sink[None])
        e1 = jnp.exp(s1 - m)
        e2 = jnp.exp(s2 - m)
        den = (jnp.sum(e1, axis=-1, keepdims=True) + jnp.sum(e2, axis=-1, keepdims=True)
               + jnp.exp(sink[None] - m))
        p1 = (e1 / den).astype(BF16)
        p2 = (e2 / den).astype(BF16)
        og = (lax.dot_general(p1, cvg, nn, preferred_element_type=F32)
              + lax.dot_general(p2, vng, nn, preferred_element_type=F32))
        outs.extend(og[:, hh * t:(hh + 1) * t, :] for hh in range(gsz))
    o_ref[...] = jnp.concatenate(outs, axis=2)
    nk_ref[...] = jnp.concatenate([ck[:, t:, :], kn], axis=1)
    nv_ref[...] = jnp.concatenate([cv[:, t:, :], vn], axis=1)


def _attn_sample(q, kn, vn, ck, cv, sinks):
    n, t, _ = q.shape
    wb = ck.shape[1]
    sn = _tile(n, 16)
    blk = lambda d1, d2: pl.BlockSpec((sn, d1, d2), lambda i: (i, 0, 0))
    return pl.pallas_call(
        functools.partial(_attn_sample_kernel, t=t),
        grid=(n // sn,),
        in_specs=[pl.BlockSpec(memory_space=pltpu.SMEM),
                  blk(t, D_MODEL), blk(t, KV_W), blk(t, KV_W), blk(wb, KV_W), blk(wb, KV_W)],
        out_specs=[blk(t, D_MODEL), blk(wb, KV_W), blk(wb, KV_W)],
        out_shape=[jax.ShapeDtypeStruct((n, t, D_MODEL), F32),
                   jax.ShapeDtypeStruct((n, wb, KV_W), F32),
                   jax.ShapeDtypeStruct((n, wb, KV_W), F32)],
        compiler_params=_cparams("parallel"),
        name="attn_sample",
    )(sinks, q, kn, vn, ck, cv)


def _rglru_gates(xc, wa_ref, ba, wx_ref, bx, lam):
    xb = xc.astype(BF16)
    npair = D_MODEL // LANES
    ga = jnp.concatenate([jnp.dot(xb[:, j * LANES:(j + 1) * LANES], wa_ref[j], preferred_element_type=F32)
                          for j in range(npair)], axis=1)
    gx = jnp.concatenate([jnp.dot(xb[:, j * LANES:(j + 1) * LANES], wx_ref[j], preferred_element_type=F32)
                          for j in range(npair)], axis=1)
    r = _sigmoid(ga + ba)
    ig = _sigmoid(gx + bx)
    log_a = -RG_C * r * _softplus(-lam)
    a = jnp.exp(log_a)
    u = jnp.sqrt(-jnp.tanh(log_a) * (a * a + 1.0)) * (ig * xc)
    return a, u


def _rglru_prompt_kernel(x_ref, cw_ref, cb_ref, wa_ref, ba_ref, wx_ref, bx_ref, lam_ref, h_ref,
                         tail_ref, hc_ref):
    i = pl.program_id(1)
    tc = x_ref.shape[0]

    @pl.when(i == 0)
    def _():
        tail_ref[...] = jnp.zeros_like(tail_ref)
        hc_ref[...] = jnp.zeros_like(hc_ref)

    x = x_ref[...]
    xp = jnp.concatenate([tail_ref[...], x], axis=0)
    cw = cw_ref[...]
    xc = cb_ref[...] + x * cw[CONV_W - 1:CONV_W]
    for d in range(1, CONV_W):
        xc = xc + xp[8 - d:8 - d + tc] * cw[CONV_W - 1 - d:CONV_W - d]
    a, u = _rglru_gates(xc, wa_ref, ba_ref[...], wx_ref, bx_ref[...], lam_ref[...])
    ng = tc // 8
    acc_a = a.reshape(ng, 8, D_MODEL)
    acc_h = u.reshape(ng, 8, D_MODEL)
    sub = lax.broadcasted_iota(jnp.int32, (1, 8, D_MODEL), 1)
    for d in (1, 2, 4):
        keep = sub >= d
        sh_a = jnp.where(keep, pltpu.roll(acc_a, d, axis=1), 1.0)
        sh_h = jnp.where(keep, pltpu.roll(acc_h, d, axis=1), 0.0)
        acc_h = acc_h + acc_a * sh_h
        acc_a = acc_a * sh_a
    carry = hc_ref[...]
    for g in range(ng):
        hg = acc_h[g] + acc_a[g] * carry
        h_ref[g * 8:(g + 1) * 8, :] = hg
        carry = hg[7:8]
    hc_ref[...] = carry
    tail_ref[...] = x[tc - 8:tc]


def _rglru_prompt(x, nb_seq, seq, cw, cb, wa2, ba, wx2, bx, lam):
    tc = _tile(seq, 256)
    nt = seq // tc
    row = lambda b, i: (b * nt + i, 0)
    full2 = lambda b, i: (0, 0)
    full3 = lambda b, i: (0, 0, 0)
    return pl.pallas_call(
        _rglru_prompt_kernel,
        grid=(nb_seq, nt),
        in_specs=[pl.BlockSpec((tc, D_MODEL), row),
                  pl.BlockSpec((CONV_W, D_MODEL), full2), pl.BlockSpec((1, D_MODEL), full2),
                  pl.BlockSpec(wa2.shape, full3), pl.BlockSpec((1, D_MODEL), full2),
                  pl.BlockSpec(wx2.shape, full3), pl.BlockSpec((1, D_MODEL), full2),
                  pl.BlockSpec((1, D_MODEL), full2)],
        out_specs=pl.BlockSpec((tc, D_MODEL), row),
        out_shape=jax.ShapeDtypeStruct((nb_seq * seq, D_MODEL), F32),
        scratch_shapes=[pltpu.VMEM((8, D_MODEL), F32), pltpu.VMEM((1, D_MODEL), F32)],
        compiler_params=_cparams("parallel", "arbitrary"),
        name="rglru_prompt",
    )(x, cw, cb, wa2, ba, wx2, bx, lam)


def _rglru_sample_kernel(x_ref, cs_ref, h0_ref, cw_ref, cb_ref, wa_ref, ba_ref, wx_ref, bx_ref, lam_ref,
                         h_ref):
    t, sn, _ = x_ref.shape
    cw = cw_ref[...]
    nst = CONV_W - 1

    def slab(s):
        return x_ref[s] if s >= 0 else cs_ref[nst + s]

    xcs = []
    for s in range(t):
        xc = cb_ref[...] + slab(s) * cw[CONV_W - 1:CONV_W]
        for d in range(1, CONV_W):
            xc = xc + slab(s - d) * cw[CONV_W - 1 - d:CONV_W - d]
        xcs.append(xc)
    xc_all = jnp.concatenate(xcs, axis=0)
    a, u = _rglru_gates(xc_all, wa_ref, ba_ref[...], wx_ref, bx_ref[...], lam_ref[...])
    h = h0_ref[...]
    for s in range(t):
        h = a[s * sn:(s + 1) * sn] * h + u[s * sn:(s + 1) * sn]
        h_ref[s] = h


def _rglru_sample(x, cs, h0, cw, cb, wa2, ba, wx2, bx, lam):
    t, n, _ = x.shape
    sn = _tile(n, 32)
    full2 = lambda i: (0, 0)
    full3 = lambda i: (0, 0, 0)
    return pl.pallas_call(
        _rglru_sample_kernel,
        grid=(n // sn,),
        in_specs=[pl.BlockSpec((t, sn, D_MODEL), lambda i: (0, i, 0)),
                  pl.BlockSpec((CONV_W - 1, sn, D_MODEL), lambda i: (0, i, 0)),
                  pl.BlockSpec((sn, D_MODEL), lambda i: (i, 0)),
                  pl.BlockSpec((CONV_W, D_MODEL), full2), pl.BlockSpec((1, D_MODEL), full2),
                  pl.BlockSpec(wa2.shape, full3), pl.BlockSpec((1, D_MODEL), full2),
                  pl.BlockSpec(wx2.shape, full3), pl.BlockSpec((1, D_MODEL), full2),
                  pl.BlockSpec((1, D_MODEL), full2)],
        out_specs=pl.BlockSpec((t, sn, D_MODEL), lambda i: (0, i, 0)),
        out_shape=jax.ShapeDtypeStruct((t, n, D_MODEL), F32),
        compiler_params=_cparams("parallel"),
        name="rglru_sample",
    )(x, cs, h0, cw, cb, wa2, ba, wx2, bx, lam)


def _rwkv_prep_math(pc, prev, mu, w0, a0, kkp, ka, wwa, gup):
    d = D_MODEL
    ps = pc + mu * (prev - pc)
    r = ps[:, 0:d]
    k = ps[:, d:2 * d]
    v = ps[:, 2 * d:3 * d]
    l01 = ps[:, 3 * d:3 * d + LANES]
    lane = lax.broadcasted_iota(jnp.int32, (1, LANES), 1)
    t01 = jnp.where(lane < LORA_W, jnp.tanh(l01), l01)
    wa = _dot(t01, wwa)
    g = _dot(_sigmoid(ps[:, 3 * d + LANES:]), gup)
    wlog = -_softplus(-(w0 + wa[:, :d])) - 0.5
    logw = -jnp.exp(wlog)
    a = _sigmoid(a0 + wa[:, d:])
    kk = k * kkp
    e, et = _head_indicator()
    nrm = jnp.maximum(jnp.sqrt(_head_sum(kk * kk, e, et)), 1e-12)
    kkn = kk / nrm
    kmod = k * (1.0 + (a - 1.0) * ka)
    return r, kmod, v, logw, kkn, kkn * a, g


_N_PREP_OUT = 7


def _rwkv_front_kernel(x_ref, w_ref, sh0_ref, mu_ref, w0_ref, a0_ref, kkp_ref, ka_ref, wwa_ref, gup_ref,
                       *refs, shift_rows):
    out_refs, sh_out_ref, carry_ref = refs[:_N_PREP_OUT], refs[_N_PREP_OUT], refs[_N_PREP_OUT + 1]
    i = pl.program_id(1)
    tm = x_ref.shape[0]
    ncar = carry_ref.shape[0]

    @pl.when(i == 0)
    def _():
        carry_ref[...] = sh0_ref[0]

    pc = jnp.dot(x_ref[...].astype(BF16), w_ref[...], preferred_element_type=F32)
    prev = jnp.concatenate([carry_ref[ncar - shift_rows:, :], pc[:tm - shift_rows]], axis=0)
    outs = _rwkv_prep_math(pc, prev, mu_ref[...], w0_ref[...], a0_ref[...], kkp_ref[...], ka_ref[...],
                           wwa_ref[...], gup_ref[...])
    for o_ref, o in zip(out_refs, outs):
        o_ref[...] = o
    carry_ref[...] = pc[tm - ncar:]

    @pl.when(i == pl.num_programs(1) - 1)
    def _():
        sh_out_ref[0] = pc[tm - ncar:]


def _rwkv_front(x, w_rw, shift0, n_groups, shift_rows, params):
    rows = x.shape[0] // n_groups
    ncar = shift0.shape[1]
    tm = _tile(rows, 256)
    assert tm >= ncar >= shift_rows and tm % shift_rows == 0
    nt = rows // tm
    row = lambda g, i: (g * nt + i, 0)
    full = lambda g, i: (0, 0)
    grp = pl.BlockSpec((1, ncar, SHIFT_PAD), lambda g, i: (g, 0, 0))
    res = pl.pallas_call(
        functools.partial(_rwkv_front_kernel, shift_rows=shift_rows),
        grid=(n_groups, nt),
        in_specs=[pl.BlockSpec((tm, D_MODEL), row),
                  pl.BlockSpec(w_rw.shape, full, pipeline_mode=pl.Buffered(1)), grp]
                 + [pl.BlockSpec(p.shape, full) for p in params],
        out_specs=[pl.BlockSpec((tm, D_MODEL), row)] * _N_PREP_OUT + [grp],
        out_shape=[jax.ShapeDtypeStruct((n_groups * rows, D_MODEL), F32)] * _N_PREP_OUT
                  + [jax.ShapeDtypeStruct((n_groups, ncar, SHIFT_PAD), F32)],
        scratch_shapes=[pltpu.VMEM((ncar, SHIFT_PAD), F32)],
        compiler_params=_cparams("parallel", "arbitrary"),
        name="rwkv_front",
    )(x, w_rw, shift0, *params)
    return res[:_N_PREP_OUT], res[_N_PREP_OUT]


_PAIR_W = 2 * HEAD_DIM
_N_PAIRS = D_MODEL // _PAIR_W
_WKV_STATE_PASSES = 1


def _dot_tn(a, b):
    return lax.dot_general(a.astype(BF16), b.astype(BF16), (((0,), (0,)), ((), ())),
                           preferred_element_type=F32)


def _dot_nt(a, b):
    return lax.dot_general(a.astype(BF16), b.astype(BF16), (((1,), (1,)), ((), ())),
                           preferred_element_type=F32)


def _dot_passes(a, b, passes):
    if passes == 1:
        return _dot(a, b)
    ah, al = _split_terms(a, 2)
    bh, bl = _split_terms(b, 2)
    return (jnp.dot(ah, bh, preferred_element_type=F32) + jnp.dot(ah, bl, preferred_element_type=F32)
            + jnp.dot(al, bh, preferred_element_type=F32))


def _wkv_prompt_kernel(r_ref, k_ref, v_ref, lw_ref, kk_ref, b_ref, y_ref, s_ref, h_ref):
    i = pl.program_id(1)
    rows = r_ref.shape[0]
    c = WKV_CHUNK
    nck = rows // c
    hd = HEAD_DIM
    pw = _PAIR_W

    @pl.when(i == 0)
    def _():
        h_ref[...] = jnp.zeros_like(h_ref)

    row_t = lax.broadcasted_iota(jnp.int32, (rows, rows), 0)
    col_t = lax.broadcasted_iota(jnp.int32, (rows, rows), 1)
    tri = (row_t >= col_t) & (row_t // c == col_t // c)
    lw = lw_ref[...]
    cum = _dot_rhs_split(jnp.where(tri, 1.0, 0.0).astype(BF16), lw, 3)
    last = jnp.concatenate([jnp.broadcast_to(cum[(ci + 1) * c - 1:(ci + 1) * c], (c, D_MODEL))
                            for ci in range(nck)], axis=0)
    g_inv = jnp.exp(-cum)
    g_end = jnp.exp(last - cum)
    kk = kk_ref[...]
    b = b_ref[...]
    k = k_ref[...]
    at_all = -(kk * jnp.exp(cum - lw))
    rt_all = r_ref[...] * jnp.exp(cum)
    bt_all = b * g_inv
    kt_all = k * g_inv
    bh_all = b * g_end
    kh_all = k * g_end
    gc_all = jnp.exp(last)

    rowp = lax.broadcasted_iota(jnp.int32, (c, pw), 0)
    colp = lax.broadcasted_iota(jnp.int32, (c, pw), 1) % hd
    strict = rowp > colp
    incl = rowp >= colp
    eye_p = jnp.where(rowp == colp, 1.0, 0.0)
    left = lax.broadcasted_iota(jnp.int32, (1, pw), 1) < hd
    r2 = lax.broadcasted_iota(jnp.int32, (pw, pw), 0)
    c2 = lax.broadcasted_iota(jnp.int32, (pw, pw), 1)
    same_head = (r2 < hd) == (c2 < hd)
    eye2 = r2 == c2

    def bd(w):
        return jnp.concatenate([jnp.where(left, w, 0.0), jnp.where(left, 0.0, w)], axis=0)

    pairs = range(_N_PAIRS)
    items = [(ci, p) for ci in range(nck) for p in pairs]
    idx = range(len(items))
    sub = [(slice(ci * c, (ci + 1) * c), slice(p * pw, (p + 1) * pw)) for ci, p in items]
    v_all = v_ref[...]
    at = [at_all[s] for s in sub]
    rt = [rt_all[s] for s in sub]
    v = [v_all[s] for s in sub]
    amat = [_dot_nt(jnp.concatenate([at[j], rt[j]], axis=0),
                    jnp.concatenate([bd(bt_all[sub[j]]), bd(kt_all[sub[j]])], axis=0))
            for j in idx]
    a_ab = [jnp.where(strict, a[:c, :pw], 0.0) for a in amat]
    a_ak = [jnp.where(strict, a[:c, pw:], 0.0) for a in amat]
    a_rb = [jnp.where(incl, a[c:, :pw], 0.0) for a in amat]
    a_rk = [jnp.where(incl, a[c:, pw:], 0.0) for a in amat]
    tinv = [eye_p + a for a in a_ab]
    npow = [_dot(a, bd(a)) for a in a_ab]
    span = 4
    while span < c:
        both = [_dot(jnp.concatenate([tinv[j], npow[j]], axis=0), bd(npow[j])) for j in idx]
        tinv = [tinv[j] + both[j][:c] for j in idx]
        npow = [x[c:] for x in both]
        span *= 2
    tinv = [tinv[j] + _dot(tinv[j], bd(npow[j])) for j in idx]
    w12 = [_dot(jnp.concatenate([a_ak[j], a_rk[j]], axis=0), bd(v[j])) for j in idx]
    pq = [_dot(tinv[j], jnp.concatenate([bd(at[j]), bd(w12[j][:c])], axis=1)) for j in idx]
    y10 = [jnp.concatenate([rt[j], w12[j][c:]], axis=1)
           + _dot(a_rb[j], jnp.concatenate([bd(pq[j][:, :pw]), bd(pq[j][:, pw:])], axis=1))
           for j in idx]
    mn = [_dot_tn(bh_all[sub[j]], pq[j]) for j in idx]
    kv = [_dot_tn(kh_all[sub[j]], v[j]) for j in idx]
    m = [jnp.where(same_head, mn[j][:, :pw], 0.0)
         + jnp.where(eye2, gc_all[sub[j][0], sub[j][1]][:1], 0.0) for j in idx]
    n = [jnp.where(same_head, mn[j][:, pw:] + kv[j], 0.0) for j in idx]
    h = [h_ref[p] for p in pairs]
    y_rows = []
    for ci in range(nck):
        js = [ci * _N_PAIRS + p for p in pairs]
        my = [_dot_passes(jnp.concatenate([m[j], y10[j][:, :pw]], axis=0), h[p], _WKV_STATE_PASSES)
              for p, j in zip(pairs, js)]
        h = [my[p][:pw] + n[j] for p, j in zip(pairs, js)]
        y_rows.append(jnp.concatenate([my[p][pw:] + y10[j][:, pw:] for p, j in zip(pairs, js)], axis=1))
    for p in pairs:
        h_ref[p] = h[p]
    y_ref[...] = jnp.concatenate(y_rows, axis=0)

    @pl.when(i == pl.num_programs(1) - 1)
    def _():
        for p in pairs:
            s_ref[0, p] = h[p]


def _wkv_prompt(r, k, v, lw, kk, b, nb_seq, seq):
    c = WKV_CHUNK * WKV_CHUNKS_PER_TILE
    assert WKV_CHUNK == HEAD_DIM and seq % c == 0
    nc = seq // c
    row = lambda bb, i: (bb * nc + i, 0)
    return pl.pallas_call(
        _wkv_prompt_kernel,
        grid=(nb_seq, nc),
        in_specs=[pl.BlockSpec((c, D_MODEL), row)] * 6,
        out_specs=[pl.BlockSpec((c, D_MODEL), row),
                   pl.BlockSpec((1, _N_PAIRS, _PAIR_W, _PAIR_W), lambda bb, i: (bb, 0, 0, 0))],
        out_shape=[jax.ShapeDtypeStruct((nb_seq * seq, D_MODEL), F32),
                   jax.ShapeDtypeStruct((nb_seq, _N_PAIRS, _PAIR_W, _PAIR_W), F32)],
        scratch_shapes=[pltpu.VMEM((_N_PAIRS, _PAIR_W, _PAIR_W), F32)],
        compiler_params=_cparams("parallel", "arbitrary"),
        name="wkv_prompt",
    )(r, k, v, lw, kk, b)


def _unpair_state(hb):
    n = hb.shape[0]
    h6 = hb.reshape(n, _N_PAIRS, 2, HEAD_DIM, 2, HEAD_DIM)
    blocks = jnp.stack([h6[:, :, 0, :, 0, :], h6[:, :, 1, :, 1, :]], axis=2)
    return jnp.swapaxes(blocks, -1, -2).reshape(n, N_HEADS, HEAD_DIM, HEAD_DIM)


_WKV_VROWS = 4


def _wkv_sample_kernel(r_ref, lw_ref, k_ref, v_ref, kk_ref, b_ref, s0_ref, y_ref, s_ref):
    t = r_ref.shape[1]
    vr = _WKV_VROWS

    def body(vi, carry):
        v0 = pl.multiple_of(vi * vr, vr)
        s = s0_ref[0, pl.ds(v0, vr)]
        for st in range(t):
            sa = -jnp.sum(s * kk_ref[0, st][None], axis=1, keepdims=True)
            vv = v_ref[0, st, pl.ds(v0, vr)]
            s = s * jnp.exp(lw_ref[0, st])[None] + sa * b_ref[0, st][None] + vv * k_ref[0, st][None]
            y_ref[0, st, pl.ds(v0, vr)] = jnp.sum(s * r_ref[0, st][None], axis=1, keepdims=True)
        s_ref[0, pl.ds(v0, vr)] = s
        return carry

    lax.fori_loop(0, HEAD_DIM // vr, body, 0)


def _wkv_sample(r, lw, k, v, kk, b, s0):
    nh, t, hd, n = r.shape
    vec = pl.BlockSpec((1, t, hd, n), lambda h: (h, 0, 0, 0))
    vec1 = pl.BlockSpec((1, t, hd, 1, n), lambda h: (h, 0, 0, 0, 0))
    st = pl.BlockSpec((1, hd, hd, n), lambda h: (h, 0, 0, 0))
    return pl.pallas_call(
        _wkv_sample_kernel,
        grid=(nh,),
        in_specs=[vec, vec, vec, vec1, vec, vec, st],
        out_specs=[vec1, st],
        out_shape=[jax.ShapeDtypeStruct((nh, t, hd, 1, n), F32),
                   jax.ShapeDtypeStruct((nh, hd, hd, n), F32)],
        compiler_params=_cparams("parallel"),
        name="wkv_sample",
    )(r, lw, k, v, kk, b, s0)


def _mix_out_kernel(x_ref, oa_ref, ob_ref, y_ref, r_ref, k_ref, v_ref, g_ref, wgate_ref,
                    lnw_ref, lnb_ref, rk_ref, wout_ref, g1_ref, b1_ref, o_ref, *, alpha):
    d = D_MODEL
    x = x_ref[...]
    gate = jnp.dot(x.astype(BF16), wgate_ref[...], preferred_element_type=F32)
    e, et = _head_indicator()
    y = y_ref[...]
    mu = _head_sum(y, e, et) * (1.0 / HEAD_DIM)
    yc = y - mu
    var = _head_sum(yc * yc, e, et) * (1.0 / HEAD_DIM)
    yn = yc * lax.rsqrt(var + GN_EPS) * lnw_ref[...] + lnb_ref[...]
    v = v_ref[...]
    bonus = _head_sum(r_ref[...] * k_ref[...] * rk_ref[...], e, et) * v
    oc = (yn + bonus) * g_ref[...]
    mixed = (_sigmoid(gate[:, 0:d]) * oa_ref[...] + _sigmoid(gate[:, d:2 * d]) * ob_ref[...]
             + _sigmoid(gate[:, 2 * d:3 * d]) * oc)
    z = alpha * x + jnp.dot(mixed.astype(BF16), wout_ref[...], preferred_element_type=F32)
    o_ref[...] = _layer_norm(z, g1_ref[...], b1_ref[...])


def _mix_out(x, oa, ob, y, r, k, v, g, wgate, lnw, lnb, rk, wout, g1, b1, alpha):
    m = x.shape[0]
    tm = _tile(m, 256)
    row = lambda i: (i, 0)
    full = lambda i: (0, 0)
    rows = pl.BlockSpec((tm, D_MODEL), row)
    vec = pl.BlockSpec((1, D_MODEL), full)
    resident = lambda w: pl.BlockSpec(w.shape, full, pipeline_mode=pl.Buffered(1))
    return pl.pallas_call(
        functools.partial(_mix_out_kernel, alpha=alpha),
        grid=(m // tm,),
        in_specs=[rows] * 8 + [resident(wgate), vec, vec, vec, resident(wout), vec, vec],
        out_specs=rows,
        out_shape=jax.ShapeDtypeStruct((m, D_MODEL), F32),
        compiler_params=_cparams("parallel"),
        name="mix_out",
    )(x, oa, ob, y, r, k, v, g, wgate, lnw, lnb, rk, wout, g1, b1)


def _ffn_kernel(x_ref, wg_ref, wu_ref, wd_ref, g2_ref, b2_ref, o_ref, *, alpha):
    x = x_ref[...]
    xb = x.astype(BF16)
    gt = jnp.dot(xb, wg_ref[...], preferred_element_type=F32)
    up = jnp.dot(xb, wu_ref[...], preferred_element_type=F32)
    ff = jnp.dot((gt * _sigmoid(gt) * up).astype(BF16), wd_ref[...], preferred_element_type=F32)
    o_ref[...] = _layer_norm(alpha * x + ff, g2_ref[...], b2_ref[...])


def _ffn(x, wg, wu, wd, g2, b2, alpha):
    m = x.shape[0]
    tm = _tile(m, 256)
    row = lambda i: (i, 0)
    full = lambda i: (0, 0)
    resident = lambda w: pl.BlockSpec(w.shape, full, pipeline_mode=pl.Buffered(1))
    return pl.pallas_call(
        functools.partial(_ffn_kernel, alpha=alpha),
        grid=(m // tm,),
        in_specs=[pl.BlockSpec((tm, D_MODEL), row), resident(wg), resident(wu), resident(wd),
                  pl.BlockSpec((1, D_MODEL), full), pl.BlockSpec((1, D_MODEL), full)],
        out_specs=pl.BlockSpec((tm, D_MODEL), row),
        out_shape=jax.ShapeDtypeStruct((m, D_MODEL), F32),
        compiler_params=_cparams("parallel"),
        name="ffn",
    )(x, wg, wu, wd, g2, b2)


def _blockdiag_pairs(w):
    nb, bs, _ = w.shape
    w = w.reshape(nb // 2, 2, bs, bs)
    z = jnp.zeros((nb // 2, bs, bs), w.dtype)
    top = jnp.concatenate([w[:, 0], z], axis=2)
    bot = jnp.concatenate([z, w[:, 1]], axis=2)
    return jnp.concatenate([top, bot], axis=1)


def _row(p):
    return p.reshape(1, -1)


def kernel(x_prompt, x_sample, cache_k, cache_v, state_conv, state_rglru, state_shift, state_wkv, w_in, attn_sinks, conv_w, conv_b, rg_wa, rg_ba, rg_wx, rg_bx, rg_lambda, rw_mu, rw_w0, rw_wup, rw_a0, rw_aup, rw_gup, rw_kk, rw_ka, rw_rk, rw_lnw, rw_lnb, w_out, ln1_g, ln1_b, w_gu, w_down, ln2_g, ln2_b):
    depth = w_in.shape[0]
    bp, seq, d = x_prompt.shape
    ns, ts, _ = x_sample.shape
    wb = cache_k.shape[2]
    alpha = (2 * depth) ** 0.25
    dff = w_down.shape[1]
    pad = SHIFT_PAD - SHIFT_W

    xp = x_prompt.reshape(bp * seq, d)
    xs = jnp.swapaxes(x_sample, 0, 1).reshape(ts * ns, d)
    st_p = [[] for _ in range(6)]
    st_s = [[] for _ in range(6)]

    for l in range(depth):
        wl = w_in[l]
        w_all = wl[:, :OFF_RW].astype(BF16)
        w_rw = jnp.pad(wl[:, OFF_RW:OFF_GATE], ((0, 0), (0, pad))).astype(BF16)
        w_gate = wl[:, OFF_GATE:].astype(BF16)
        wa2 = _blockdiag_pairs(rg_wa[l]).astype(BF16)
        wx2 = _blockdiag_pairs(rg_wx[l]).astype(BF16)
        rg_args = (conv_w[l], _row(conv_b[l]), wa2, _row(rg_ba[l]), wx2, _row(rg_bx[l]), _row(rg_lambda[l]))
        zw = jnp.zeros((LORA_W, d), F32)
        wwa = jnp.concatenate([jnp.concatenate([rw_wup[l], zw], axis=1),
                               jnp.concatenate([zw, rw_aup[l]], axis=1)], axis=0).astype(BF16)
        gup = jnp.pad(rw_gup[l], ((0, 2 * LANES - LORA_G), (0, 0))).astype(BF16)
        prep_params = (_row(jnp.pad(rw_mu[l], (0, pad))), _row(rw_w0[l]), _row(rw_a0[l]), _row(rw_kk[l]),
                       _row(rw_ka[l]), wwa, gup)
        mix_params = (_row(rw_lnw[l]), _row(rw_lnb[l]), _row(rw_rk[l]), w_out[l].astype(BF16),
                      _row(ln1_g[l]), _row(ln1_b[l]))
        wg = w_gu[l][:, :dff].astype(BF16)
        wu = w_gu[l][:, dff:].astype(BF16)
        wd = w_down[l].astype(BF16)
        ffn_params = (wg, wu, wd, _row(ln2_g[l]), _row(ln2_b[l]))
        sinks = attn_sinks[l]

        qkv, rnn = _in_proj(xp, w_all)
        oa = _attn_prompt(qkv, sinks, bp, seq)
        ob = _rglru_prompt(rnn, bp, seq, *rg_args)
        (r, k, v, lw, kk, b, g), sh_p = _rwkv_front(xp, w_rw, jnp.zeros((bp, 8, SHIFT_PAD), F32), bp, 1,
                                                    prep_params)
        y, hfin = _wkv_prompt(r, k, v, lw, kk, b, bp, seq)
        x1 = _mix_out(xp, oa, ob, y, r, k, v, g, w_gate, *mix_params, alpha)
        xp = _ffn(x1, *ffn_params, alpha)
        qkv3 = qkv.reshape(bp, seq, OFF_RNN)
        st_p[0].append(qkv3[:, seq - wb:, OFF_K:OFF_V].reshape(bp, wb, N_KV_HEADS, HEAD_DIM))
        st_p[1].append(qkv3[:, seq - wb:, OFF_V:OFF_RNN].reshape(bp, wb, N_KV_HEADS, HEAD_DIM))
        st_p[2].append(rnn.reshape(bp, seq, d)[:, seq - (CONV_W - 1):])
        st_p[3].append(ob.reshape(bp, seq, d)[:, seq - 1])
        st_p[4].append(sh_p[:, 7, :SHIFT_W])
        st_p[5].append(_unpair_state(hfin))

        qkv, rnn = _in_proj(xs, w_all)
        qkv_n = jnp.swapaxes(qkv.reshape(ts, ns, OFF_RNN), 0, 1)
        oa_n, new_k, new_v = _attn_sample(qkv_n[:, :, :OFF_K], qkv_n[:, :, OFF_K:OFF_V], qkv_n[:, :, OFF_V:],
                                          cache_k[l].reshape(ns, wb, KV_W), cache_v[l].reshape(ns, wb, KV_W),
                                          sinks)
        oa = jnp.swapaxes(oa_n, 0, 1).reshape(ts * ns, d)
        rnn3 = rnn.reshape(ts, ns, d)
        ob3 = _rglru_sample(rnn3, jnp.swapaxes(state_conv[l], 0, 1), state_rglru[l], *rg_args)
        ob = ob3.reshape(ts * ns, d)
        shift = jnp.pad(state_shift[l], ((0, 0), (0, pad)))[None]
        (r, k, v, lw, kk, b, g), sh_s = _rwkv_front(xs, w_rw, shift, 1, ns, prep_params)

        def lanes(z):
            return jnp.transpose(z.reshape(ts, ns, N_HEADS, HEAD_DIM), (2, 0, 3, 1))

        s0 = jnp.transpose(state_wkv[l], (1, 2, 3, 0))
        y5, s_new = _wkv_sample(lanes(r), lanes(lw), lanes(k),
                                lanes(v).reshape(N_HEADS, ts, HEAD_DIM, 1, ns), lanes(kk), lanes(b), s0)
        y = jnp.transpose(y5.reshape(N_HEADS, ts, HEAD_DIM, ns), (1, 3, 0, 2)).reshape(ts * ns, d)
        x1 = _mix_out(xs, oa, ob, y, r, k, v, g, w_gate, *mix_params, alpha)
        xs = _ffn(x1, *ffn_params, alpha)
        st_s[0].append(new_k.reshape(ns, wb, N_KV_HEADS, HEAD_DIM))
        st_s[1].append(new_v.reshape(ns, wb, N_KV_HEADS, HEAD_DIM))
        st_s[2].append(jnp.swapaxes(rnn3[ts - (CONV_W - 1):], 0, 1))
        st_s[3].append(ob3[ts - 1])
        st_s[4].append(sh_s[0, :, :SHIFT_W])
        st_s[5].append(jnp.transpose(s_new, (3, 0, 1, 2)))

    yp = xp.reshape(bp, seq, d)
    ys = jnp.swapaxes(xs.reshape(ts, ns, d), 0, 1)
    return (yp, ys, *(jnp.stack(s) for s in st_p), *(jnp.stack(s) for s in st_s))
```

```python
import functools
import math

import jax
import jax.numpy as jnp
from jax import lax
from jax.experimental import pallas as pl
from jax.experimental.pallas import tpu as pltpu

F32 = jnp.float32
BF16 = jnp.bfloat16

D_MODEL = 1024
HEAD_DIM = 64
N_HEADS = 16
N_KV_HEADS = 4
KV_W = N_KV_HEADS * HEAD_DIM
WINDOW = 128
CONV_W = 4
RG_C = 8.0
LORA_W = 64
LORA_A = 64
LORA_G = 160
SHIFT_W = 3 * D_MODEL + LORA_W + LORA_A + LORA_G
LANES = 128
SHIFT_PAD = -(-SHIFT_W // LANES) * LANES
GN_EPS = 64e-5
LN_EPS = 1e-5
OFF_K = D_MODEL
OFF_V = OFF_K + KV_W
OFF_RNN = OFF_V + KV_W
OFF_RW = OFF_RNN + D_MODEL
OFF_GATE = OFF_RW + SHIFT_W
WKV_CHUNK = 64
WKV_CHUNKS_PER_TILE = 4
VMEM_LIMIT = 48 * 1024 * 1024


def _cparams(*sem):
    return pltpu.CompilerParams(dimension_semantics=sem, vmem_limit_bytes=VMEM_LIMIT)


def _tile(n, pref):
    if n <= pref:
        return n
    for t in range(pref, 7, -1):
        if n % t == 0 and t % 8 == 0:
            return t
    return n


def _dot(a, b):
    return jnp.dot(a.astype(BF16), b.astype(BF16), preferred_element_type=F32)


def _split_terms(x, n):
    terms = []
    rem = x
    for _ in range(n):
        hi = rem.astype(BF16)
        terms.append(hi)
        rem = rem - hi.astype(F32)
    return terms


def _dot_lhs_split(x, w_bf16, n):
    acc = None
    for t in _split_terms(x, n):
        p = jnp.dot(t, w_bf16, preferred_element_type=F32)
        acc = p if acc is None else acc + p
    return acc


def _dot_rhs_split(w_bf16, x, n):
    acc = None
    for t in _split_terms(x, n):
        p = jnp.dot(w_bf16, t, preferred_element_type=F32)
        acc = p if acc is None else acc + p
    return acc


def _sigmoid(x):
    return 0.5 * jnp.tanh(0.5 * x) + 0.5


def _softplus(x):
    return jnp.maximum(x, 0.0) + jnp.log(1.0 + jnp.exp(-jnp.abs(x)))


def _layer_norm(z, g, b):
    mu = jnp.mean(z, axis=-1, keepdims=True)
    zc = z - mu
    var = jnp.mean(zc * zc, axis=-1, keepdims=True)
    return zc * lax.rsqrt(var + LN_EPS) * g + b


def _head_indicator():
    c = lax.broadcasted_iota(jnp.int32, (D_MODEL, LANES), 0) // HEAD_DIM
    h = lax.broadcasted_iota(jnp.int32, (D_MODEL, LANES), 1)
    e = jnp.where(c == h, 1.0, 0.0).astype(BF16)
    ct = lax.broadcasted_iota(jnp.int32, (LANES, D_MODEL), 1) // HEAD_DIM
    ht = lax.broadcasted_iota(jnp.int32, (LANES, D_MODEL), 0)
    et = jnp.where(ct == ht, 1.0, 0.0).astype(BF16)
    return e, et


def _head_sum(x, e, et):
    s = jnp.dot(x.astype(BF16), e, preferred_element_type=F32)
    return _dot_lhs_split(s, et, 2)


_PROJ_WIDTHS = (OFF_RNN, D_MODEL)


def _in_proj_kernel(x_ref, w_ref, *o_refs):
    xb = x_ref[...].astype(BF16)
    off = 0
    for o_ref in o_refs:
        n = o_ref.shape[1]
        o_ref[...] = jnp.dot(xb, w_ref[:, off:off + n], preferred_element_type=F32)
        off += n


def _in_proj(x, w):
    m, k = x.shape
    tm = _tile(m, 256)
    return pl.pallas_call(
        _in_proj_kernel,
        grid=(m // tm,),
        in_specs=[pl.BlockSpec((tm, k), lambda i: (i, 0)),
                  pl.BlockSpec(w.shape, lambda i: (0, 0), pipeline_mode=pl.Buffered(1))],
        out_specs=[pl.BlockSpec((tm, n), lambda i: (i, 0)) for n in _PROJ_WIDTHS],
        out_shape=[jax.ShapeDtypeStruct((m, n), F32) for n in _PROJ_WIDTHS],
        compiler_params=_cparams("parallel"),
        name="in_proj",
    )(x, w)


def _alibi_slope(h):
    return 2.0 ** (-8.0 * (h + 1) / N_HEADS)


def _attn_prompt_bias():
    w = WINDOW
    rel = w + jnp.arange(w)[:, None] - jnp.arange(2 * w)[None, :]
    band = (rel >= 0) & (rel < WINDOW)
    has_prev = jnp.arange(2)[:, None, None] > 0
    valid = band[None] & (has_prev | (jnp.arange(2 * w) >= w)[None, None, :])
    slopes = jnp.array([_alibi_slope(h) for h in range(N_HEADS)], F32)
    bias = -slopes[None, :, None, None] * rel.astype(F32)[None, None]
    return jnp.where(valid[:, None], bias, -jnp.inf)


def _attn_prompt_kernel(sink_ref, bias_ref, q_ref, kc_ref, vc_ref, kp_ref, vp_ref, o_ref):
    q = q_ref[...] * (HEAD_DIM ** -0.5)
    kcat = jnp.concatenate([kp_ref[...], kc_ref[...]], axis=0).astype(BF16)
    vcat = jnp.concatenate([vp_ref[...], vc_ref[...]], axis=0).astype(BF16)
    heads = range(N_HEADS)
    gsz = N_HEADS // N_KV_HEADS
    kg = [kcat[:, g * HEAD_DIM:(g + 1) * HEAD_DIM] for g in range(N_KV_HEADS)]
    vg = [vcat[:, g * HEAD_DIM:(g + 1) * HEAD_DIM] for g in range(N_KV_HEADS)]
    qh = [q[:, h * HEAD_DIM:(h + 1) * HEAD_DIM].astype(BF16) for h in heads]
    s = [lax.dot_general(qh[h], kg[h // gsz], (((1,), (1,)), ((), ())), preferred_element_type=F32)
         + bias_ref[0, h] for h in heads]
    m = [jnp.maximum(jnp.max(s[h], axis=-1, keepdims=True), sink_ref[h]) for h in heads]
    e = [jnp.exp(s[h] - m[h]) for h in heads]
    den = [jnp.sum(e[h], axis=-1, keepdims=True) + jnp.exp(sink_ref[h] - m[h]) for h in heads]
    p = [(e[h] / den[h]).astype(BF16) for h in heads]
    outs = [jnp.dot(p[h], vg[h // gsz], preferred_element_type=F32) for h in heads]
    o_ref[...] = jnp.concatenate(outs, axis=1)


def _attn_prompt(qkv, sinks, nb_seq, seq):
    w = WINDOW
    nb = seq // w
    kcol = OFF_K // KV_W
    vcol = OFF_V // KV_W

    def cur(c):
        return lambda b, j: (b * nb + j, c)

    def prev(c):
        return lambda b, j: (jnp.maximum(b * nb + j - 1, 0), c)

    return pl.pallas_call(
        _attn_prompt_kernel,
        grid=(nb_seq, nb),
        in_specs=[pl.BlockSpec(memory_space=pltpu.SMEM),
                  pl.BlockSpec((1, N_HEADS, w, 2 * w), lambda b, j: (jnp.minimum(j, 1), 0, 0, 0)),
                  pl.BlockSpec((w, D_MODEL), cur(0)),
                  pl.BlockSpec((w, KV_W), cur(kcol)),
                  pl.BlockSpec((w, KV_W), cur(vcol)),
                  pl.BlockSpec((w, KV_W), prev(kcol)),
                  pl.BlockSpec((w, KV_W), prev(vcol))],
        out_specs=pl.BlockSpec((w, D_MODEL), cur(0)),
        out_shape=jax.ShapeDtypeStruct((nb_seq * seq, D_MODEL), F32),
        compiler_params=_cparams("parallel", "parallel"),
        name="attn_prompt",
    )(sinks, _attn_prompt_bias(), qkv, qkv, qkv, qkv, qkv)


def _attn_sample_kernel(sink_ref, q_ref, kn_ref, vn_ref, ck_ref, cv_ref, o_ref, nk_ref, nv_ref, *, t):
    sn = q_ref.shape[0]
    wb = ck_ref.shape[1]
    gsz = N_HEADS // N_KV_HEADS
    rows = gsz * t
    q = q_ref[...]
    kn = kn_ref[...]
    vn = vn_ref[...]
    ck = ck_ref[...]
    cv = cv_ref[...]
    tq1 = lax.broadcasted_iota(jnp.int32, (rows, wb), 0) % t
    j1 = lax.broadcasted_iota(jnp.int32, (rows, wb), 1)
    rel1 = wb + tq1 - j1
    valid1 = (rel1 >= 0) & (rel1 < WINDOW)
    tq2 = lax.broadcasted_iota(jnp.int32, (rows, t), 0) % t
    s2i = lax.broadcasted_iota(jnp.int32, (rows, t), 1)
    rel2 = tq2 - s2i
    valid2 = (rel2 >= 0) & (rel2 < WINDOW)
    hrow = lax.broadcasted_iota(jnp.int32, (rows, 1), 0) // t
    outs = []
    for g in range(N_KV_HEADS):
        slope = jnp.zeros((rows, 1), F32)
        sink = jnp.zeros((rows, 1), F32)
        for hh in range(gsz):
            h = g * gsz + hh
            slope = jnp.where(hrow == hh, _alibi_slope(h), slope)
            sink = jnp.where(hrow == hh, sink_ref[h], sink)
        qg = jnp.concatenate([q[:, :, (g * gsz + hh) * HEAD_DIM:(g * gsz + hh + 1) * HEAD_DIM]
                              for hh in range(gsz)], axis=1).astype(BF16)
        lo, hi = g * HEAD_DIM, (g + 1) * HEAD_DIM
        ckg = ck[:, :, lo:hi].astype(BF16)
        cvg = cv[:, :, lo:hi].astype(BF16)
        kng = kn[:, :, lo:hi].astype(BF16)
        vng = vn[:, :, lo:hi].astype(BF16)
        nt = (((2,), (2,)), ((0,), (0,)))
        nn = (((2,), (1,)), ((0,), (0,)))
        s1 = lax.dot_general(qg, ckg, nt, preferred_element_type=F32) * (HEAD_DIM ** -0.5)
        s2 = lax.dot_general(qg, kng, nt, preferred_element_type=F32) * (HEAD_DIM ** -0.5)
        s1 = jnp.where(valid1[None], s1 - (slope * rel1.astype(F32))[None], -jnp.inf)
        s2 = jnp.where(valid2[None], s2 - (slope * rel2.astype(F32))[None], -jnp.inf)
        m = jnp.maximum(jnp.maximum(jnp.max(s1, axis=-1, keepdims=True),
                                    jnp.max(s2, axis=-1, keepdims=True)), sink[None])
        e1 = jnp.exp(s1 - m)
        e2 = jnp.exp(s2 - m)
        den = (jnp.sum(e1, axis=-1, keepdims=True) + jnp.sum(e2, axis=-1, keepdims=True)
               + jnp.exp(sink[None] - m))
        p1 = (e1 / den).astype(BF16)
        p2 = (e2 / den).astype(BF16)
        og = (lax.dot_general(p1, cvg, nn, preferred_element_type=F32)
              + lax.dot_general(p2, vng, nn, preferred_element_type=F32))
        outs.extend(og[:, hh * t:(hh + 1) * t, :] for hh in range(gsz))
    o_ref[...] = jnp.concatenate(outs, axis=2)
    nk_ref[...] = jnp.concatenate([ck[:, t:, :], kn], axis=1)
    nv_ref[...] = jnp.concatenate([cv[:, t:, :], vn], axis=1)


def _attn_sample(q, kn, vn, ck, cv, sinks):
    n, t, _ = q.shape
    wb = ck.shape[1]
    sn = _tile(n, 16)
    blk = lambda d1, d2: pl.BlockSpec((sn, d1, d2), lambda i: (i, 0, 0))
    return pl.pallas_call(
        functools.partial(_attn_sample_kernel, t=t),
        grid=(n // sn,),
        in_specs=[pl.BlockSpec(memory_space=pltpu.SMEM),
                  blk(t, D_MODEL), blk(t, KV_W), blk(t, KV_W), blk(wb, KV_W), blk(wb, KV_W)],
        out_specs=[blk(t, D_MODEL), blk(wb, KV_W), blk(wb, KV_W)],
        out_shape=[jax.ShapeDtypeStruct((n, t, D_MODEL), F32),
                   jax.ShapeDtypeStruct((n, wb, KV_W), F32),
                   jax.ShapeDtypeStruct((n, wb, KV_W), F32)],
        compiler_params=_cparams("parallel"),
        name="attn_sample",
    )(sinks, q, kn, vn, ck, cv)


def _rglru_gates(xc, wa_ref, ba, wx_ref, bx, lam):
    xb = xc.astype(BF16)
    npair = D_MODEL // LANES
    ga = jnp.concatenate([jnp.dot(xb[:, j * LANES:(j + 1) * LANES], wa_ref[j], preferred_element_type=F32)
                          for j in range(npair)], axis=1)
    gx = jnp.concatenate([jnp.dot(xb[:, j * LANES:(j + 1) * LANES], wx_ref[j], preferred_element_type=F32)
                          for j in range(npair)], axis=1)
    r = _sigmoid(ga + ba)
    ig = _sigmoid(gx + bx)
    log_a = -RG_C * r * _softplus(-lam)
    a = jnp.exp(log_a)
    u = jnp.sqrt(-jnp.tanh(log_a) * (a * a + 1.0)) * (ig * xc)
    return a, u


def _proj_rglru_prompt_kernel(xin_ref, w_ref, cw_ref, cb_ref, wa_ref, ba_ref, wx_ref, bx_ref, lam_ref,
                              qkv_ref, h_ref, tail_out_ref, tail_ref, hc_ref):
    i = pl.program_id(1)
    tc = xin_ref.shape[0]

    @pl.when(i == 0)
    def _():
        tail_ref[...] = jnp.zeros_like(tail_ref)
        hc_ref[...] = jnp.zeros_like(hc_ref)

    xb = xin_ref[...].astype(BF16)
    qkv_ref[...] = jnp.dot(xb, w_ref[:, :OFF_RNN], preferred_element_type=F32)
    x = jnp.dot(xb, w_ref[:, OFF_RNN:], preferred_element_type=F32)
    xp = jnp.concatenate([tail_ref[...], x], axis=0)
    cw = cw_ref[...]
    xc = cb_ref[...] + x * cw[CONV_W - 1:CONV_W]
    for d in range(1, CONV_W):
        xc = xc + xp[8 - d:8 - d + tc] * cw[CONV_W - 1 - d:CONV_W - d]
    a, u = _rglru_gates(xc, wa_ref, ba_ref[...], wx_ref, bx_ref[...], lam_ref[...])
    ng = tc // 8
    acc_a = a.reshape(ng, 8, D_MODEL)
    acc_h = u.reshape(ng, 8, D_MODEL)
    sub = lax.broadcasted_iota(jnp.int32, (1, 8, D_MODEL), 1)
    for d in (1, 2, 4):
        keep = sub >= d
        sh_a = jnp.where(keep, pltpu.roll(acc_a, d, axis=1), 1.0)
        sh_h = jnp.where(keep, pltpu.roll(acc_h, d, axis=1), 0.0)
        acc_h = acc_h + acc_a * sh_h
        acc_a = acc_a * sh_a
    carry = hc_ref[...]
    for g in range(ng):
        hg = acc_h[g] + acc_a[g] * carry
        h_ref[g * 8:(g + 1) * 8, :] = hg
        carry = hg[7:8]
    hc_ref[...] = carry
    tail_ref[...] = x[tc - 8:tc]

    @pl.when(i == pl.num_programs(1) - 1)
    def _():
        tail_out_ref[0] = x[tc - 8:tc]


def _proj_rglru_prompt(x, w, nb_seq, seq, cw, cb, wa2, ba, wx2, bx, lam):
    tc = _tile(seq, 256)
    nt = seq // tc
    row = lambda b, i: (b * nt + i, 0)
    full2 = lambda b, i: (0, 0)
    full3 = lambda b, i: (0, 0, 0)
    return pl.pallas_call(
        _proj_rglru_prompt_kernel,
        grid=(nb_seq, nt),
        in_specs=[pl.BlockSpec((tc, D_MODEL), row),
                  pl.BlockSpec(w.shape, full2, pipeline_mode=pl.Buffered(1)),
                  pl.BlockSpec((CONV_W, D_MODEL), full2), pl.BlockSpec((1, D_MODEL), full2),
                  pl.BlockSpec(wa2.shape, full3), pl.BlockSpec((1, D_MODEL), full2),
                  pl.BlockSpec(wx2.shape, full3), pl.BlockSpec((1, D_MODEL), full2),
                  pl.BlockSpec((1, D_MODEL), full2)],
        out_specs=[pl.BlockSpec((tc, OFF_RNN), row), pl.BlockSpec((tc, D_MODEL), row),
                   pl.BlockSpec((1, 8, D_MODEL), lambda b, i: (b, 0, 0))],
        out_shape=[jax.ShapeDtypeStruct((nb_seq * seq, OFF_RNN), F32),
                   jax.ShapeDtypeStruct((nb_seq * seq, D_MODEL), F32),
                   jax.ShapeDtypeStruct((nb_seq, 8, D_MODEL), F32)],
        scratch_shapes=[pltpu.VMEM((8, D_MODEL), F32), pltpu.VMEM((1, D_MODEL), F32)],
        compiler_params=_cparams("parallel", "arbitrary"),
        name="proj_rglru_prompt",
    )(x, w, cw, cb, wa2, ba, wx2, bx, lam)


def _rglru_sample_kernel(x_ref, cs_ref, h0_ref, cw_ref, cb_ref, wa_ref, ba_ref, wx_ref, bx_ref, lam_ref,
                         h_ref):
    t, sn, _ = x_ref.shape
    cw = cw_ref[...]
    nst = CONV_W - 1

    def slab(s):
        return x_ref[s] if s >= 0 else cs_ref[nst + s]

    xcs = []
    for s in range(t):
        xc = cb_ref[...] + slab(s) * cw[CONV_W - 1:CONV_W]
        for d in range(1, CONV_W):
            xc = xc + slab(s - d) * cw[CONV_W - 1 - d:CONV_W - d]
        xcs.append(xc)
    xc_all = jnp.concatenate(xcs, axis=0)
    a, u = _rglru_gates(xc_all, wa_ref, ba_ref[...], wx_ref, bx_ref[...], lam_ref[...])
    h = h0_ref[...]
    for s in range(t):
        h = a[s * sn:(s + 1) * sn] * h + u[s * sn:(s + 1) * sn]
        h_ref[s] = h


def _rglru_sample(x, cs, h0, cw, cb, wa2, ba, wx2, bx, lam):
    t, n, _ = x.shape
    sn = _tile(n, 32)
    full2 = lambda i: (0, 0)
    full3 = lambda i: (0, 0, 0)
    return pl.pallas_call(
        _rglru_sample_kernel,
        grid=(n // sn,),
        in_specs=[pl.BlockSpec((t, sn, D_MODEL), lambda i: (0, i, 0)),
                  pl.BlockSpec((CONV_W - 1, sn, D_MODEL), lambda i: (0, i, 0)),
                  pl.BlockSpec((sn, D_MODEL), lambda i: (i, 0)),
                  pl.BlockSpec((CONV_W, D_MODEL), full2), pl.BlockSpec((1, D_MODEL), full2),
                  pl.BlockSpec(wa2.shape, full3), pl.BlockSpec((1, D_MODEL), full2),
                  pl.BlockSpec(wx2.shape, full3), pl.BlockSpec((1, D_MODEL), full2),
                  pl.BlockSpec((1, D_MODEL), full2)],
        out_specs=pl.BlockSpec((t, sn, D_MODEL), lambda i: (0, i, 0)),
        out_shape=jax.ShapeDtypeStruct((t, n, D_MODEL), F32),
        compiler_params=_cparams("parallel"),
        name="rglru_sample",
    )(x, cs, h0, cw, cb, wa2, ba, wx2, bx, lam)


def _rwkv_prep_math(pc, prev, mu, w0, a0, kkp, ka, wwa, gup, e, et):
    d = D_MODEL
    ps = pc + mu * (prev - pc)
    r = ps[:, 0:d]
    k = ps[:, d:2 * d]
    v = ps[:, 2 * d:3 * d]
    l01 = ps[:, 3 * d:3 * d + LANES]
    lane = lax.broadcasted_iota(jnp.int32, (1, LANES), 1)
    t01 = jnp.where(lane < LORA_W, jnp.tanh(l01), l01)
    wa = _dot(t01, wwa)
    g = _dot(_sigmoid(ps[:, 3 * d + LANES:]), gup)
    wlog = -_softplus(-(w0 + wa[:, :d])) - 0.5
    logw = -jnp.exp(wlog)
    a = _sigmoid(a0 + wa[:, d:])
    kk = k * kkp
    nrm = jnp.maximum(jnp.sqrt(_head_sum(kk * kk, e, et)), 1e-12)
    kkn = kk / nrm
    kmod = k * (1.0 + (a - 1.0) * ka)
    return r, kmod, v, logw, kkn, kkn * a, g


_N_PREP_OUT = 7


def _rwkv_front_kernel(x_ref, w_ref, sh0_ref, mu_ref, w0_ref, a0_ref, kkp_ref, ka_ref, wwa_ref, gup_ref,
                       e_ref, et_ref, *refs, shift_rows):
    out_refs, sh_out_ref, carry_ref = refs[:_N_PREP_OUT], refs[_N_PREP_OUT], refs[_N_PREP_OUT + 1]
    i = pl.program_id(1)
    tm = x_ref.shape[0]
    ncar = carry_ref.shape[0]

    @pl.when(i == 0)
    def _():
        carry_ref[...] = sh0_ref[0]

    pc = jnp.dot(x_ref[...].astype(BF16), w_ref[...], preferred_element_type=F32)
    prev = jnp.concatenate([carry_ref[ncar - shift_rows:, :], pc[:tm - shift_rows]], axis=0)
    outs = _rwkv_prep_math(pc, prev, mu_ref[...], w0_ref[...], a0_ref[...], kkp_ref[...], ka_ref[...],
                           wwa_ref[...], gup_ref[...], e_ref[...], et_ref[...])
    for o_ref, o in zip(out_refs, outs):
        o_ref[...] = o
    carry_ref[...] = pc[tm - ncar:]

    @pl.when(i == pl.num_programs(1) - 1)
    def _():
        sh_out_ref[0] = pc[tm - ncar:]


def _rwkv_front(x, w_rw, shift0, n_groups, shift_rows, params):
    rows = x.shape[0] // n_groups
    ncar = shift0.shape[1]
    tm = _tile(rows, 256)
    assert tm >= ncar >= shift_rows and tm % shift_rows == 0
    nt = rows // tm
    row = lambda g, i: (g * nt + i, 0)
    full = lambda g, i: (0, 0)
    grp = pl.BlockSpec((1, ncar, SHIFT_PAD), lambda g, i: (g, 0, 0))
    res = pl.pallas_call(
        functools.partial(_rwkv_front_kernel, shift_rows=shift_rows),
        grid=(n_groups, nt),
        in_specs=[pl.BlockSpec((tm, D_MODEL), row),
                  pl.BlockSpec(w_rw.shape, full, pipeline_mode=pl.Buffered(1)), grp]
                 + [pl.BlockSpec(p.shape, full) for p in params],
        out_specs=[pl.BlockSpec((tm, D_MODEL), row)] * _N_PREP_OUT + [grp],
        out_shape=[jax.ShapeDtypeStruct((n_groups * rows, D_MODEL), F32)] * _N_PREP_OUT
                  + [jax.ShapeDtypeStruct((n_groups, ncar, SHIFT_PAD), F32)],
        scratch_shapes=[pltpu.VMEM((ncar, SHIFT_PAD), F32)],
        compiler_params=_cparams("parallel", "arbitrary"),
        name="rwkv_front",
    )(x, w_rw, shift0, *params)
    return res[:_N_PREP_OUT], res[_N_PREP_OUT]


_PAIR_W = 2 * HEAD_DIM
_N_PAIRS = D_MODEL // _PAIR_W
_WKV_STATE_PASSES = 1


def _dot_tn(a, b):
    return lax.dot_general(a.astype(BF16), b.astype(BF16), (((0,), (0,)), ((), ())),
                           preferred_element_type=F32)


def _dot_nt(a, b):
    return lax.dot_general(a.astype(BF16), b.astype(BF16), (((1,), (1,)), ((), ())),
                           preferred_element_type=F32)


def _dot_passes(a, b, passes):
    if passes == 1:
        return _dot(a, b)
    ah, al = _split_terms(a, 2)
    bh, bl = _split_terms(b, 2)
    return (jnp.dot(ah, bh, preferred_element_type=F32) + jnp.dot(ah, bl, preferred_element_type=F32)
            + jnp.dot(al, bh, preferred_element_type=F32))


def _wkv_prompt_kernel(r_ref, k_ref, v_ref, lw_ref, kk_ref, b_ref, y_ref, s_ref, h_ref):
    i = pl.program_id(1)
    rows = r_ref.shape[0]
    c = WKV_CHUNK
    nck = rows // c
    hd = HEAD_DIM
    pw = _PAIR_W

    @pl.when(i == 0)
    def _():
        h_ref[...] = jnp.zeros_like(h_ref)

    row_t = lax.broadcasted_iota(jnp.int32, (rows, rows), 0)
    col_t = lax.broadcasted_iota(jnp.int32, (rows, rows), 1)
    tri = (row_t >= col_t) & (row_t // c == col_t // c)
    lw = lw_ref[...]
    cum = _dot_rhs_split(jnp.where(tri, 1.0, 0.0).astype(BF16), lw, 3)
    last = jnp.concatenate([jnp.broadcast_to(cum[(ci + 1) * c - 1:(ci + 1) * c], (c, D_MODEL))
                            for ci in range(nck)], axis=0)
    g_inv = jnp.exp(-cum)
    g_end = jnp.exp(last - cum)
    kk = kk_ref[...]
    b = b_ref[...]
    k = k_ref[...]
    at_all = -(kk * jnp.exp(cum - lw))
    rt_all = r_ref[...] * jnp.exp(cum)
    bt_all = b * g_inv
    kt_all = k * g_inv
    bh_all = b * g_end
    kh_all = k * g_end
    gc_all = jnp.exp(last)

    rowp = lax.broadcasted_iota(jnp.int32, (c, pw), 0)
    colp = lax.broadcasted_iota(jnp.int32, (c, pw), 1) % hd
    strict = rowp > colp
    incl = rowp >= colp
    eye_p = jnp.where(rowp == colp, 1.0, 0.0)
    left = lax.broadcasted_iota(jnp.int32, (1, pw), 1) < hd
    r2 = lax.broadcasted_iota(jnp.int32, (pw, pw), 0)
    c2 = lax.broadcasted_iota(jnp.int32, (pw, pw), 1)
    same_head = (r2 < hd) == (c2 < hd)
    eye2 = r2 == c2

    def bd(w):
        return jnp.concatenate([jnp.where(left, w, 0.0), jnp.where(left, 0.0, w)], axis=0)

    pairs = range(_N_PAIRS)
    items = [(ci, p) for ci in range(nck) for p in pairs]
    idx = range(len(items))
    sub = [(slice(ci * c, (ci + 1) * c), slice(p * pw, (p + 1) * pw)) for ci, p in items]
    v_all = v_ref[...]
    at = [at_all[s] for s in sub]
    rt = [rt_all[s] for s in sub]
    v = [v_all[s] for s in sub]
    amat = [_dot_nt(jnp.concatenate([at[j], rt[j]], axis=0),
                    jnp.concatenate([bd(bt_all[sub[j]]), bd(kt_all[sub[j]])], axis=0))
            for j in idx]
    a_ab = [jnp.where(strict, a[:c, :pw], 0.0) for a in amat]
    a_ak = [jnp.where(strict, a[:c, pw:], 0.0) for a in amat]
    a_rb = [jnp.where(incl, a[c:, :pw], 0.0) for a in amat]
    a_rk = [jnp.where(incl, a[c:, pw:], 0.0) for a in amat]
    tinv = [eye_p + a for a in a_ab]
    npow = [_dot(a, bd(a)) for a in a_ab]
    span = 4
    while span < c:
        both = [_dot(jnp.concatenate([tinv[j], npow[j]], axis=0), bd(npow[j])) for j in idx]
        tinv = [tinv[j] + both[j][:c] for j in idx]
        npow = [x[c:] for x in both]
        span *= 2
    tinv = [tinv[j] + _dot(tinv[j], bd(npow[j])) for j in idx]
    w12 = [_dot(jnp.concatenate([a_ak[j], a_rk[j]], axis=0), bd(v[j])) for j in idx]
    pq = [_dot(tinv[j], jnp.concatenate([bd(at[j]), bd(w12[j][:c])], axis=1)) for j in idx]
    y10 = [jnp.concatenate([rt[j], w12[j][c:]], axis=1)
           + _dot(a_rb[j], jnp.concatenate([bd(pq[j][:, :pw]), bd(pq[j][:, pw:])], axis=1))
           for j in idx]
    mn = [_dot_tn(bh_all[sub[j]], pq[j]) for j in idx]
    kv = [_dot_tn(kh_all[sub[j]], v[j]) for j in idx]
    m = [jnp.where(same_head, mn[j][:, :pw], 0.0)
         + jnp.where(eye2, gc_all[sub[j][0], sub[j][1]][:1], 0.0) for j in idx]
    n = [jnp.where(same_head, mn[j][:, pw:] + kv[j], 0.0) for j in idx]
    h = [h_ref[p] for p in pairs]
    y_rows = []
    for ci in range(nck):
        js = [ci * _N_PAIRS + p for p in pairs]
        my = [_dot_passes(jnp.concatenate([m[j], y10[j][:, :pw]], axis=0), h[p], _WKV_STATE_PASSES)
              for p, j in zip(pairs, js)]
        h = [my[p][:pw] + n[j] for p, j in zip(pairs, js)]
        y_rows.append(jnp.concatenate([my[p][pw:] + y10[j][:, pw:] for p, j in zip(pairs, js)], axis=1))
    for p in pairs:
        h_ref[p] = h[p]
    y_ref[...] = jnp.concatenate(y_rows, axis=0)

    @pl.when(i == pl.num_programs(1) - 1)
    def _():
        for p in pairs:
            s_ref[0, p] = h[p]


def _wkv_prompt(r, k, v, lw, kk, b, nb_seq, seq):
    c = WKV_CHUNK * WKV_CHUNKS_PER_TILE
    assert WKV_CHUNK == HEAD_DIM and seq % c == 0
    nc = seq // c
    row = lambda bb, i: (bb * nc + i, 0)
    return pl.pallas_call(
        _wkv_prompt_kernel,
        grid=(nb_seq, nc),
        in_specs=[pl.BlockSpec((c, D_MODEL), row)] * 6,
        out_specs=[pl.BlockSpec((c, D_MODEL), row),
                   pl.BlockSpec((1, _N_PAIRS, _PAIR_W, _PAIR_W), lambda bb, i: (bb, 0, 0, 0))],
        out_shape=[jax.ShapeDtypeStruct((nb_seq * seq, D_MODEL), F32),
                   jax.ShapeDtypeStruct((nb_seq, _N_PAIRS, _PAIR_W, _PAIR_W), F32)],
        scratch_shapes=[pltpu.VMEM((_N_PAIRS, _PAIR_W, _PAIR_W), F32)],
        compiler_params=_cparams("parallel", "arbitrary"),
        name="wkv_prompt",
    )(r, k, v, lw, kk, b)


def _unpair_state(hb):
    n = hb.shape[0]
    h6 = hb.reshape(n, _N_PAIRS, 2, HEAD_DIM, 2, HEAD_DIM)
    blocks = jnp.stack([h6[:, :, 0, :, 0, :], h6[:, :, 1, :, 1, :]], axis=2)
    return jnp.swapaxes(blocks, -1, -2).reshape(n, N_HEADS, HEAD_DIM, HEAD_DIM)


_WKV_VROWS = 4


def _wkv_sample_kernel(r_ref, lw_ref, k_ref, v_ref, kk_ref, b_ref, s0_ref, y_ref, s_ref):
    t = r_ref.shape[1]
    vr = _WKV_VROWS

    def body(vi, carry):
        v0 = pl.multiple_of(vi * vr, vr)
        s = s0_ref[0, pl.ds(v0, vr)]
        for st in range(t):
            sa = -jnp.sum(s * kk_ref[0, st][None], axis=1, keepdims=True)
            vv = v_ref[0, st, pl.ds(v0, vr)]
            s = s * jnp.exp(lw_ref[0, st])[None] + sa * b_ref[0, st][None] + vv * k_ref[0, st][None]
            y_ref[0, st, pl.ds(v0, vr)] = jnp.sum(s * r_ref[0, st][None], axis=1, keepdims=True)
        s_ref[0, pl.ds(v0, vr)] = s
        return carry

    lax.fori_loop(0, HEAD_DIM // vr, body, 0)


def _wkv_sample(r, lw, k, v, kk, b, s0):
    nh, t, hd, n = r.shape
    vec = pl.BlockSpec((1, t, hd, n), lambda h: (h, 0, 0, 0))
    vec1 = pl.BlockSpec((1, t, hd, 1, n), lambda h: (h, 0, 0, 0, 0))
    st = pl.BlockSpec((1, hd, hd, n), lambda h: (h, 0, 0, 0))
    return pl.pallas_call(
        _wkv_sample_kernel,
        grid=(nh,),
        in_specs=[vec, vec, vec, vec1, vec, vec, st],
        out_specs=[vec1, st],
        out_shape=[jax.ShapeDtypeStruct((nh, t, hd, 1, n), F32),
                   jax.ShapeDtypeStruct((nh, hd, hd, n), F32)],
        compiler_params=_cparams("parallel"),
        name="wkv_sample",
    )(r, lw, k, v, kk, b, s0)


def _mix_out_kernel(x_ref, oa_ref, ob_ref, y_ref, r_ref, k_ref, v_ref, g_ref, wgate_ref,
                    lnw_ref, lnb_ref, rk_ref, wout_ref, g1_ref, b1_ref, e_ref, et_ref, o_ref, *, alpha):
    d = D_MODEL
    x = x_ref[...]
    gate = jnp.dot(x.astype(BF16), wgate_ref[...], preferred_element_type=F32)
    e, et = e_ref[...], et_ref[...]
    y = y_ref[...]
    mu = _head_sum(y, e, et) * (1.0 / HEAD_DIM)
    yc = y - mu
    var = _head_sum(yc * yc, e, et) * (1.0 / HEAD_DIM)
    yn = yc * lax.rsqrt(var + GN_EPS) * lnw_ref[...] + lnb_ref[...]
    v = v_ref[...]
    bonus = _head_sum(r_ref[...] * k_ref[...] * rk_ref[...], e, et) * v
    oc = (yn + bonus) * g_ref[...]
    mixed = (_sigmoid(gate[:, 0:d]) * oa_ref[...] + _sigmoid(gate[:, d:2 * d]) * ob_ref[...]
             + _sigmoid(gate[:, 2 * d:3 * d]) * oc)
    z = alpha * x + jnp.dot(mixed.astype(BF16), wout_ref[...], preferred_element_type=F32)
    o_ref[...] = _layer_norm(z, g1_ref[...], b1_ref[...])


def _mix_out(x, oa, ob, y, r, k, v, g, wgate, lnw, lnb, rk, wout, g1, b1, e, et, alpha):
    m = x.shape[0]
    tm = _tile(m, 256)
    row = lambda i: (i, 0)
    full = lambda i: (0, 0)
    rows = pl.BlockSpec((tm, D_MODEL), row)
    vec = pl.BlockSpec((1, D_MODEL), full)
    resident = lambda w: pl.BlockSpec(w.shape, full, pipeline_mode=pl.Buffered(1))
    return pl.pallas_call(
        functools.partial(_mix_out_kernel, alpha=alpha),
        grid=(m // tm,),
        in_specs=[rows] * 8 + [resident(wgate), vec, vec, vec, resident(wout), vec, vec,
                               pl.BlockSpec(e.shape, full), pl.BlockSpec(et.shape, full)],
        out_specs=rows,
        out_shape=jax.ShapeDtypeStruct((m, D_MODEL), F32),
        compiler_params=_cparams("parallel"),
        name="mix_out",
    )(x, oa, ob, y, r, k, v, g, wgate, lnw, lnb, rk, wout, g1, b1, e, et)


def _ffn_kernel(x_ref, wgu_ref, wd_ref, g2_ref, b2_ref, o_ref, *, alpha):
    dff = wd_ref.shape[0]
    x = x_ref[...]
    xb = x.astype(BF16)
    gt = jnp.dot(xb, wgu_ref[:, :dff], preferred_element_type=F32)
    up = jnp.dot(xb, wgu_ref[:, dff:], preferred_element_type=F32)
    ff = jnp.dot((gt * _sigmoid(gt) * up).astype(BF16), wd_ref[...], preferred_element_type=F32)
    o_ref[...] = _layer_norm(alpha * x + ff, g2_ref[...], b2_ref[...])


def _ffn(x, wgu, wd, g2, b2, alpha):
    m = x.shape[0]
    tm = _tile(m, 256)
    row = lambda i: (i, 0)
    full = lambda i: (0, 0)
    resident = lambda w: pl.BlockSpec(w.shape, full, pipeline_mode=pl.Buffered(1))
    return pl.pallas_call(
        functools.partial(_ffn_kernel, alpha=alpha),
        grid=(m // tm,),
        in_specs=[pl.BlockSpec((tm, D_MODEL), row), resident(wgu), resident(wd),
                  pl.BlockSpec((1, D_MODEL), full), pl.BlockSpec((1, D_MODEL), full)],
        out_specs=pl.BlockSpec((tm, D_MODEL), row),
        out_shape=jax.ShapeDtypeStruct((m, D_MODEL), F32),
        compiler_params=_cparams("parallel"),
        name="ffn",
    )(x, wgu, wd, g2, b2)


def _blockdiag_pairs(w):
    nb, bs, _ = w.shape
    w = w.reshape(nb // 2, 2, bs, bs)
    z = jnp.zeros((nb // 2, bs, bs), w.dtype)
    top = jnp.concatenate([w[:, 0], z], axis=2)
    bot = jnp.concatenate([z, w[:, 1]], axis=2)
    return jnp.concatenate([top, bot], axis=1)


def _row(p):
    return p.reshape(1, -1)


def kernel(x_prompt, x_sample, cache_k, cache_v, state_conv, state_rglru, state_shift, state_wkv, w_in, attn_sinks, conv_w, conv_b, rg_wa, rg_ba, rg_wx, rg_bx, rg_lambda, rw_mu, rw_w0, rw_wup, rw_a0, rw_aup, rw_gup, rw_kk, rw_ka, rw_rk, rw_lnw, rw_lnb, w_out, ln1_g, ln1_b, w_gu, w_down, ln2_g, ln2_b):
    depth = w_in.shape[0]
    bp, seq, d = x_prompt.shape
    ns, ts, _ = x_sample.shape
    wb = cache_k.shape[2]
    alpha = (2 * depth) ** 0.25
    dff = w_down.shape[1]
    pad = SHIFT_PAD - SHIFT_W

    xp = x_prompt.reshape(bp * seq, d)
    xs = jnp.swapaxes(x_sample, 0, 1).reshape(ts * ns, d)
    st_p = [[] for _ in range(6)]
    st_s = [[] for _ in range(6)]
    head_e, head_et = _head_indicator()
    w_in_bf = w_in.astype(BF16)
    w_gu_bf = w_gu.astype(BF16)
    w_down_bf = w_down.astype(BF16)

    for l in range(depth):
        wl = w_in_bf[l]
        w_all = wl[:, :OFF_RW]
        w_rw = jnp.pad(wl[:, OFF_RW:OFF_GATE], ((0, 0), (0, pad)))
        w_gate = wl[:, OFF_GATE:]
        wa2 = _blockdiag_pairs(rg_wa[l]).astype(BF16)
        wx2 = _blockdiag_pairs(rg_wx[l]).astype(BF16)
        rg_args = (conv_w[l], _row(conv_b[l]), wa2, _row(rg_ba[l]), wx2, _row(rg_bx[l]), _row(rg_lambda[l]))
        zw = jnp.zeros((LORA_W, d), F32)
        wwa = jnp.concatenate([jnp.concatenate([rw_wup[l], zw], axis=1),
                               jnp.concatenate([zw, rw_aup[l]], axis=1)], axis=0).astype(BF16)
        gup = jnp.pad(rw_gup[l], ((0, 2 * LANES - LORA_G), (0, 0))).astype(BF16)
        prep_params = (_row(jnp.pad(rw_mu[l], (0, pad))), _row(rw_w0[l]), _row(rw_a0[l]), _row(rw_kk[l]),
                       _row(rw_ka[l]), wwa, gup, head_e, head_et)
        mix_params = (_row(rw_lnw[l]), _row(rw_lnb[l]), _row(rw_rk[l]), w_out[l].astype(BF16),
                      _row(ln1_g[l]), _row(ln1_b[l]), head_e, head_et)
        ffn_params = (w_gu_bf[l], w_down_bf[l], _row(ln2_g[l]), _row(ln2_b[l]))
        sinks = attn_sinks[l]

        qkv, ob, rnn_tail = _proj_rglru_prompt(xp, w_all, bp, seq, *rg_args)
        oa = _attn_prompt(qkv, sinks, bp, seq)
        (r, k, v, lw, kk, b, g), sh_p = _rwkv_front(xp, w_rw, jnp.zeros((bp, 8, SHIFT_PAD), F32), bp, 1,
                                                    prep_params)
        y, hfin = _wkv_prompt(r, k, v, lw, kk, b, bp, seq)
        x1 = _mix_out(xp, oa, ob, y, r, k, v, g, w_gate, *mix_params, alpha)
        xp = _ffn(x1, *ffn_params, alpha)
        qkv3 = qkv.reshape(bp, seq, OFF_RNN)
        st_p[0].append(qkv3[:, seq - wb:, OFF_K:OFF_V].reshape(bp, wb, N_KV_HEADS, HEAD_DIM))
        st_p[1].append(qkv3[:, seq - wb:, OFF_V:OFF_RNN].reshape(bp, wb, N_KV_HEADS, HEAD_DIM))
        st_p[2].append(rnn_tail[:, 8 - (CONV_W - 1):])
        st_p[3].append(ob.reshape(bp, seq, d)[:, seq - 1])
        st_p[4].append(sh_p[:, 7, :SHIFT_W])
        st_p[5].append(_unpair_state(hfin))

        qkv, rnn = _in_proj(xs, w_all)
        qkv_n = jnp.swapaxes(qkv.reshape(ts, ns, OFF_RNN), 0, 1)
        oa_n, new_k, new_v = _attn_sample(qkv_n[:, :, :OFF_K], qkv_n[:, :, OFF_K:OFF_V], qkv_n[:, :, OFF_V:],
                                          cache_k[l].reshape(ns, wb, KV_W), cache_v[l].reshape(ns, wb, KV_W),
                                          sinks)
        oa = jnp.swapaxes(oa_n, 0, 1).reshape(ts * ns, d)
        rnn3 = rnn.reshape(ts, ns, d)
        ob3 = _rglru_sample(rnn3, jnp.swapaxes(state_conv[l], 0, 1), state_rglru[l], *rg_args)
        ob = ob3.reshape(ts * ns, d)
        shift = jnp.pad(state_shift[l], ((0, 0), (0, pad)))[None]
        (r, k, v, lw, kk, b, g), sh_s = _rwkv_front(xs, w_rw, shift, 1, ns, prep_params)

        def lanes(z):
            return jnp.transpose(z.reshape(ts, ns, N_HEADS, HEAD_DIM), (2, 0, 3, 1))

        s0 = jnp.transpose(state_wkv[l], (1, 2, 3, 0))
        y5, s_new = _wkv_sample(lanes(r), lanes(lw), lanes(k),
                                lanes(v).reshape(N_HEADS, ts, HEAD_DIM, 1, ns), lanes(kk), lanes(b), s0)
        y = jnp.transpose(y5.reshape(N_HEADS, ts, HEAD_DIM, ns), (1, 3, 0, 2)).reshape(ts * ns, d)
        x1 = _mix_out(xs, oa, ob, y, r, k, v, g, w_gate, *mix_params, alpha)
        xs = _ffn(x1, *ffn_params, alpha)
        st_s[0].append(new_k.reshape(ns, wb, N_KV_HEADS, HEAD_DIM))
        st_s[1].append(new_v.reshape(ns, wb, N_KV_HEADS, HEAD_DIM))
        st_s[2].append(jnp.swapaxes(rnn3[ts - (CONV_W - 1):], 0, 1))
        st_s[3].append(ob3[ts - 1])
        st_s[4].append(sh_s[0, :, :SHIFT_W])
        st_s[5].append(jnp.transpose(s_new, (3, 0, 1, 2)))

    yp = xp.reshape(bp, seq, d)
    ys = jnp.swapaxes(xs.reshape(ts, ns, d), 0, 1)
    return (yp, ys, *(jnp.stack(s) for s in st_p), *(jnp.stack(s) for s in st_s))
```

```python
import functools
import math

import jax
import jax.numpy as jnp
from jax import lax
from jax.experimental import pallas as pl
from jax.experimental.pallas import tpu as pltpu

F32 = jnp.float32
BF16 = jnp.bfloat16

D_MODEL = 1024
HEAD_DIM = 64
N_HEADS = 16
N_KV_HEADS = 4
KV_W = N_KV_HEADS * HEAD_DIM
WINDOW = 128
CONV_W = 4
RG_C = 8.0
LORA_W = 64
LORA_A = 64
LORA_G = 160
SHIFT_W = 3 * D_MODEL + LORA_W + LORA_A + LORA_G
LANES = 128
SHIFT_PAD = -(-SHIFT_W // LANES) * LANES
GN_EPS = 64e-5
LN_EPS = 1e-5
OFF_K = D_MODEL
OFF_V = OFF_K + KV_W
OFF_RNN = OFF_V + KV_W
OFF_RW = OFF_RNN + D_MODEL
OFF_GATE = OFF_RW + SHIFT_W
WKV_CHUNK = 64
WKV_CHUNKS_PER_TILE = 4
VMEM_LIMIT = 48 * 1024 * 1024


def _cparams(*sem):
    return pltpu.CompilerParams(dimension_semantics=sem, vmem_limit_bytes=VMEM_LIMIT)


def _tile(n, pref):
    if n <= pref:
        return n
    for t in range(pref, 7, -1):
        if n % t == 0 and t % 8 == 0:
            return t
    return n


def _dot(a, b):
    return jnp.dot(a.astype(BF16), b.astype(BF16), preferred_element_type=F32)


def _split_terms(x, n):
    terms = []
    rem = x
    for _ in range(n):
        hi = rem.astype(BF16)
        terms.append(hi)
        rem = rem - hi.astype(F32)
    return terms


def _dot_lhs_split(x, w_bf16, n):
    acc = None
    for t in _split_terms(x, n):
        p = jnp.dot(t, w_bf16, preferred_element_type=F32)
        acc = p if acc is None else acc + p
    return acc


def _dot_rhs_split(w_bf16, x, n):
    acc = None
    for t in _split_terms(x, n):
        p = jnp.dot(w_bf16, t, preferred_element_type=F32)
        acc = p if acc is None else acc + p
    return acc


def _sigmoid(x):
    return 0.5 * jnp.tanh(0.5 * x) + 0.5


def _softplus(x):
    return jnp.maximum(x, 0.0) + jnp.log(1.0 + jnp.exp(-jnp.abs(x)))


def _layer_norm(z, g, b):
    mu = jnp.mean(z, axis=-1, keepdims=True)
    zc = z - mu
    var = jnp.mean(zc * zc, axis=-1, keepdims=True)
    return zc * lax.rsqrt(var + LN_EPS) * g + b


def _head_indicator():
    c = lax.broadcasted_iota(jnp.int32, (D_MODEL, LANES), 0) // HEAD_DIM
    h = lax.broadcasted_iota(jnp.int32, (D_MODEL, LANES), 1)
    e = jnp.where(c == h, 1.0, 0.0).astype(BF16)
    ct = lax.broadcasted_iota(jnp.int32, (LANES, D_MODEL), 1) // HEAD_DIM
    ht = lax.broadcasted_iota(jnp.int32, (LANES, D_MODEL), 0)
    et = jnp.where(ct == ht, 1.0, 0.0).astype(BF16)
    return e, et


def _head_sum(x, e, et):
    s = jnp.dot(x.astype(BF16), e, preferred_element_type=F32)
    return _dot_lhs_split(s, et, 2)


_PROJ_WIDTHS = (OFF_RNN, D_MODEL)


def _in_proj_kernel(x_ref, w_ref, *o_refs):
    xb = x_ref[...].astype(BF16)
    off = 0
    for o_ref in o_refs:
        n = o_ref.shape[1]
        o_ref[...] = jnp.dot(xb, w_ref[:, off:off + n], preferred_element_type=F32)
        off += n


def _in_proj(x, w):
    m, k = x.shape
    tm = _tile(m, 256)
    return pl.pallas_call(
        _in_proj_kernel,
        grid=(m // tm,),
        in_specs=[pl.BlockSpec((tm, k), lambda i: (i, 0)),
                  pl.BlockSpec(w.shape, lambda i: (0, 0), pipeline_mode=pl.Buffered(1))],
        out_specs=[pl.BlockSpec((tm, n), lambda i: (i, 0)) for n in _PROJ_WIDTHS],
        out_shape=[jax.ShapeDtypeStruct((m, n), F32) for n in _PROJ_WIDTHS],
        compiler_params=_cparams("parallel"),
        name="in_proj",
    )(x, w)


def _alibi_slope(h):
    return 2.0 ** (-8.0 * (h + 1) / N_HEADS)


def _attn_prompt_bias():
    w = WINDOW
    rel = w + jnp.arange(w)[:, None] - jnp.arange(2 * w)[None, :]
    band = (rel >= 0) & (rel < WINDOW)
    has_prev = jnp.arange(2)[:, None, None] > 0
    valid = band[None] & (has_prev | (jnp.arange(2 * w) >= w)[None, None, :])
    slopes = jnp.array([_alibi_slope(h) for h in range(N_HEADS)], F32)
    bias = -slopes[None, :, None, None] * rel.astype(F32)[None, None]
    return jnp.where(valid[:, None], bias, -jnp.inf)


_ATTN_BLOCKS_PER_STEP = 2


def _attn_prompt_kernel(sink_ref, bias_ref, q_ref, kc_ref, vc_ref, kp_ref, vp_ref, o_ref):
    j = pl.program_id(1)
    w = WINDOW
    nblk = q_ref.shape[0] // w
    q = q_ref[...] * (HEAD_DIM ** -0.5)
    kall = jnp.concatenate([kp_ref[...], kc_ref[...]], axis=0).astype(BF16)
    vall = jnp.concatenate([vp_ref[...], vc_ref[...]], axis=0).astype(BF16)
    first = jnp.minimum(j, 1)
    gsz = N_HEADS // N_KV_HEADS
    items = [(n, h) for n in range(nblk) for h in range(N_HEADS)]
    kg = {(n, g): kall[n * w:(n + 2) * w, g * HEAD_DIM:(g + 1) * HEAD_DIM]
          for n in range(nblk) for g in range(N_KV_HEADS)}
    vg = {(n, g): vall[n * w:(n + 2) * w, g * HEAD_DIM:(g + 1) * HEAD_DIM]
          for n in range(nblk) for g in range(N_KV_HEADS)}
    qh = [q[n * w:(n + 1) * w, h * HEAD_DIM:(h + 1) * HEAD_DIM].astype(BF16) for n, h in items]
    bias = [bias_ref[first, h] if n == 0 else bias_ref[1, h] for n, h in items]
    s = [lax.dot_general(qh[i], kg[(n, h // gsz)], (((1,), (1,)), ((), ())), preferred_element_type=F32)
         + bias[i] for i, (n, h) in enumerate(items)]
    m = [jnp.maximum(jnp.max(s[i], axis=-1, keepdims=True), sink_ref[h]) for i, (n, h) in enumerate(items)]
    e = [jnp.exp(s[i] - m[i]) for i in range(len(items))]
    den = [jnp.sum(e[i], axis=-1, keepdims=True) + jnp.exp(sink_ref[h] - m[i])
           for i, (n, h) in enumerate(items)]
    p = [(e[i] / den[i]).astype(BF16) for i in range(len(items))]
    outs = [jnp.dot(p[i], vg[(n, h // gsz)], preferred_element_type=F32) for i, (n, h) in enumerate(items)]
    o_ref[...] = jnp.concatenate(
        [jnp.concatenate(outs[n * N_HEADS:(n + 1) * N_HEADS], axis=1) for n in range(nblk)], axis=0)


def _attn_prompt(qkv, sinks, nb_seq, seq):
    w = WINDOW
    nblk = _ATTN_BLOCKS_PER_STEP if seq % (_ATTN_BLOCKS_PER_STEP * w) == 0 else 1
    tq = nblk * w
    nb = seq // tq
    kcol = OFF_K // KV_W
    vcol = OFF_V // KV_W
    bias = _attn_prompt_bias()

    def cur(c):
        return lambda b, j: (b * nb + j, c)

    def prev(c):
        return lambda b, j: (jnp.maximum((b * nb + j) * nblk - 1, 0), c)

    return pl.pallas_call(
        _attn_prompt_kernel,
        grid=(nb_seq, nb),
        in_specs=[pl.BlockSpec(memory_space=pltpu.SMEM),
                  pl.BlockSpec(bias.shape, lambda b, j: (0, 0, 0, 0), pipeline_mode=pl.Buffered(1)),
                  pl.BlockSpec((tq, D_MODEL), cur(0)),
                  pl.BlockSpec((tq, KV_W), cur(kcol)),
                  pl.BlockSpec((tq, KV_W), cur(vcol)),
                  pl.BlockSpec((w, KV_W), prev(kcol)),
                  pl.BlockSpec((w, KV_W), prev(vcol))],
        out_specs=pl.BlockSpec((tq, D_MODEL), cur(0)),
        out_shape=jax.ShapeDtypeStruct((nb_seq * seq, D_MODEL), F32),
        compiler_params=_cparams("parallel", "parallel"),
        name="attn_prompt",
    )(sinks, bias, qkv, qkv, qkv, qkv, qkv)


def _attn_sample_kernel(sink_ref, q_ref, kn_ref, vn_ref, ck_ref, cv_ref, o_ref, nk_ref, nv_ref, *, t):
    sn = q_ref.shape[0]
    wb = ck_ref.shape[1]
    gsz = N_HEADS // N_KV_HEADS
    rows = gsz * t
    q = q_ref[...]
    kn = kn_ref[...]
    vn = vn_ref[...]
    ck = ck_ref[...]
    cv = cv_ref[...]
    tq1 = lax.broadcasted_iota(jnp.int32, (rows, wb), 0) % t
    j1 = lax.broadcasted_iota(jnp.int32, (rows, wb), 1)
    rel1 = wb + tq1 - j1
    valid1 = (rel1 >= 0) & (rel1 < WINDOW)
    tq2 = lax.broadcasted_iota(jnp.int32, (rows, t), 0) % t
    s2i = lax.broadcasted_iota(jnp.int32, (rows, t), 1)
    rel2 = tq2 - s2i
    valid2 = (rel2 >= 0) & (rel2 < WINDOW)
    hrow = lax.broadcasted_iota(jnp.int32, (rows, 1), 0) // t
    outs = []
    for g in range(N_KV_HEADS):
        slope = jnp.zeros((rows, 1), F32)
        sink = jnp.zeros((rows, 1), F32)
        for hh in range(gsz):
            h = g * gsz + hh
            slope = jnp.where(hrow == hh, _alibi_slope(h), slope)
            sink = jnp.where(hrow == hh, sink_ref[h], sink)
        qg = jnp.concatenate([q[:, :, (g * gsz + hh) * HEAD_DIM:(g * gsz + hh + 1) * HEAD_DIM]
                              for hh in range(gsz)], axis=1).astype(BF16)
        lo, hi = g * HEAD_DIM, (g + 1) * HEAD_DIM
        ckg = ck[:, :, lo:hi].astype(BF16)
        cvg = cv[:, :, lo:hi].astype(BF16)
        kng = kn[:, :, lo:hi].astype(BF16)
        vng = vn[:, :, lo:hi].astype(BF16)
        nt = (((2,), (2,)), ((0,), (0,)))
        nn = (((2,), (1,)), ((0,), (0,)))
        s1 = lax.dot_general(qg, ckg, nt, preferred_element_type=F32) * (HEAD_DIM ** -0.5)
        s2 = lax.dot_general(qg, kng, nt, preferred_element_type=F32) * (HEAD_DIM ** -0.5)
        s1 = jnp.where(valid1[None], s1 - (slope * rel1.astype(F32))[None], -jnp.inf)
        s2 = jnp.where(valid2[None], s2 - (slope * rel2.astype(F32))[None], -jnp.inf)
        m = jnp.maximum(jnp.maximum(jnp.max(s1, axis=-1, keepdims=True),
                                    jnp.max(s2, axis=-1, keepdims=True)), sink[None])
        e1 = jnp.exp(s1 - m)
        e2 = jnp.exp(s2 - m)
        den = (jnp.sum(e1, axis=-1, keepdims=True) + jnp.sum(e2, axis=-1, keepdims=True)
               + jnp.exp(sink[None] - m))
        p1 = (e1 / den).astype(BF16)
        p2 = (e2 / den).astype(BF16)
        og = (lax.dot_general(p1, cvg, nn, preferred_element_type=F32)
              + lax.dot_general(p2, vng, nn, preferred_element_type=F32))
        outs.extend(og[:, hh * t:(hh + 1) * t, :] for hh in range(gsz))
    o_ref[...] = jnp.concatenate(outs, axis=2)
    nk_ref[...] = jnp.concatenate([ck[:, t:, :], kn], axis=1)
    nv_ref[...] = jnp.concatenate([cv[:, t:, :], vn], axis=1)


def _attn_sample(q, kn, vn, ck, cv, sinks):
    n, t, _ = q.shape
    wb = ck.shape[1]
    sn = _tile(n, 16)
    blk = lambda d1, d2: pl.BlockSpec((sn, d1, d2), lambda i: (i, 0, 0))
    return pl.pallas_call(
        functools.partial(_attn_sample_kernel, t=t),
        grid=(n // sn,),
        in_specs=[pl.BlockSpec(memory_space=pltpu.SMEM),
                  blk(t, D_MODEL), blk(t, KV_W), blk(t, KV_W), blk(wb, KV_W), blk(wb, KV_W)],
        out_specs=[blk(t, D_MODEL), blk(wb, KV_W), blk(wb, KV_W)],
        out_shape=[jax.ShapeDtypeStruct((n, t, D_MODEL), F32),
                   jax.ShapeDtypeStruct((n, wb, KV_W), F32),
                   jax.ShapeDtypeStruct((n, wb, KV_W), F32)],
        compiler_params=_cparams("parallel"),
        name="attn_sample",
    )(sinks, q, kn, vn, ck, cv)


def _rglru_gates(xc, wa_ref, ba, wx_ref, bx, lam):
    xb = xc.astype(BF16)
    npair = D_MODEL // LANES
    ga = jnp.concatenate([jnp.dot(xb[:, j * LANES:(j + 1) * LANES], wa_ref[j], preferred_element_type=F32)
                          for j in range(npair)], axis=1)
    gx = jnp.concatenate([jnp.dot(xb[:, j * LANES:(j + 1) * LANES], wx_ref[j], preferred_element_type=F32)
                          for j in range(npair)], axis=1)
    r = _sigmoid(ga + ba)
    ig = _sigmoid(gx + bx)
    log_a = -RG_C * r * _softplus(-lam)
    a = jnp.exp(log_a)
    u = jnp.sqrt(-jnp.tanh(log_a) * (a * a + 1.0)) * (ig * xc)
    return a, u


def _proj_rglru_prompt_kernel(xin_ref, w_ref, cw_ref, cb_ref, wa_ref, ba_ref, wx_ref, bx_ref, lam_ref,
                              qkv_ref, h_ref, tail_out_ref, tail_ref, hc_ref):
    i = pl.program_id(1)
    tc = xin_ref.shape[0]

    @pl.when(i == 0)
    def _():
        tail_ref[...] = jnp.zeros_like(tail_ref)
        hc_ref[...] = jnp.zeros_like(hc_ref)

    xb = xin_ref[...].astype(BF16)
    qkv_ref[...] = jnp.dot(xb, w_ref[:, :OFF_RNN], preferred_element_type=F32)
    x = jnp.dot(xb, w_ref[:, OFF_RNN:], preferred_element_type=F32)
    xp = jnp.concatenate([tail_ref[...], x], axis=0)
    cw = cw_ref[...]
    xc = cb_ref[...] + x * cw[CONV_W - 1:CONV_W]
    for d in range(1, CONV_W):
        xc = xc + xp[8 - d:8 - d + tc] * cw[CONV_W - 1 - d:CONV_W - d]
    a, u = _rglru_gates(xc, wa_ref, ba_ref[...], wx_ref, bx_ref[...], lam_ref[...])
    ng = tc // 8
    acc_a = a.reshape(ng, 8, D_MODEL)
    acc_h = u.reshape(ng, 8, D_MODEL)
    sub = lax.broadcasted_iota(jnp.int32, (1, 8, D_MODEL), 1)
    for d in (1, 2, 4):
        keep = sub >= d
        sh_a = jnp.where(keep, pltpu.roll(acc_a, d, axis=1), 1.0)
        sh_h = jnp.where(keep, pltpu.roll(acc_h, d, axis=1), 0.0)
        acc_h = acc_h + acc_a * sh_h
        acc_a = acc_a * sh_a
    carry = hc_ref[...]
    for g in range(ng):
        hg = acc_h[g] + acc_a[g] * carry
        h_ref[g * 8:(g + 1) * 8, :] = hg
        carry = hg[7:8]
    hc_ref[...] = carry
    tail_ref[...] = x[tc - 8:tc]

    @pl.when(i == pl.num_programs(1) - 1)
    def _():
        tail_out_ref[0] = x[tc - 8:tc]


def _proj_rglru_prompt(x, w, nb_seq, seq, cw, cb, wa2, ba, wx2, bx, lam):
    tc = _tile(seq, 256)
    nt = seq // tc
    row = lambda b, i: (b * nt + i, 0)
    full2 = lambda b, i: (0, 0)
    full3 = lambda b, i: (0, 0, 0)
    return pl.pallas_call(
        _proj_rglru_prompt_kernel,
        grid=(nb_seq, nt),
        in_specs=[pl.BlockSpec((tc, D_MODEL), row),
                  pl.BlockSpec(w.shape, full2, pipeline_mode=pl.Buffered(1)),
                  pl.BlockSpec((CONV_W, D_MODEL), full2), pl.BlockSpec((1, D_MODEL), full2),
                  pl.BlockSpec(wa2.shape, full3), pl.BlockSpec((1, D_MODEL), full2),
                  pl.BlockSpec(wx2.shape, full3), pl.BlockSpec((1, D_MODEL), full2),
                  pl.BlockSpec((1, D_MODEL), full2)],
        out_specs=[pl.BlockSpec((tc, OFF_RNN), row), pl.BlockSpec((tc, D_MODEL), row),
                   pl.BlockSpec((1, 8, D_MODEL), lambda b, i: (b, 0, 0))],
        out_shape=[jax.ShapeDtypeStruct((nb_seq * seq, OFF_RNN), F32),
                   jax.ShapeDtypeStruct((nb_seq * seq, D_MODEL), F32),
                   jax.ShapeDtypeStruct((nb_seq, 8, D_MODEL), F32)],
        scratch_shapes=[pltpu.VMEM((8, D_MODEL), F32), pltpu.VMEM((1, D_MODEL), F32)],
        compiler_params=_cparams("parallel", "arbitrary"),
        name="proj_rglru_prompt",
    )(x, w, cw, cb, wa2, ba, wx2, bx, lam)


def _rglru_sample_kernel(x_ref, cs_ref, h0_ref, cw_ref, cb_ref, wa_ref, ba_ref, wx_ref, bx_ref, lam_ref,
                         h_ref):
    t, sn, _ = x_ref.shape
    cw = cw_ref[...]
    nst = CONV_W - 1

    def slab(s):
        return x_ref[s] if s >= 0 else cs_ref[nst + s]

    xcs = []
    for s in range(t):
        xc = cb_ref[...] + slab(s) * cw[CONV_W - 1:CONV_W]
        for d in range(1, CONV_W):
            xc = xc + slab(s - d) * cw[CONV_W - 1 - d:CONV_W - d]
        xcs.append(xc)
    xc_all = jnp.concatenate(xcs, axis=0)
    a, u = _rglru_gates(xc_all, wa_ref, ba_ref[...], wx_ref, bx_ref[...], lam_ref[...])
    h = h0_ref[...]
    for s in range(t):
        h = a[s * sn:(s + 1) * sn] * h + u[s * sn:(s + 1) * sn]
        h_ref[s] = h


def _rglru_sample(x, cs, h0, cw, cb, wa2, ba, wx2, bx, lam):
    t, n, _ = x.shape
    sn = _tile(n, 32)
    full2 = lambda i: (0, 0)
    full3 = lambda i: (0, 0, 0)
    return pl.pallas_call(
        _rglru_sample_kernel,
        grid=(n // sn,),
        in_specs=[pl.BlockSpec((t, sn, D_MODEL), lambda i: (0, i, 0)),
                  pl.BlockSpec((CONV_W - 1, sn, D_MODEL), lambda i: (0, i, 0)),
                  pl.BlockSpec((sn, D_MODEL), lambda i: (i, 0)),
                  pl.BlockSpec((CONV_W, D_MODEL), full2), pl.BlockSpec((1, D_MODEL), full2),
                  pl.BlockSpec(wa2.shape, full3), pl.BlockSpec((1, D_MODEL), full2),
                  pl.BlockSpec(wx2.shape, full3), pl.BlockSpec((1, D_MODEL), full2),
                  pl.BlockSpec((1, D_MODEL), full2)],
        out_specs=pl.BlockSpec((t, sn, D_MODEL), lambda i: (0, i, 0)),
        out_shape=jax.ShapeDtypeStruct((t, n, D_MODEL), F32),
        compiler_params=_cparams("parallel"),
        name="rglru_sample",
    )(x, cs, h0, cw, cb, wa2, ba, wx2, bx, lam)


def _rwkv_prep_math(pc, prev, mu, w0, a0, kkp, ka, wwa, gup, e, et):
    d = D_MODEL
    ps = pc + mu * (prev - pc)
    r = ps[:, 0:d]
    k = ps[:, d:2 * d]
    v = ps[:, 2 * d:3 * d]
    l01 = ps[:, 3 * d:3 * d + LANES]
    lane = lax.broadcasted_iota(jnp.int32, (1, LANES), 1)
    t01 = jnp.where(lane < LORA_W, jnp.tanh(l01), l01)
    wa = _dot(t01, wwa)
    g = _dot(_sigmoid(ps[:, 3 * d + LANES:]), gup)
    wlog = -_softplus(-(w0 + wa[:, :d])) - 0.5
    logw = -jnp.exp(wlog)
    a = _sigmoid(a0 + wa[:, d:])
    kk = k * kkp
    nrm = jnp.maximum(jnp.sqrt(_head_sum(kk * kk, e, et)), 1e-12)
    kkn = kk / nrm
    kmod = k * (1.0 + (a - 1.0) * ka)
    return r, kmod, v, logw, kkn, kkn * a, g


_P_R, _P_K, _P_V, _P_LW, _P_KK, _P_B, _P_G = range(7)
_N_PREP_OUT = 7


def _plane_spec(plane, rows, index):
    return pl.BlockSpec((None, rows, D_MODEL), lambda *g: (plane, index(*g), 0))


def _rwkv_front_kernel(x_ref, w_ref, sh0_ref, mu_ref, w0_ref, a0_ref, kkp_ref, ka_ref, wwa_ref, gup_ref,
                       e_ref, et_ref, out_ref, sh_out_ref, carry_ref, *, shift_rows):
    i = pl.program_id(1)
    tm = x_ref.shape[0]
    ncar = carry_ref.shape[0]

    @pl.when(i == 0)
    def _():
        carry_ref[...] = sh0_ref[0]

    pc = jnp.dot(x_ref[...].astype(BF16), w_ref[...], preferred_element_type=F32)
    prev = jnp.concatenate([carry_ref[ncar - shift_rows:, :], pc[:tm - shift_rows]], axis=0)
    outs = _rwkv_prep_math(pc, prev, mu_ref[...], w0_ref[...], a0_ref[...], kkp_ref[...], ka_ref[...],
                           wwa_ref[...], gup_ref[...], e_ref[...], et_ref[...])
    for j, o in enumerate(outs):
        out_ref[j] = o
    carry_ref[...] = pc[tm - ncar:]

    @pl.when(i == pl.num_programs(1) - 1)
    def _():
        sh_out_ref[0] = pc[tm - ncar:]


def _rwkv_front(x, w_rw, shift0, n_groups, shift_rows, params):
    rows = x.shape[0] // n_groups
    ncar = shift0.shape[1]
    tm = _tile(rows, 256)
    assert tm >= ncar >= shift_rows and tm % shift_rows == 0
    nt = rows // tm
    row = lambda g, i: (g * nt + i, 0)
    full = lambda g, i: (0, 0)
    grp = pl.BlockSpec((1, ncar, SHIFT_PAD), lambda g, i: (g, 0, 0))
    return pl.pallas_call(
        functools.partial(_rwkv_front_kernel, shift_rows=shift_rows),
        grid=(n_groups, nt),
        in_specs=[pl.BlockSpec((tm, D_MODEL), row),
                  pl.BlockSpec(w_rw.shape, full, pipeline_mode=pl.Buffered(1)), grp]
                 + [pl.BlockSpec(p.shape, full) for p in params],
        out_specs=[pl.BlockSpec((_N_PREP_OUT, tm, D_MODEL), lambda g, i: (0, g * nt + i, 0)), grp],
        out_shape=[jax.ShapeDtypeStruct((_N_PREP_OUT, n_groups * rows, D_MODEL), F32),
                   jax.ShapeDtypeStruct((n_groups, ncar, SHIFT_PAD), F32)],
        scratch_shapes=[pltpu.VMEM((ncar, SHIFT_PAD), F32)],
        compiler_params=_cparams("parallel", "arbitrary"),
        name="rwkv_front",
    )(x, w_rw, shift0, *params)


_PAIR_W = 2 * HEAD_DIM
_N_PAIRS = D_MODEL // _PAIR_W
_WKV_STATE_PASSES = 1


def _dot_tn(a, b):
    return lax.dot_general(a.astype(BF16), b.astype(BF16), (((0,), (0,)), ((), ())),
                           preferred_element_type=F32)


def _dot_nt(a, b):
    return lax.dot_general(a.astype(BF16), b.astype(BF16), (((1,), (1,)), ((), ())),
                           preferred_element_type=F32)


def _dot_passes(a, b, passes):
    if passes == 1:
        return _dot(a, b)
    ah, al = _split_terms(a, 2)
    bh, bl = _split_terms(b, 2)
    return (jnp.dot(ah, bh, preferred_element_type=F32) + jnp.dot(ah, bl, preferred_element_type=F32)
            + jnp.dot(al, bh, preferred_element_type=F32))


def _wkv_prompt_kernel(r_ref, k_ref, v_ref, lw_ref, kk_ref, b_ref, y_ref, s_ref, h_ref):
    i = pl.program_id(1)
    rows = r_ref.shape[0]
    c = WKV_CHUNK
    nck = rows // c
    hd = HEAD_DIM
    pw = _PAIR_W

    @pl.when(i == 0)
    def _():
        h_ref[...] = jnp.zeros_like(h_ref)

    row_t = lax.broadcasted_iota(jnp.int32, (rows, rows), 0)
    col_t = lax.broadcasted_iota(jnp.int32, (rows, rows), 1)
    tri = (row_t >= col_t) & (row_t // c == col_t // c)
    lw = lw_ref[...]
    cum = _dot_rhs_split(jnp.where(tri, 1.0, 0.0).astype(BF16), lw, 3)
    last = jnp.concatenate([jnp.broadcast_to(cum[(ci + 1) * c - 1:(ci + 1) * c], (c, D_MODEL))
                            for ci in range(nck)], axis=0)
    g_inv = jnp.exp(-cum)
    g_end = jnp.exp(last - cum)
    kk = kk_ref[...]
    b = b_ref[...]
    k = k_ref[...]
    at_all = -(kk * jnp.exp(cum - lw))
    rt_all = r_ref[...] * jnp.exp(cum)
    bt_all = b * g_inv
    kt_all = k * g_inv
    bh_all = b * g_end
    kh_all = k * g_end
    gc_all = jnp.exp(last)

    rowp = lax.broadcasted_iota(jnp.int32, (c, pw), 0)
    colp = lax.broadcasted_iota(jnp.int32, (c, pw), 1) % hd
    strict = rowp > colp
    incl = rowp >= colp
    eye_p = jnp.where(rowp == colp, 1.0, 0.0)
    left = lax.broadcasted_iota(jnp.int32, (1, pw), 1) < hd
    r2 = lax.broadcasted_iota(jnp.int32, (pw, pw), 0)
    c2 = lax.broadcasted_iota(jnp.int32, (pw, pw), 1)
    same_head = (r2 < hd) == (c2 < hd)
    eye2 = r2 == c2

    def bd(w):
        return jnp.concatenate([jnp.where(left, w, 0.0), jnp.where(left, 0.0, w)], axis=0)

    pairs = range(_N_PAIRS)
    items = [(ci, p) for ci in range(nck) for p in pairs]
    idx = range(len(items))
    sub = [(slice(ci * c, (ci + 1) * c), slice(p * pw, (p + 1) * pw)) for ci, p in items]
    v_all = v_ref[...]
    at = [at_all[s] for s in sub]
    rt = [rt_all[s] for s in sub]
    v = [v_all[s] for s in sub]
    amat = [_dot_nt(jnp.concatenate([at[j], rt[j]], axis=0),
                    jnp.concatenate([bd(bt_all[sub[j]]), bd(kt_all[sub[j]])], axis=0))
            for j in idx]
    a_ab = [jnp.where(strict, a[:c, :pw], 0.0) for a in amat]
    a_ak = [jnp.where(strict, a[:c, pw:], 0.0) for a in amat]
    a_rb = [jnp.where(incl, a[c:, :pw], 0.0) for a in amat]
    a_rk = [jnp.where(incl, a[c:, pw:], 0.0) for a in amat]
    tinv = [eye_p + a for a in a_ab]
    npow = [_dot(a, bd(a)) for a in a_ab]
    span = 4
    while span < c:
        both = [_dot(jnp.concatenate([tinv[j], npow[j]], axis=0), bd(npow[j])) for j in idx]
        tinv = [tinv[j] + both[j][:c] for j in idx]
        npow = [x[c:] for x in both]
        span *= 2
    tinv = [tinv[j] + _dot(tinv[j], bd(npow[j])) for j in idx]
    w12 = [_dot(jnp.concatenate([a_ak[j], a_rk[j]], axis=0), bd(v[j])) for j in idx]
    pq = [_dot(tinv[j], jnp.concatenate([bd(at[j]), bd(w12[j][:c])], axis=1)) for j in idx]
    y10 = [jnp.concatenate([rt[j], w12[j][c:]], axis=1)
           + _dot(a_rb[j], jnp.concatenate([bd(pq[j][:, :pw]), bd(pq[j][:, pw:])], axis=1))
           for j in idx]
    mn = [_dot_tn(bh_all[sub[j]], pq[j]) for j in idx]
    kv = [_dot_tn(kh_all[sub[j]], v[j]) for j in idx]
    m = [jnp.where(same_head, mn[j][:, :pw], 0.0)
         + jnp.where(eye2, gc_all[sub[j][0], sub[j][1]][:1], 0.0) for j in idx]
    n = [jnp.where(same_head, mn[j][:, pw:] + kv[j], 0.0) for j in idx]
    h = [h_ref[p] for p in pairs]
    y_rows = []
    for ci in range(nck):
        js = [ci * _N_PAIRS + p for p in pairs]
        my = [_dot_passes(jnp.concatenate([m[j], y10[j][:, :pw]], axis=0), h[p], _WKV_STATE_PASSES)
              for p, j in zip(pairs, js)]
        h = [my[p][:pw] + n[j] for p, j in zip(pairs, js)]
        y_rows.append(jnp.concatenate([my[p][pw:] + y10[j][:, pw:] for p, j in zip(pairs, js)], axis=1))
    for p in pairs:
        h_ref[p] = h[p]
    y_ref[...] = jnp.concatenate(y_rows, axis=0)

    @pl.when(i == pl.num_programs(1) - 1)
    def _():
        for p in pairs:
            s_ref[0, p] = h[p]


def _wkv_prompt(prep, nb_seq, seq):
    c = WKV_CHUNK * WKV_CHUNKS_PER_TILE
    assert WKV_CHUNK == HEAD_DIM and seq % c == 0
    nc = seq // c
    row = lambda bb, i: (bb * nc + i, 0)
    planes = (_P_R, _P_K, _P_V, _P_LW, _P_KK, _P_B)
    return pl.pallas_call(
        _wkv_prompt_kernel,
        grid=(nb_seq, nc),
        in_specs=[_plane_spec(p, c, lambda bb, i: bb * nc + i) for p in planes],
        out_specs=[pl.BlockSpec((c, D_MODEL), row),
                   pl.BlockSpec((1, _N_PAIRS, _PAIR_W, _PAIR_W), lambda bb, i: (bb, 0, 0, 0))],
        out_shape=[jax.ShapeDtypeStruct((nb_seq * seq, D_MODEL), F32),
                   jax.ShapeDtypeStruct((nb_seq, _N_PAIRS, _PAIR_W, _PAIR_W), F32)],
        scratch_shapes=[pltpu.VMEM((_N_PAIRS, _PAIR_W, _PAIR_W), F32)],
        compiler_params=_cparams("parallel", "arbitrary"),
        name="wkv_prompt",
    )(*([prep] * len(planes)))


def _unpair_state(hb):
    n = hb.shape[0]
    h6 = hb.reshape(n, _N_PAIRS, 2, HEAD_DIM, 2, HEAD_DIM)
    blocks = jnp.stack([h6[:, :, 0, :, 0, :], h6[:, :, 1, :, 1, :]], axis=2)
    return jnp.swapaxes(blocks, -1, -2).reshape(n, N_HEADS, HEAD_DIM, HEAD_DIM)


_WKV_VROWS = 4


def _wkv_sample_kernel(r_ref, lw_ref, k_ref, v_ref, kk_ref, b_ref, s0_ref, y_ref, s_ref, w_ref):
    t = r_ref.shape[0]
    vr = _WKV_VROWS
    w_ref[...] = jnp.exp(lw_ref[...])

    def body(vi, carry):
        v0 = pl.multiple_of(vi * vr, vr)
        s = s0_ref[0, pl.ds(v0, vr)]
        for st in range(t):
            sa = -jnp.sum(s * kk_ref[st][None], axis=1, keepdims=True)
            vv = v_ref[0, st, pl.ds(v0, vr)]
            s = s * w_ref[st][None] + sa * b_ref[st][None] + vv * k_ref[st][None]
            y_ref[0, st, pl.ds(v0, vr)] = jnp.sum(s * r_ref[st][None], axis=1, keepdims=True)
        s_ref[0, pl.ds(v0, vr)] = s
        return carry

    lax.fori_loop(0, HEAD_DIM // vr, body, 0)


def _wkv_sample(vecs, v, s0):
    _, nh, t, hd, n = vecs.shape
    vec = lambda p: pl.BlockSpec((None, None, t, hd, n), lambda h: (p, h, 0, 0, 0))
    vec1 = pl.BlockSpec((1, t, hd, 1, n), lambda h: (h, 0, 0, 0, 0))
    st = pl.BlockSpec((1, hd, hd, n), lambda h: (h, 0, 0, 0))
    return pl.pallas_call(
        _wkv_sample_kernel,
        grid=(nh,),
        in_specs=[vec(_P_R), vec(_P_LW), vec(_P_K), vec1, vec(_P_KK), vec(_P_B), st],
        out_specs=[vec1, st],
        out_shape=[jax.ShapeDtypeStruct((nh, t, hd, 1, n), F32),
                   jax.ShapeDtypeStruct((nh, hd, hd, n), F32)],
        scratch_shapes=[pltpu.VMEM((t, hd, n), F32)],
        compiler_params=_cparams("parallel"),
        name="wkv_sample",
    )(vecs, vecs, vecs, v, vecs, vecs, s0)


def _mix_out_kernel(x_ref, oa_ref, ob_ref, y_ref, r_ref, k_ref, v_ref, g_ref, wgate_ref,
                    lnw_ref, lnb_ref, rk_ref, wout_ref, g1_ref, b1_ref, e_ref, et_ref, o_ref, *, alpha):
    d = D_MODEL
    x = x_ref[...]
    gate = jnp.dot(x.astype(BF16), wgate_ref[...], preferred_element_type=F32)
    e, et = e_ref[...], et_ref[...]
    y = y_ref[...]
    mu = _head_sum(y, e, et) * (1.0 / HEAD_DIM)
    yc = y - mu
    var = _head_sum(yc * yc, e, et) * (1.0 / HEAD_DIM)
    yn = yc * lax.rsqrt(var + GN_EPS) * lnw_ref[...] + lnb_ref[...]
    v = v_ref[...]
    bonus = _head_sum(r_ref[...] * k_ref[...] * rk_ref[...], e, et) * v
    oc = (yn + bonus) * g_ref[...]
    mixed = (_sigmoid(gate[:, 0:d]) * oa_ref[...] + _sigmoid(gate[:, d:2 * d]) * ob_ref[...]
             + _sigmoid(gate[:, 2 * d:3 * d]) * oc)
    z = alpha * x + jnp.dot(mixed.astype(BF16), wout_ref[...], preferred_element_type=F32)
    o_ref[...] = _layer_norm(z, g1_ref[...], b1_ref[...])


def _mix_out(x, oa, ob, y, prep, wgate, lnw, lnb, rk, wout, g1, b1, e, et, alpha):
    m = x.shape[0]
    tm = _tile(m, 256)
    row = lambda i: (i, 0)
    full = lambda i: (0, 0)
    rows = pl.BlockSpec((tm, D_MODEL), row)
    vec = pl.BlockSpec((1, D_MODEL), full)
    resident = lambda w: pl.BlockSpec(w.shape, full, pipeline_mode=pl.Buffered(1))
    planes = (_P_R, _P_K, _P_V, _P_G)
    return pl.pallas_call(
        functools.partial(_mix_out_kernel, alpha=alpha),
        grid=(m // tm,),
        in_specs=[rows] * 4 + [_plane_spec(p, tm, lambda i: i) for p in planes]
                 + [resident(wgate), vec, vec, vec, resident(wout), vec, vec,
                    pl.BlockSpec(e.shape, full), pl.BlockSpec(et.shape, full)],
        out_specs=rows,
        out_shape=jax.ShapeDtypeStruct((m, D_MODEL), F32),
        compiler_params=_cparams("parallel"),
        name="mix_out",
    )(x, oa, ob, y, *([prep] * len(planes)), wgate, lnw, lnb, rk, wout, g1, b1, e, et)


def _ffn_kernel(x_ref, wgu_ref, wd_ref, g2_ref, b2_ref, o_ref, *, alpha):
    dff = wd_ref.shape[0]
    x = x_ref[...]
    xb = x.astype(BF16)
    gt = jnp.dot(xb, wgu_ref[:, :dff], preferred_element_type=F32)
    up = jnp.dot(xb, wgu_ref[:, dff:], preferred_element_type=F32)
    ff = jnp.dot((gt * _sigmoid(gt) * up).astype(BF16), wd_ref[...], preferred_element_type=F32)
    o_ref[...] = _layer_norm(alpha * x + ff, g2_ref[...], b2_ref[...])


def _ffn(x, wgu, wd, g2, b2, alpha):
    m = x.shape[0]
    tm = _tile(m, 256)
    row = lambda i: (i, 0)
    full = lambda i: (0, 0)
    resident = lambda w: pl.BlockSpec(w.shape, full, pipeline_mode=pl.Buffered(1))
    return pl.pallas_call(
        functools.partial(_ffn_kernel, alpha=alpha),
        grid=(m // tm,),
        in_specs=[pl.BlockSpec((tm, D_MODEL), row), resident(wgu), resident(wd),
                  pl.BlockSpec((1, D_MODEL), full), pl.BlockSpec((1, D_MODEL), full)],
        out_specs=pl.BlockSpec((tm, D_MODEL), row),
        out_shape=jax.ShapeDtypeStruct((m, D_MODEL), F32),
        compiler_params=_cparams("parallel"),
        name="ffn",
    )(x, wgu, wd, g2, b2)


def _blockdiag_pairs(w):
    nb, bs, _ = w.shape
    w = w.reshape(nb // 2, 2, bs, bs)
    z = jnp.zeros((nb // 2, bs, bs), w.dtype)
    top = jnp.concatenate([w[:, 0], z], axis=2)
    bot = jnp.concatenate([z, w[:, 1]], axis=2)
    return jnp.concatenate([top, bot], axis=1)


def _row(p):
    return p.reshape(1, -1)


def kernel(x_prompt, x_sample, cache_k, cache_v, state_conv, state_rglru, state_shift, state_wkv, w_in, attn_sinks, conv_w, conv_b, rg_wa, rg_ba, rg_wx, rg_bx, rg_lambda, rw_mu, rw_w0, rw_wup, rw_a0, rw_aup, rw_gup, rw_kk, rw_ka, rw_rk, rw_lnw, rw_lnb, w_out, ln1_g, ln1_b, w_gu, w_down, ln2_g, ln2_b):
    depth = w_in.shape[0]
    bp, seq, d = x_prompt.shape
    ns, ts, _ = x_sample.shape
    wb = cache_k.shape[2]
    alpha = (2 * depth) ** 0.25
    dff = w_down.shape[1]
    pad = SHIFT_PAD - SHIFT_W

    xp = x_prompt.reshape(bp * seq, d)
    xs = jnp.swapaxes(x_sample, 0, 1).reshape(ts * ns, d)
    st_p = [[] for _ in range(6)]
    st_s = [[] for _ in range(6)]
    head_e, head_et = _head_indicator()
    w_in_bf = w_in.astype(BF16)
    w_gu_bf = w_gu.astype(BF16)
    w_down_bf = w_down.astype(BF16)

    for l in range(depth):
        wl = w_in_bf[l]
        w_all = wl[:, :OFF_RW]
        w_rw = jnp.pad(wl[:, OFF_RW:OFF_GATE], ((0, 0), (0, pad)))
        w_gate = wl[:, OFF_GATE:]
        wa2 = _blockdiag_pairs(rg_wa[l]).astype(BF16)
        wx2 = _blockdiag_pairs(rg_wx[l]).astype(BF16)
        rg_args = (conv_w[l], _row(conv_b[l]), wa2, _row(rg_ba[l]), wx2, _row(rg_bx[l]), _row(rg_lambda[l]))
        zw = jnp.zeros((LORA_W, d), F32)
        wwa = jnp.concatenate([jnp.concatenate([rw_wup[l], zw], axis=1),
                               jnp.concatenate([zw, rw_aup[l]], axis=1)], axis=0).astype(BF16)
        gup = jnp.pad(rw_gup[l], ((0, 2 * LANES - LORA_G), (0, 0))).astype(BF16)
        prep_params = (_row(jnp.pad(rw_mu[l], (0, pad))), _row(rw_w0[l]), _row(rw_a0[l]), _row(rw_kk[l]),
                       _row(rw_ka[l]), wwa, gup, head_e, head_et)
        mix_params = (_row(rw_lnw[l]), _row(rw_lnb[l]), _row(rw_rk[l]), w_out[l].astype(BF16),
                      _row(ln1_g[l]), _row(ln1_b[l]), head_e, head_et)
        ffn_params = (w_gu_bf[l], w_down_bf[l], _row(ln2_g[l]), _row(ln2_b[l]))
        sinks = attn_sinks[l]

        qkv, ob, rnn_tail = _proj_rglru_prompt(xp, w_all, bp, seq, *rg_args)
        oa = _attn_prompt(qkv, sinks, bp, seq)
        prep, sh_p = _rwkv_front(xp, w_rw, jnp.zeros((bp, 8, SHIFT_PAD), F32), bp, 1, prep_params)
        y, hfin = _wkv_prompt(prep, bp, seq)
        x1 = _mix_out(xp, oa, ob, y, prep, w_gate, *mix_params, alpha)
        xp = _ffn(x1, *ffn_params, alpha)
        qkv3 = qkv.reshape(bp, seq, OFF_RNN)
        st_p[0].append(qkv3[:, seq - wb:, OFF_K:OFF_V].reshape(bp, wb, N_KV_HEADS, HEAD_DIM))
        st_p[1].append(qkv3[:, seq - wb:, OFF_V:OFF_RNN].reshape(bp, wb, N_KV_HEADS, HEAD_DIM))
        st_p[2].append(rnn_tail[:, 8 - (CONV_W - 1):])
        st_p[3].append(ob.reshape(bp, seq, d)[:, seq - 1])
        st_p[4].append(sh_p[:, 7, :SHIFT_W])
        st_p[5].append(_unpair_state(hfin))

        qkv, rnn = _in_proj(xs, w_all)
        qkv_n = jnp.swapaxes(qkv.reshape(ts, ns, OFF_RNN), 0, 1)
        oa_n, new_k, new_v = _attn_sample(qkv_n[:, :, :OFF_K], qkv_n[:, :, OFF_K:OFF_V], qkv_n[:, :, OFF_V:],
                                          cache_k[l].reshape(ns, wb, KV_W), cache_v[l].reshape(ns, wb, KV_W),
                                          sinks)
        oa = jnp.swapaxes(oa_n, 0, 1).reshape(ts * ns, d)
        rnn3 = rnn.reshape(ts, ns, d)
        ob3 = _rglru_sample(rnn3, jnp.swapaxes(state_conv[l], 0, 1), state_rglru[l], *rg_args)
        ob = ob3.reshape(ts * ns, d)
        shift = jnp.pad(state_shift[l], ((0, 0), (0, pad)))[None]
        prep, sh_s = _rwkv_front(xs, w_rw, shift, 1, ns, prep_params)
        vecs = jnp.transpose(prep.reshape(_N_PREP_OUT, ts, ns, N_HEADS, HEAD_DIM), (0, 3, 1, 4, 2))
        s0 = jnp.transpose(state_wkv[l], (1, 2, 3, 0))
        y5, s_new = _wkv_sample(vecs, vecs[_P_V].reshape(N_HEADS, ts, HEAD_DIM, 1, ns), s0)
        y = jnp.transpose(y5.reshape(N_HEADS, ts, HEAD_DIM, ns), (1, 3, 0, 2)).reshape(ts * ns, d)
        x1 = _mix_out(xs, oa, ob, y, prep, w_gate, *mix_params, alpha)
        xs = _ffn(x1, *ffn_params, alpha)
        st_s[0].append(new_k.reshape(ns, wb, N_KV_HEADS, HEAD_DIM))
        st_s[1].append(new_v.reshape(ns, wb, N_KV_HEADS, HEAD_DIM))
        st_s[2].append(jnp.swapaxes(rnn3[ts - (CONV_W - 1):], 0, 1))
        st_s[3].append(ob3[ts - 1])
        st_s[4].append(sh_s[0, :, :SHIFT_W])
        st_s[5].append(jnp.transpose(s_new, (3, 0, 1, 2)))

    yp = xp.reshape(bp, seq, d)
    ys = jnp.swapaxes(xs.reshape(ts, ns, d), 0, 1)
    return (yp, ys, *(jnp.stack(s) for s in st_p), *(jnp.stack(s) for s in st_s))
```

```python
import functools
import math

import jax
import jax.numpy as jnp
from jax import lax
from jax.experimental import pallas as pl
from jax.experimental.pallas import tpu as pltpu

F32 = jnp.float32
BF16 = jnp.bfloat16

D_MODEL = 1024
HEAD_DIM = 64
N_HEADS = 16
N_KV_HEADS = 4
KV_W = N_KV_HEADS * HEAD_DIM
WINDOW = 128
CONV_W = 4
RG_C = 8.0
LORA_W = 64
LORA_A = 64
LORA_G = 160
SHIFT_W = 3 * D_MODEL + LORA_W + LORA_A + LORA_G
LANES = 128
SHIFT_PAD = -(-SHIFT_W // LANES) * LANES
GN_EPS = 64e-5
LN_EPS = 1e-5
OFF_K = D_MODEL
OFF_V = OFF_K + KV_W
OFF_RNN = OFF_V + KV_W
OFF_RW = OFF_RNN + D_MODEL
OFF_GATE = OFF_RW + SHIFT_W
WKV_CHUNK = 64
WKV_CHUNKS_PER_TILE = 4
VMEM_LIMIT = 48 * 1024 * 1024


def _cparams(*sem):
    return pltpu.CompilerParams(dimension_semantics=sem, vmem_limit_bytes=VMEM_LIMIT)


def _tile(n, pref):
    if n <= pref:
        return n
    for t in range(pref, 7, -1):
        if n % t == 0 and t % 8 == 0:
            return t
    return n


def _dot(a, b):
    return jnp.dot(a.astype(BF16), b.astype(BF16), preferred_element_type=F32)


def _split_terms(x, n):
    terms = []
    rem = x
    for _ in range(n):
        hi = rem.astype(BF16)
        terms.append(hi)
        rem = rem - hi.astype(F32)
    return terms


def _dot_lhs_split(x, w_bf16, n):
    acc = None
    for t in _split_terms(x, n):
        p = jnp.dot(t, w_bf16, preferred_element_type=F32)
        acc = p if acc is None else acc + p
    return acc


def _dot_rhs_split(w_bf16, x, n):
    acc = None
    for t in _split_terms(x, n):
        p = jnp.dot(w_bf16, t, preferred_element_type=F32)
        acc = p if acc is None else acc + p
    return acc


def _sigmoid(x):
    return 0.5 * jnp.tanh(0.5 * x) + 0.5


def _softplus(x):
    return jnp.maximum(x, 0.0) + jnp.log(1.0 + jnp.exp(-jnp.abs(x)))


def _layer_norm(z, g, b):
    mu = jnp.mean(z, axis=-1, keepdims=True)
    zc = z - mu
    var = jnp.mean(zc * zc, axis=-1, keepdims=True)
    return zc * lax.rsqrt(var + LN_EPS) * g + b


def _head_indicator():
    c = lax.broadcasted_iota(jnp.int32, (D_MODEL, LANES), 0) // HEAD_DIM
    h = lax.broadcasted_iota(jnp.int32, (D_MODEL, LANES), 1)
    e = jnp.where(c == h, 1.0, 0.0).astype(BF16)
    ct = lax.broadcasted_iota(jnp.int32, (LANES, D_MODEL), 1) // HEAD_DIM
    ht = lax.broadcasted_iota(jnp.int32, (LANES, D_MODEL), 0)
    et = jnp.where(ct == ht, 1.0, 0.0).astype(BF16)
    return e, et


def _head_sum(x, e, et):
    s = jnp.dot(x.astype(BF16), e, preferred_element_type=F32)
    return _dot_lhs_split(s, et, 2)


_PROJ_WIDTHS = (OFF_RNN, D_MODEL)


def _in_proj_kernel(x_ref, w_ref, *o_refs):
    xb = x_ref[...].astype(BF16)
    off = 0
    for o_ref in o_refs:
        n = o_ref.shape[1]
        o_ref[...] = jnp.dot(xb, w_ref[:, off:off + n], preferred_element_type=F32)
        off += n


def _in_proj(x, w):
    m, k = x.shape
    tm = _tile(m, 256)
    return pl.pallas_call(
        _in_proj_kernel,
        grid=(m // tm,),
        in_specs=[pl.BlockSpec((tm, k), lambda i: (i, 0)),
                  pl.BlockSpec(w.shape, lambda i: (0, 0), pipeline_mode=pl.Buffered(1))],
        out_specs=[pl.BlockSpec((tm, n), lambda i: (i, 0)) for n in _PROJ_WIDTHS],
        out_shape=[jax.ShapeDtypeStruct((m, n), F32) for n in _PROJ_WIDTHS],
        compiler_params=_cparams("parallel"),
        name="in_proj",
    )(x, w)


def _alibi_slope(h):
    return 2.0 ** (-8.0 * (h + 1) / N_HEADS)


def _attn_prompt_bias():
    w = WINDOW
    rel = w + jnp.arange(w)[:, None] - jnp.arange(2 * w)[None, :]
    band = (rel >= 0) & (rel < WINDOW)
    has_prev = jnp.arange(2)[:, None, None] > 0
    valid = band[None] & (has_prev | (jnp.arange(2 * w) >= w)[None, None, :])
    slopes = jnp.array([_alibi_slope(h) for h in range(N_HEADS)], F32)
    bias = -slopes[None, :, None, None] * rel.astype(F32)[None, None]
    return jnp.where(valid[:, None], bias, -jnp.inf)


_ATTN_BLOCKS_PER_STEP = 2


def _attn_prompt_kernel(sink_ref, bias_ref, q_ref, kc_ref, vc_ref, kp_ref, vp_ref, o_ref):
    j = pl.program_id(1)
    w = WINDOW
    nblk = q_ref.shape[0] // w
    q = q_ref[...] * (HEAD_DIM ** -0.5)
    kall = jnp.concatenate([kp_ref[...], kc_ref[...]], axis=0).astype(BF16)
    vall = jnp.concatenate([vp_ref[...], vc_ref[...]], axis=0).astype(BF16)
    first = jnp.minimum(j, 1)
    gsz = N_HEADS // N_KV_HEADS
    items = [(n, h) for n in range(nblk) for h in range(N_HEADS)]
    kg = {(n, g): kall[n * w:(n + 2) * w, g * HEAD_DIM:(g + 1) * HEAD_DIM]
          for n in range(nblk) for g in range(N_KV_HEADS)}
    vg = {(n, g): vall[n * w:(n + 2) * w, g * HEAD_DIM:(g + 1) * HEAD_DIM]
          for n in range(nblk) for g in range(N_KV_HEADS)}
    qh = [q[n * w:(n + 1) * w, h * HEAD_DIM:(h + 1) * HEAD_DIM].astype(BF16) for n, h in items]
    bias = [bias_ref[first, h] if n == 0 else bias_ref[1, h] for n, h in items]
    s = [lax.dot_general(qh[i], kg[(n, h // gsz)], (((1,), (1,)), ((), ())), preferred_element_type=F32)
         + bias[i] for i, (n, h) in enumerate(items)]
    m = [jnp.maximum(jnp.max(s[i], axis=-1, keepdims=True), sink_ref[h]) for i, (n, h) in enumerate(items)]
    e = [jnp.exp(s[i] - m[i]) for i in range(len(items))]
    den = [jnp.sum(e[i], axis=-1, keepdims=True) + jnp.exp(sink_ref[h] - m[i])
           for i, (n, h) in enumerate(items)]
    p = [(e[i] / den[i]).astype(BF16) for i in range(len(items))]
    outs = [jnp.dot(p[i], vg[(n, h // gsz)], preferred_element_type=F32) for i, (n, h) in enumerate(items)]
    o_ref[...] = jnp.concatenate(
        [jnp.concatenate(outs[n * N_HEADS:(n + 1) * N_HEADS], axis=1) for n in range(nblk)], axis=0)


def _attn_prompt(qkv, sinks, nb_seq, seq):
    w = WINDOW
    nblk = _ATTN_BLOCKS_PER_STEP if seq % (_ATTN_BLOCKS_PER_STEP * w) == 0 else 1
    tq = nblk * w
    nb = seq // tq
    kcol = OFF_K // KV_W
    vcol = OFF_V // KV_W
    bias = _attn_prompt_bias()

    def cur(c):
        return lambda b, j: (b * nb + j, c)

    def prev(c):
        return lambda b, j: (jnp.maximum((b * nb + j) * nblk - 1, 0), c)

    return pl.pallas_call(
        _attn_prompt_kernel,
        grid=(nb_seq, nb),
        in_specs=[pl.BlockSpec(memory_space=pltpu.SMEM),
                  pl.BlockSpec(bias.shape, lambda b, j: (0, 0, 0, 0), pipeline_mode=pl.Buffered(1)),
                  pl.BlockSpec((tq, D_MODEL), cur(0)),
                  pl.BlockSpec((tq, KV_W), cur(kcol)),
                  pl.BlockSpec((tq, KV_W), cur(vcol)),
                  pl.BlockSpec((w, KV_W), prev(kcol)),
                  pl.BlockSpec((w, KV_W), prev(vcol))],
        out_specs=pl.BlockSpec((tq, D_MODEL), cur(0)),
        out_shape=jax.ShapeDtypeStruct((nb_seq * seq, D_MODEL), F32),
        compiler_params=_cparams("parallel", "parallel"),
        name="attn_prompt",
    )(sinks, bias, qkv, qkv, qkv, qkv, qkv)


def _attn_sample_kernel(sink_ref, q_ref, kn_ref, vn_ref, ck_ref, cv_ref, o_ref, nk_ref, nv_ref, *, t):
    sn = q_ref.shape[0]
    wb = ck_ref.shape[1]
    gsz = N_HEADS // N_KV_HEADS
    rows = gsz * t
    q = q_ref[...]
    kn = kn_ref[...]
    vn = vn_ref[...]
    ck = ck_ref[...]
    cv = cv_ref[...]
    tq1 = lax.broadcasted_iota(jnp.int32, (rows, wb), 0) % t
    j1 = lax.broadcasted_iota(jnp.int32, (rows, wb), 1)
    rel1 = wb + tq1 - j1
    valid1 = (rel1 >= 0) & (rel1 < WINDOW)
    tq2 = lax.broadcasted_iota(jnp.int32, (rows, t), 0) % t
    s2i = lax.broadcasted_iota(jnp.int32, (rows, t), 1)
    rel2 = tq2 - s2i
    valid2 = (rel2 >= 0) & (rel2 < WINDOW)
    hrow = lax.broadcasted_iota(jnp.int32, (rows, 1), 0) // t
    outs = []
    for g in range(N_KV_HEADS):
        slope = jnp.zeros((rows, 1), F32)
        sink = jnp.zeros((rows, 1), F32)
        for hh in range(gsz):
            h = g * gsz + hh
            slope = jnp.where(hrow == hh, _alibi_slope(h), slope)
            sink = jnp.where(hrow == hh, sink_ref[h], sink)
        qg = jnp.concatenate([q[:, :, (g * gsz + hh) * HEAD_DIM:(g * gsz + hh + 1) * HEAD_DIM]
                              for hh in range(gsz)], axis=1).astype(BF16)
        lo, hi = g * HEAD_DIM, (g + 1) * HEAD_DIM
        ckg = ck[:, :, lo:hi].astype(BF16)
        cvg = cv[:, :, lo:hi].astype(BF16)
        kng = kn[:, :, lo:hi].astype(BF16)
        vng = vn[:, :, lo:hi].astype(BF16)
        nt = (((2,), (2,)), ((0,), (0,)))
        nn = (((2,), (1,)), ((0,), (0,)))
        s1 = lax.dot_general(qg, ckg, nt, preferred_element_type=F32) * (HEAD_DIM ** -0.5)
        s2 = lax.dot_general(qg, kng, nt, preferred_element_type=F32) * (HEAD_DIM ** -0.5)
        s1 = jnp.where(valid1[None], s1 - (slope * rel1.astype(F32))[None], -jnp.inf)
        s2 = jnp.where(valid2[None], s2 - (slope * rel2.astype(F32))[None], -jnp.inf)
        m = jnp.maximum(jnp.maximum(jnp.max(s1, axis=-1, keepdims=True),
                                    jnp.max(s2, axis=-1, keepdims=True)), sink[None])
        e1 = jnp.exp(s1 - m)
        e2 = jnp.exp(s2 - m)
        den = (jnp.sum(e1, axis=-1, keepdims=True) + jnp.sum(e2, axis=-1, keepdims=True)
               + jnp.exp(sink[None] - m))
        p1 = (e1 / den).astype(BF16)
        p2 = (e2 / den).astype(BF16)
        og = (lax.dot_general(p1, cvg, nn, preferred_element_type=F32)
              + lax.dot_general(p2, vng, nn, preferred_element_type=F32))
        outs.extend(og[:, hh * t:(hh + 1) * t, :] for hh in range(gsz))
    o_ref[...] = jnp.concatenate(outs, axis=2)
    nk_ref[...] = jnp.concatenate([ck[:, t:, :], kn], axis=1)
    nv_ref[...] = jnp.concatenate([cv[:, t:, :], vn], axis=1)


def _attn_sample(q, kn, vn, ck, cv, sinks):
    n, t, _ = q.shape
    wb = ck.shape[1]
    sn = _tile(n, 16)
    blk = lambda d1, d2: pl.BlockSpec((sn, d1, d2), lambda i: (i, 0, 0))
    return pl.pallas_call(
        functools.partial(_attn_sample_kernel, t=t),
        grid=(n // sn,),
        in_specs=[pl.BlockSpec(memory_space=pltpu.SMEM),
                  blk(t, D_MODEL), blk(t, KV_W), blk(t, KV_W), blk(wb, KV_W), blk(wb, KV_W)],
        out_specs=[blk(t, D_MODEL), blk(wb, KV_W), blk(wb, KV_W)],
        out_shape=[jax.ShapeDtypeStruct((n, t, D_MODEL), F32),
                   jax.ShapeDtypeStruct((n, wb, KV_W), F32),
                   jax.ShapeDtypeStruct((n, wb, KV_W), F32)],
        compiler_params=_cparams("parallel"),
        name="attn_sample",
    )(sinks, q, kn, vn, ck, cv)


def _rglru_gates(xc, wa_ref, ba, wx_ref, bx, lam):
    xb = xc.astype(BF16)
    npair = D_MODEL // LANES
    ga = jnp.concatenate([jnp.dot(xb[:, j * LANES:(j + 1) * LANES], wa_ref[j], preferred_element_type=F32)
                          for j in range(npair)], axis=1)
    gx = jnp.concatenate([jnp.dot(xb[:, j * LANES:(j + 1) * LANES], wx_ref[j], preferred_element_type=F32)
                          for j in range(npair)], axis=1)
    r = _sigmoid(ga + ba)
    ig = _sigmoid(gx + bx)
    log_a = -RG_C * r * _softplus(-lam)
    a = jnp.exp(log_a)
    u = jnp.sqrt(-jnp.tanh(log_a) * (a * a + 1.0)) * (ig * xc)
    return a, u


def _proj_rglru_prompt_kernel(xin_ref, w_ref, cw_ref, cb_ref, wa_ref, ba_ref, wx_ref, bx_ref, lam_ref,
                              qkv_ref, h_ref, tail_out_ref, tail_ref, hc_ref):
    i = pl.program_id(1)
    tc = xin_ref.shape[0]

    @pl.when(i == 0)
    def _():
        tail_ref[...] = jnp.zeros_like(tail_ref)
        hc_ref[...] = jnp.zeros_like(hc_ref)

    xb = xin_ref[...].astype(BF16)
    qkv_ref[...] = jnp.dot(xb, w_ref[:, :OFF_RNN], preferred_element_type=F32)
    x = jnp.dot(xb, w_ref[:, OFF_RNN:], preferred_element_type=F32)
    xp = jnp.concatenate([tail_ref[...], x], axis=0)
    cw = cw_ref[...]
    xc = cb_ref[...] + x * cw[CONV_W - 1:CONV_W]
    for d in range(1, CONV_W):
        xc = xc + xp[8 - d:8 - d + tc] * cw[CONV_W - 1 - d:CONV_W - d]
    a, u = _rglru_gates(xc, wa_ref, ba_ref[...], wx_ref, bx_ref[...], lam_ref[...])
    ng = tc // 8
    acc_a = a.reshape(ng, 8, D_MODEL)
    acc_h = u.reshape(ng, 8, D_MODEL)
    sub = lax.broadcasted_iota(jnp.int32, (1, 8, D_MODEL), 1)
    for d in (1, 2, 4):
        keep = sub >= d
        sh_a = jnp.where(keep, pltpu.roll(acc_a, d, axis=1), 1.0)
        sh_h = jnp.where(keep, pltpu.roll(acc_h, d, axis=1), 0.0)
        acc_h = acc_h + acc_a * sh_h
        acc_a = acc_a * sh_a
    carry = hc_ref[...]
    for g in range(ng):
        hg = acc_h[g] + acc_a[g] * carry
        h_ref[g * 8:(g + 1) * 8, :] = hg
        carry = hg[7:8]
    hc_ref[...] = carry
    tail_ref[...] = x[tc - 8:tc]

    @pl.when(i == pl.num_programs(1) - 1)
    def _():
        tail_out_ref[0] = x[tc - 8:tc]


def _proj_rglru_prompt(x, w, nb_seq, seq, cw, cb, wa2, ba, wx2, bx, lam):
    tc = _tile(seq, 256)
    nt = seq // tc
    row = lambda b, i: (b * nt + i, 0)
    full2 = lambda b, i: (0, 0)
    full3 = lambda b, i: (0, 0, 0)
    return pl.pallas_call(
        _proj_rglru_prompt_kernel,
        grid=(nb_seq, nt),
        in_specs=[pl.BlockSpec((tc, D_MODEL), row),
                  pl.BlockSpec(w.shape, full2, pipeline_mode=pl.Buffered(1)),
                  pl.BlockSpec((CONV_W, D_MODEL), full2), pl.BlockSpec((1, D_MODEL), full2),
                  pl.BlockSpec(wa2.shape, full3), pl.BlockSpec((1, D_MODEL), full2),
                  pl.BlockSpec(wx2.shape, full3), pl.BlockSpec((1, D_MODEL), full2),
                  pl.BlockSpec((1, D_MODEL), full2)],
        out_specs=[pl.BlockSpec((tc, OFF_RNN), row), pl.BlockSpec((tc, D_MODEL), row),
                   pl.BlockSpec((1, 8, D_MODEL), lambda b, i: (b, 0, 0))],
        out_shape=[jax.ShapeDtypeStruct((nb_seq * seq, OFF_RNN), F32),
                   jax.ShapeDtypeStruct((nb_seq * seq, D_MODEL), F32),
                   jax.ShapeDtypeStruct((nb_seq, 8, D_MODEL), F32)],
        scratch_shapes=[pltpu.VMEM((8, D_MODEL), F32), pltpu.VMEM((1, D_MODEL), F32)],
        compiler_params=_cparams("parallel", "arbitrary"),
        name="proj_rglru_prompt",
    )(x, w, cw, cb, wa2, ba, wx2, bx, lam)


def _rglru_sample_kernel(x_ref, cs_ref, h0_ref, cw_ref, cb_ref, wa_ref, ba_ref, wx_ref, bx_ref, lam_ref,
                         h_ref):
    t, sn, _ = x_ref.shape
    cw = cw_ref[...]
    nst = CONV_W - 1

    def slab(s):
        return x_ref[s] if s >= 0 else cs_ref[nst + s]

    xcs = []
    for s in range(t):
        xc = cb_ref[...] + slab(s) * cw[CONV_W - 1:CONV_W]
        for d in range(1, CONV_W):
            xc = xc + slab(s - d) * cw[CONV_W - 1 - d:CONV_W - d]
        xcs.append(xc)
    xc_all = jnp.concatenate(xcs, axis=0)
    a, u = _rglru_gates(xc_all, wa_ref, ba_ref[...], wx_ref, bx_ref[...], lam_ref[...])
    h = h0_ref[...]
    for s in range(t):
        h = a[s * sn:(s + 1) * sn] * h + u[s * sn:(s + 1) * sn]
        h_ref[s] = h


def _rglru_sample(x, cs, h0, cw, cb, wa2, ba, wx2, bx, lam):
    t, n, _ = x.shape
    sn = _tile(n, 32)
    full2 = lambda i: (0, 0)
    full3 = lambda i: (0, 0, 0)
    return pl.pallas_call(
        _rglru_sample_kernel,
        grid=(n // sn,),
        in_specs=[pl.BlockSpec((t, sn, D_MODEL), lambda i: (0, i, 0)),
                  pl.BlockSpec((CONV_W - 1, sn, D_MODEL), lambda i: (0, i, 0)),
                  pl.BlockSpec((sn, D_MODEL), lambda i: (i, 0)),
                  pl.BlockSpec((CONV_W, D_MODEL), full2), pl.BlockSpec((1, D_MODEL), full2),
                  pl.BlockSpec(wa2.shape, full3), pl.BlockSpec((1, D_MODEL), full2),
                  pl.BlockSpec(wx2.shape, full3), pl.BlockSpec((1, D_MODEL), full2),
                  pl.BlockSpec((1, D_MODEL), full2)],
        out_specs=pl.BlockSpec((t, sn, D_MODEL), lambda i: (0, i, 0)),
        out_shape=jax.ShapeDtypeStruct((t, n, D_MODEL), F32),
        compiler_params=_cparams("parallel"),
        name="rglru_sample",
    )(x, cs, h0, cw, cb, wa2, ba, wx2, bx, lam)


def _rwkv_prep_math(pc, prev, mu, w0, a0, kkp, ka, wwa, gup, e, et):
    d = D_MODEL
    ps = pc + mu * (prev - pc)
    r = ps[:, 0:d]
    k = ps[:, d:2 * d]
    v = ps[:, 2 * d:3 * d]
    l01 = ps[:, 3 * d:3 * d + LANES]
    lane = lax.broadcasted_iota(jnp.int32, (1, LANES), 1)
    t01 = jnp.where(lane < LORA_W, jnp.tanh(l01), l01)
    wa = _dot(t01, wwa)
    g = _dot(_sigmoid(ps[:, 3 * d + LANES:]), gup)
    wlog = -_softplus(-(w0 + wa[:, :d])) - 0.5
    logw = -jnp.exp(wlog)
    a = _sigmoid(a0 + wa[:, d:])
    kk = k * kkp
    nrm = jnp.maximum(jnp.sqrt(_head_sum(kk * kk, e, et)), 1e-12)
    kkn = kk / nrm
    kmod = k * (1.0 + (a - 1.0) * ka)
    return r, kmod, v, logw, kkn, kkn * a, g


_P_R, _P_K, _P_V, _P_LW, _P_KK, _P_B, _P_G = range(7)
_N_PREP_OUT = 7


def _plane_spec(plane, rows, index):
    return pl.BlockSpec((None, rows, D_MODEL), lambda *g: (plane, index(*g), 0))


def _rwkv_front_kernel(x_ref, w_ref, sh0_ref, mu_ref, w0_ref, a0_ref, kkp_ref, ka_ref, wwa_ref, gup_ref,
                       e_ref, et_ref, out_ref, sh_out_ref, carry_ref, *, shift_rows):
    i = pl.program_id(1)
    tm = x_ref.shape[0]
    ncar = carry_ref.shape[0]

    @pl.when(i == 0)
    def _():
        carry_ref[...] = sh0_ref[0]

    pc = jnp.dot(x_ref[...].astype(BF16), w_ref[...], preferred_element_type=F32)
    prev = jnp.concatenate([carry_ref[ncar - shift_rows:, :], pc[:tm - shift_rows]], axis=0)
    outs = _rwkv_prep_math(pc, prev, mu_ref[...], w0_ref[...], a0_ref[...], kkp_ref[...], ka_ref[...],
                           wwa_ref[...], gup_ref[...], e_ref[...], et_ref[...])
    for j, o in enumerate(outs):
        out_ref[j] = o
    carry_ref[...] = pc[tm - ncar:]

    @pl.when(i == pl.num_programs(1) - 1)
    def _():
        sh_out_ref[0] = pc[tm - ncar:]


def _rwkv_front(x, w_rw, shift0, n_groups, shift_rows, params):
    rows = x.shape[0] // n_groups
    ncar = shift0.shape[1]
    tm = _tile(rows, 256)
    assert tm >= ncar >= shift_rows and tm % shift_rows == 0
    nt = rows // tm
    row = lambda g, i: (g * nt + i, 0)
    full = lambda g, i: (0, 0)
    grp = pl.BlockSpec((1, ncar, SHIFT_PAD), lambda g, i: (g, 0, 0))
    return pl.pallas_call(
        functools.partial(_rwkv_front_kernel, shift_rows=shift_rows),
        grid=(n_groups, nt),
        in_specs=[pl.BlockSpec((tm, D_MODEL), row),
                  pl.BlockSpec(w_rw.shape, full, pipeline_mode=pl.Buffered(1)), grp]
                 + [pl.BlockSpec(p.shape, full) for p in params],
        out_specs=[pl.BlockSpec((_N_PREP_OUT, tm, D_MODEL), lambda g, i: (0, g * nt + i, 0)), grp],
        out_shape=[jax.ShapeDtypeStruct((_N_PREP_OUT, n_groups * rows, D_MODEL), F32),
                   jax.ShapeDtypeStruct((n_groups, ncar, SHIFT_PAD), F32)],
        scratch_shapes=[pltpu.VMEM((ncar, SHIFT_PAD), F32)],
        compiler_params=_cparams("parallel", "arbitrary"),
        name="rwkv_front",
    )(x, w_rw, shift0, *params)


_PAIR_W = 2 * HEAD_DIM
_N_PAIRS = D_MODEL // _PAIR_W
_WKV_STATE_PASSES = 1


def _dot_tn(a, b):
    return lax.dot_general(a.astype(BF16), b.astype(BF16), (((0,), (0,)), ((), ())),
                           preferred_element_type=F32)


def _dot_nt(a, b):
    return lax.dot_general(a.astype(BF16), b.astype(BF16), (((1,), (1,)), ((), ())),
                           preferred_element_type=F32)


def _dot_passes(a, b, passes):
    if passes == 1:
        return _dot(a, b)
    ah, al = _split_terms(a, 2)
    bh, bl = _split_terms(b, 2)
    return (jnp.dot(ah, bh, preferred_element_type=F32) + jnp.dot(ah, bl, preferred_element_type=F32)
            + jnp.dot(al, bh, preferred_element_type=F32))


def _wkv_prompt_kernel(r_ref, k_ref, v_ref, lw_ref, kk_ref, b_ref, y_ref, s_ref, h_ref):
    i = pl.program_id(1)
    rows = r_ref.shape[0]
    c = WKV_CHUNK
    nck = rows // c
    hd = HEAD_DIM
    pw = _PAIR_W

    @pl.when(i == 0)
    def _():
        h_ref[...] = jnp.zeros_like(h_ref)

    row_t = lax.broadcasted_iota(jnp.int32, (rows, rows), 0)
    col_t = lax.broadcasted_iota(jnp.int32, (rows, rows), 1)
    tri = (row_t >= col_t) & (row_t // c == col_t // c)
    lw = lw_ref[...]
    cum = _dot_rhs_split(jnp.where(tri, 1.0, 0.0).astype(BF16), lw, 3)
    last = jnp.concatenate([jnp.broadcast_to(cum[(ci + 1) * c - 1:(ci + 1) * c], (c, D_MODEL))
                            for ci in range(nck)], axis=0)
    g_inv = jnp.exp(-cum)
    g_end = jnp.exp(last - cum)
    kk = kk_ref[...]
    b = b_ref[...]
    k = k_ref[...]
    at_all = -(kk * jnp.exp(cum - lw))
    rt_all = r_ref[...] * jnp.exp(cum)
    bt_all = b * g_inv
    kt_all = k * g_inv
    bh_all = b * g_end
    kh_all = k * g_end
    gc_all = jnp.exp(last)

    rowp = lax.broadcasted_iota(jnp.int32, (c, pw), 0)
    colp = lax.broadcasted_iota(jnp.int32, (c, pw), 1) % hd
    strict = rowp > colp
    incl = rowp >= colp
    eye_p = jnp.where(rowp == colp, 1.0, 0.0)
    left = lax.broadcasted_iota(jnp.int32, (1, pw), 1) < hd
    r2 = lax.broadcasted_iota(jnp.int32, (pw, pw), 0)
    c2 = lax.broadcasted_iota(jnp.int32, (pw, pw), 1)
    same_head = (r2 < hd) == (c2 < hd)
    eye2 = r2 == c2

    def bd(w):
        return jnp.concatenate([jnp.where(left, w, 0.0), jnp.where(left, 0.0, w)], axis=0)

    pairs = range(_N_PAIRS)
    items = [(ci, p) for ci in range(nck) for p in pairs]
    idx = range(len(items))
    sub = [(slice(ci * c, (ci + 1) * c), slice(p * pw, (p + 1) * pw)) for ci, p in items]
    v_all = v_ref[...]
    at = [at_all[s] for s in sub]
    rt = [rt_all[s] for s in sub]
    v = [v_all[s] for s in sub]
    amat = [_dot_nt(jnp.concatenate([at[j], rt[j]], axis=0),
                    jnp.concatenate([bd(bt_all[sub[j]]), bd(kt_all[sub[j]])], axis=0))
            for j in idx]
    a_ab = [jnp.where(strict, a[:c, :pw], 0.0) for a in amat]
    a_ak = [jnp.where(strict, a[:c, pw:], 0.0) for a in amat]
    a_rb = [jnp.where(incl, a[c:, :pw], 0.0) for a in amat]
    a_rk = [jnp.where(incl, a[c:, pw:], 0.0) for a in amat]
    tinv = [eye_p + a for a in a_ab]
    npow = [_dot(a, bd(a)) for a in a_ab]
    span = 4
    while span < c:
        both = [_dot(jnp.concatenate([tinv[j], npow[j]], axis=0), bd(npow[j])) for j in idx]
        tinv = [tinv[j] + both[j][:c] for j in idx]
        npow = [x[c:] for x in both]
        span *= 2
    tinv = [tinv[j] + _dot(tinv[j], bd(npow[j])) for j in idx]
    w12 = [_dot(jnp.concatenate([a_ak[j], a_rk[j]], axis=0), bd(v[j])) for j in idx]
    pq = [_dot(tinv[j], jnp.concatenate([bd(at[j]), bd(w12[j][:c])], axis=1)) for j in idx]
    y10 = [jnp.concatenate([rt[j], w12[j][c:]], axis=1)
           + _dot(a_rb[j], jnp.concatenate([bd(pq[j][:, :pw]), bd(pq[j][:, pw:])], axis=1))
           for j in idx]
    mn = [_dot_tn(bh_all[sub[j]], pq[j]) for j in idx]
    kv = [_dot_tn(kh_all[sub[j]], v[j]) for j in idx]
    m = [jnp.where(same_head, mn[j][:, :pw], 0.0)
         + jnp.where(eye2, gc_all[sub[j][0], sub[j][1]][:1], 0.0) for j in idx]
    n = [jnp.where(same_head, mn[j][:, pw:] + kv[j], 0.0) for j in idx]
    h = [h_ref[p] for p in pairs]
    y_rows = []
    for ci in range(nck):
        js = [ci * _N_PAIRS + p for p in pairs]
        my = [_dot_passes(jnp.concatenate([m[j], y10[j][:, :pw]], axis=0), h[p], _WKV_STATE_PASSES)
              for p, j in zip(pairs, js)]
        h = [my[p][:pw] + n[j] for p, j in zip(pairs, js)]
        y_rows.append(jnp.concatenate([my[p][pw:] + y10[j][:, pw:] for p, j in zip(pairs, js)], axis=1))
    for p in pairs:
        h_ref[p] = h[p]
    y_ref[...] = jnp.concatenate(y_rows, axis=0)

    @pl.when(i == pl.num_programs(1) - 1)
    def _():
        for p in pairs:
            s_ref[0, p] = h[p]


def _wkv_prompt(prep, nb_seq, seq):
    c = WKV_CHUNK * WKV_CHUNKS_PER_TILE
    assert WKV_CHUNK == HEAD_DIM and seq % c == 0
    nc = seq // c
    row = lambda bb, i: (bb * nc + i, 0)
    planes = (_P_R, _P_K, _P_V, _P_LW, _P_KK, _P_B)
    return pl.pallas_call(
        _wkv_prompt_kernel,
        grid=(nb_seq, nc),
        in_specs=[_plane_spec(p, c, lambda bb, i: bb * nc + i) for p in planes],
        out_specs=[pl.BlockSpec((c, D_MODEL), row),
                   pl.BlockSpec((1, _N_PAIRS, _PAIR_W, _PAIR_W), lambda bb, i: (bb, 0, 0, 0))],
        out_shape=[jax.ShapeDtypeStruct((nb_seq * seq, D_MODEL), F32),
                   jax.ShapeDtypeStruct((nb_seq, _N_PAIRS, _PAIR_W, _PAIR_W), F32)],
        scratch_shapes=[pltpu.VMEM((_N_PAIRS, _PAIR_W, _PAIR_W), F32)],
        compiler_params=_cparams("parallel", "arbitrary"),
        name="wkv_prompt",
    )(*([prep] * len(planes)))


def _unpair_state(hb):
    n = hb.shape[0]
    h6 = hb.reshape(n, _N_PAIRS, 2, HEAD_DIM, 2, HEAD_DIM)
    blocks = jnp.stack([h6[:, :, 0, :, 0, :], h6[:, :, 1, :, 1, :]], axis=2)
    return jnp.swapaxes(blocks, -1, -2).reshape(n, N_HEADS, HEAD_DIM, HEAD_DIM)


_WKV_VROWS = 4


def _wkv_sample_kernel(r_ref, lw_ref, k_ref, v_ref, kk_ref, b_ref, s0_ref, y_ref, s_ref, w_ref):
    t = r_ref.shape[0]
    vr = _WKV_VROWS
    w_ref[...] = jnp.exp(lw_ref[...])
    for v0 in range(0, HEAD_DIM, vr):
        s = [s0_ref[0, v0 + j] for j in range(vr)]
        for st in range(t):
            kk, w, b, k, r = kk_ref[st], w_ref[st], b_ref[st], k_ref[st], r_ref[st]
            vrow = v_ref[st, v0:v0 + vr, :]
            ys = []
            for j in range(vr):
                sa = -jnp.sum(s[j] * kk, axis=0, keepdims=True)
                s[j] = s[j] * w + sa * b + vrow[j:j + 1] * k
                ys.append(jnp.sum(s[j] * r, axis=0, keepdims=True))
            y_ref[st, v0:v0 + vr, :] = jnp.concatenate(ys, axis=0)
        for j in range(vr):
            s_ref[0, v0 + j] = s[j]


def _wkv_sample(vecs, s0):
    _, t, nh, hd, n = vecs.shape
    vec = lambda p: pl.BlockSpec((None, t, None, hd, n), lambda h: (p, 0, h, 0, 0))
    st = pl.BlockSpec((1, hd, hd, n), lambda h: (h, 0, 0, 0))
    planes = (_P_R, _P_LW, _P_K, _P_V, _P_KK, _P_B)
    return pl.pallas_call(
        _wkv_sample_kernel,
        grid=(nh,),
        in_specs=[vec(p) for p in planes] + [st],
        out_specs=[pl.BlockSpec((t, None, hd, n), lambda h: (0, h, 0, 0)), st],
        out_shape=[jax.ShapeDtypeStruct((t, nh, hd, n), F32),
                   jax.ShapeDtypeStruct((nh, hd, hd, n), F32)],
        scratch_shapes=[pltpu.VMEM((t, hd, n), F32)],
        compiler_params=_cparams("parallel"),
        name="wkv_sample",
    )(*([vecs] * len(planes)), s0)


def _mix_out_kernel(x_ref, oa_ref, ob_ref, y_ref, r_ref, k_ref, v_ref, g_ref, wgate_ref,
                    lnw_ref, lnb_ref, rk_ref, wout_ref, g1_ref, b1_ref, e_ref, et_ref, o_ref, *, alpha):
    d = D_MODEL
    x = x_ref[...]
    gate = jnp.dot(x.astype(BF16), wgate_ref[...], preferred_element_type=F32)
    e, et = e_ref[...], et_ref[...]
    y = y_ref[...]
    mu = _head_sum(y, e, et) * (1.0 / HEAD_DIM)
    yc = y - mu
    var = _head_sum(yc * yc, e, et) * (1.0 / HEAD_DIM)
    yn = yc * lax.rsqrt(var + GN_EPS) * lnw_ref[...] + lnb_ref[...]
    v = v_ref[...]
    bonus = _head_sum(r_ref[...] * k_ref[...] * rk_ref[...], e, et) * v
    oc = (yn + bonus) * g_ref[...]
    mixed = (_sigmoid(gate[:, 0:d]) * oa_ref[...] + _sigmoid(gate[:, d:2 * d]) * ob_ref[...]
             + _sigmoid(gate[:, 2 * d:3 * d]) * oc)
    z = alpha * x + jnp.dot(mixed.astype(BF16), wout_ref[...], preferred_element_type=F32)
    o_ref[...] = _layer_norm(z, g1_ref[...], b1_ref[...])


def _mix_out(x, oa, ob, y, prep, wgate, lnw, lnb, rk, wout, g1, b1, e, et, alpha):
    m = x.shape[0]
    tm = _tile(m, 256)
    row = lambda i: (i, 0)
    full = lambda i: (0, 0)
    rows = pl.BlockSpec((tm, D_MODEL), row)
    vec = pl.BlockSpec((1, D_MODEL), full)
    resident = lambda w: pl.BlockSpec(w.shape, full, pipeline_mode=pl.Buffered(1))
    planes = (_P_R, _P_K, _P_V, _P_G)
    return pl.pallas_call(
        functools.partial(_mix_out_kernel, alpha=alpha),
        grid=(m // tm,),
        in_specs=[rows] * 4 + [_plane_spec(p, tm, lambda i: i) for p in planes]
                 + [resident(wgate), vec, vec, vec, resident(wout), vec, vec,
                    pl.BlockSpec(e.shape, full), pl.BlockSpec(et.shape, full)],
        out_specs=rows,
        out_shape=jax.ShapeDtypeStruct((m, D_MODEL), F32),
        compiler_params=_cparams("parallel"),
        name="mix_out",
    )(x, oa, ob, y, *([prep] * len(planes)), wgate, lnw, lnb, rk, wout, g1, b1, e, et)


def _ffn_kernel(x_ref, wgu_ref, wd_ref, g2_ref, b2_ref, o_ref, *, alpha):
    dff = wd_ref.shape[0]
    x = x_ref[...]
    xb = x.astype(BF16)
    gt = jnp.dot(xb, wgu_ref[:, :dff], preferred_element_type=F32)
    up = jnp.dot(xb, wgu_ref[:, dff:], preferred_element_type=F32)
    ff = jnp.dot((gt * _sigmoid(gt) * up).astype(BF16), wd_ref[...], preferred_element_type=F32)
    o_ref[...] = _layer_norm(alpha * x + ff, g2_ref[...], b2_ref[...])


def _ffn(x, wgu, wd, g2, b2, alpha):
    m = x.shape[0]
    tm = _tile(m, 256)
    row = lambda i: (i, 0)
    full = lambda i: (0, 0)
    resident = lambda w: pl.BlockSpec(w.shape, full, pipeline_mode=pl.Buffered(1))
    return pl.pallas_call(
        functools.partial(_ffn_kernel, alpha=alpha),
        grid=(m // tm,),
        in_specs=[pl.BlockSpec((tm, D_MODEL), row), resident(wgu), resident(wd),
                  pl.BlockSpec((1, D_MODEL), full), pl.BlockSpec((1, D_MODEL), full)],
        out_specs=pl.BlockSpec((tm, D_MODEL), row),
        out_shape=jax.ShapeDtypeStruct((m, D_MODEL), F32),
        compiler_params=_cparams("parallel"),
        name="ffn",
    )(x, wgu, wd, g2, b2)


def _blockdiag_pairs(w):
    nb, bs, _ = w.shape
    w = w.reshape(nb // 2, 2, bs, bs)
    z = jnp.zeros((nb // 2, bs, bs), w.dtype)
    top = jnp.concatenate([w[:, 0], z], axis=2)
    bot = jnp.concatenate([z, w[:, 1]], axis=2)
    return jnp.concatenate([top, bot], axis=1)


def _row(p):
    return p.reshape(1, -1)


def kernel(x_prompt, x_sample, cache_k, cache_v, state_conv, state_rglru, state_shift, state_wkv, w_in, attn_sinks, conv_w, conv_b, rg_wa, rg_ba, rg_wx, rg_bx, rg_lambda, rw_mu, rw_w0, rw_wup, rw_a0, rw_aup, rw_gup, rw_kk, rw_ka, rw_rk, rw_lnw, rw_lnb, w_out, ln1_g, ln1_b, w_gu, w_down, ln2_g, ln2_b):
    depth = w_in.shape[0]
    bp, seq, d = x_prompt.shape
    ns, ts, _ = x_sample.shape
    wb = cache_k.shape[2]
    alpha = (2 * depth) ** 0.25
    dff = w_down.shape[1]
    pad = SHIFT_PAD - SHIFT_W

    xp = x_prompt.reshape(bp * seq, d)
    xs = jnp.swapaxes(x_sample, 0, 1).reshape(ts * ns, d)
    st_p = [[] for _ in range(6)]
    st_s = [[] for _ in range(6)]
    head_e, head_et = _head_indicator()
    w_in_bf = w_in.astype(BF16)
    w_gu_bf = w_gu.astype(BF16)
    w_down_bf = w_down.astype(BF16)

    for l in range(depth):
        wl = w_in_bf[l]
        w_all = wl[:, :OFF_RW]
        w_rw = jnp.pad(wl[:, OFF_RW:OFF_GATE], ((0, 0), (0, pad)))
        w_gate = wl[:, OFF_GATE:]
        wa2 = _blockdiag_pairs(rg_wa[l]).astype(BF16)
        wx2 = _blockdiag_pairs(rg_wx[l]).astype(BF16)
        rg_args = (conv_w[l], _row(conv_b[l]), wa2, _row(rg_ba[l]), wx2, _row(rg_bx[l]), _row(rg_lambda[l]))
        zw = jnp.zeros((LORA_W, d), F32)
        wwa = jnp.concatenate([jnp.concatenate([rw_wup[l], zw], axis=1),
                               jnp.concatenate([zw, rw_aup[l]], axis=1)], axis=0).astype(BF16)
        gup = jnp.pad(rw_gup[l], ((0, 2 * LANES - LORA_G), (0, 0))).astype(BF16)
        prep_params = (_row(jnp.pad(rw_mu[l], (0, pad))), _row(rw_w0[l]), _row(rw_a0[l]), _row(rw_kk[l]),
                       _row(rw_ka[l]), wwa, gup, head_e, head_et)
        mix_params = (_row(rw_lnw[l]), _row(rw_lnb[l]), _row(rw_rk[l]), w_out[l].astype(BF16),
                      _row(ln1_g[l]), _row(ln1_b[l]), head_e, head_et)
        ffn_params = (w_gu_bf[l], w_down_bf[l], _row(ln2_g[l]), _row(ln2_b[l]))
        sinks = attn_sinks[l]

        qkv, ob, rnn_tail = _proj_rglru_prompt(xp, w_all, bp, seq, *rg_args)
        oa = _attn_prompt(qkv, sinks, bp, seq)
        prep, sh_p = _rwkv_front(xp, w_rw, jnp.zeros((bp, 8, SHIFT_PAD), F32), bp, 1, prep_params)
        y, hfin = _wkv_prompt(prep, bp, seq)
        x1 = _mix_out(xp, oa, ob, y, prep, w_gate, *mix_params, alpha)
        xp = _ffn(x1, *ffn_params, alpha)
        qkv3 = qkv.reshape(bp, seq, OFF_RNN)
        st_p[0].append(qkv3[:, seq - wb:, OFF_K:OFF_V].reshape(bp, wb, N_KV_HEADS, HEAD_DIM))
        st_p[1].append(qkv3[:, seq - wb:, OFF_V:OFF_RNN].reshape(bp, wb, N_KV_HEADS, HEAD_DIM))
        st_p[2].append(rnn_tail[:, 8 - (CONV_W - 1):])
        st_p[3].append(ob.reshape(bp, seq, d)[:, seq - 1])
        st_p[4].append(sh_p[:, 7, :SHIFT_W])
        st_p[5].append(_unpair_state(hfin))

        qkv, rnn = _in_proj(xs, w_all)
        qkv_n = jnp.swapaxes(qkv.reshape(ts, ns, OFF_RNN), 0, 1)
        oa_n, new_k, new_v = _attn_sample(qkv_n[:, :, :OFF_K], qkv_n[:, :, OFF_K:OFF_V], qkv_n[:, :, OFF_V:],
                                          cache_k[l].reshape(ns, wb, KV_W), cache_v[l].reshape(ns, wb, KV_W),
                                          sinks)
        oa = jnp.swapaxes(oa_n, 0, 1).reshape(ts * ns, d)
        rnn3 = rnn.reshape(ts, ns, d)
        ob3 = _rglru_sample(rnn3, jnp.swapaxes(state_conv[l], 0, 1), state_rglru[l], *rg_args)
        ob = ob3.reshape(ts * ns, d)
        shift = jnp.pad(state_shift[l], ((0, 0), (0, pad)))[None]
        prep, sh_s = _rwkv_front(xs, w_rw, shift, 1, ns, prep_params)
        vecs = jnp.swapaxes(prep.reshape(_N_PREP_OUT, ts, ns, d), 2, 3).reshape(
            _N_PREP_OUT, ts, N_HEADS, HEAD_DIM, ns)
        s0 = jnp.transpose(state_wkv[l], (1, 2, 3, 0))
        y4, s_new = _wkv_sample(vecs, s0)
        y = jnp.swapaxes(y4.reshape(ts, d, ns), 1, 2).reshape(ts * ns, d)
        x1 = _mix_out(xs, oa, ob, y, prep, w_gate, *mix_params, alpha)
        xs = _ffn(x1, *ffn_params, alpha)
        st_s[0].append(new_k.reshape(ns, wb, N_KV_HEADS, HEAD_DIM))
        st_s[1].append(new_v.reshape(ns, wb, N_KV_HEADS, HEAD_DIM))
        st_s[2].append(jnp.swapaxes(rnn3[ts - (CONV_W - 1):], 0, 1))
        st_s[3].append(ob3[ts - 1])
        st_s[4].append(sh_s[0, :, :SHIFT_W])
        st_s[5].append(jnp.transpose(s_new, (3, 0, 1, 2)))

    yp = xp.reshape(bp, seq, d)
    ys = jnp.swapaxes(xs.reshape(ts, ns, d), 0, 1)
    return (yp, ys, *(jnp.stack(s) for s in st_p), *(jnp.stack(s) for s in st_s))
```

```python
import functools
import math

import jax
import jax.numpy as jnp
from jax import lax
from jax.experimental import pallas as pl
from jax.experimental.pallas import tpu as pltpu

F32 = jnp.float32
BF16 = jnp.bfloat16

D_MODEL = 1024
HEAD_DIM = 64
N_HEADS = 16
N_KV_HEADS = 4
KV_W = N_KV_HEADS * HEAD_DIM
WINDOW = 128
CONV_W = 4
RG_C = 8.0
LORA_W = 64
LORA_A = 64
LORA_G = 160
SHIFT_W = 3 * D_MODEL + LORA_W + LORA_A + LORA_G
LANES = 128
SHIFT_PAD = -(-SHIFT_W // LANES) * LANES
GN_EPS = 64e-5
LN_EPS = 1e-5
OFF_K = D_MODEL
OFF_V = OFF_K + KV_W
OFF_RNN = OFF_V + KV_W
OFF_RW = OFF_RNN + D_MODEL
OFF_GATE = OFF_RW + SHIFT_W
WKV_CHUNK = 64
WKV_CHUNKS_PER_TILE = 4
VMEM_LIMIT = 48 * 1024 * 1024


def _cparams(*sem):
    return pltpu.CompilerParams(dimension_semantics=sem, vmem_limit_bytes=VMEM_LIMIT)


def _tile(n, pref):
    if n <= pref:
        return n
    for t in range(pref, 7, -1):
        if n % t == 0 and t % 8 == 0:
            return t
    return n


def _dot(a, b):
    return jnp.dot(a.astype(BF16), b.astype(BF16), preferred_element_type=F32)


def _split_terms(x, n):
    terms = []
    rem = x
    for _ in range(n):
        hi = rem.astype(BF16)
        terms.append(hi)
        rem = rem - hi.astype(F32)
    return terms


def _dot_lhs_split(x, w_bf16, n):
    acc = None
    for t in _split_terms(x, n):
        p = jnp.dot(t, w_bf16, preferred_element_type=F32)
        acc = p if acc is None else acc + p
    return acc


def _dot_rhs_split(w_bf16, x, n):
    acc = None
    for t in _split_terms(x, n):
        p = jnp.dot(w_bf16, t, preferred_element_type=F32)
        acc = p if acc is None else acc + p
    return acc


def _sigmoid(x):
    return 0.5 * jnp.tanh(0.5 * x) + 0.5


def _softplus(x):
    return jnp.maximum(x, 0.0) + jnp.log(1.0 + jnp.exp(-jnp.abs(x)))


def _layer_norm(z, g, b):
    mu = jnp.mean(z, axis=-1, keepdims=True)
    zc = z - mu
    var = jnp.mean(zc * zc, axis=-1, keepdims=True)
    return zc * lax.rsqrt(var + LN_EPS) * g + b


def _head_indicator():
    c = lax.broadcasted_iota(jnp.int32, (D_MODEL, LANES), 0) // HEAD_DIM
    h = lax.broadcasted_iota(jnp.int32, (D_MODEL, LANES), 1)
    e = jnp.where(c == h, 1.0, 0.0).astype(BF16)
    ct = lax.broadcasted_iota(jnp.int32, (LANES, D_MODEL), 1) // HEAD_DIM
    ht = lax.broadcasted_iota(jnp.int32, (LANES, D_MODEL), 0)
    et = jnp.where(ct == ht, 1.0, 0.0).astype(BF16)
    return e, et


def _head_sum(x, e, et):
    s = jnp.dot(x.astype(BF16), e, preferred_element_type=F32)
    return _dot_lhs_split(s, et, 2)


_PROJ_WIDTHS = (OFF_RNN, D_MODEL)


def _in_proj_kernel(x_ref, w_ref, *o_refs):
    xb = x_ref[...].astype(BF16)
    off = 0
    for o_ref in o_refs:
        n = o_ref.shape[1]
        o_ref[...] = jnp.dot(xb, w_ref[:, off:off + n], preferred_element_type=F32)
        off += n


def _in_proj(x, w):
    m, k = x.shape
    tm = _tile(m, 256)
    return pl.pallas_call(
        _in_proj_kernel,
        grid=(m // tm,),
        in_specs=[pl.BlockSpec((tm, k), lambda i: (i, 0)),
                  pl.BlockSpec(w.shape, lambda i: (0, 0), pipeline_mode=pl.Buffered(1))],
        out_specs=[pl.BlockSpec((tm, n), lambda i: (i, 0)) for n in _PROJ_WIDTHS],
        out_shape=[jax.ShapeDtypeStruct((m, n), F32) for n in _PROJ_WIDTHS],
        compiler_params=_cparams("parallel"),
        name="in_proj",
    )(x, w)


_LOG2E = math.log2(math.e)


def _alibi_slope(h):
    return 2.0 ** (-8.0 * (h + 1) / N_HEADS)


def _attn_prompt_bias():
    w = WINDOW
    rel = w + jnp.arange(w)[:, None] - jnp.arange(2 * w)[None, :]
    band = (rel >= 0) & (rel < WINDOW)
    has_prev = jnp.arange(2)[:, None, None] > 0
    valid = band[None] & (has_prev | (jnp.arange(2 * w) >= w)[None, None, :])
    slopes = jnp.array([_alibi_slope(h) for h in range(N_HEADS)], F32)
    bias = -slopes[None, :, None, None] * rel.astype(F32)[None, None]
    return jnp.where(valid[:, None], bias * _LOG2E, -jnp.inf)


_ATTN_BLOCKS_PER_STEP = 2


def _attn_prompt_kernel(sink_ref, bias_ref, q_ref, kc_ref, vc_ref, kp_ref, vp_ref, o_ref):
    j = pl.program_id(1)
    w = WINDOW
    nblk = q_ref.shape[0] // w
    q = q_ref[...] * (HEAD_DIM ** -0.5 * _LOG2E)
    kall = jnp.concatenate([kp_ref[...], kc_ref[...]], axis=0).astype(BF16)
    vall = jnp.concatenate([vp_ref[...], vc_ref[...]], axis=0).astype(BF16)
    first = jnp.minimum(j, 1)
    gsz = N_HEADS // N_KV_HEADS
    items = [(n, h) for n in range(nblk) for h in range(N_HEADS)]
    kg = {(n, g): kall[n * w:(n + 2) * w, g * HEAD_DIM:(g + 1) * HEAD_DIM]
          for n in range(nblk) for g in range(N_KV_HEADS)}
    vg = {(n, g): vall[n * w:(n + 2) * w, g * HEAD_DIM:(g + 1) * HEAD_DIM]
          for n in range(nblk) for g in range(N_KV_HEADS)}
    qh = [q[n * w:(n + 1) * w, h * HEAD_DIM:(h + 1) * HEAD_DIM].astype(BF16) for n, h in items]
    bias = [bias_ref[first, h] if n == 0 else bias_ref[1, h] for n, h in items]
    s = [lax.dot_general(qh[i], kg[(n, h // gsz)], (((1,), (1,)), ((), ())), preferred_element_type=F32)
         + bias[i] for i, (n, h) in enumerate(items)]
    sink = [sink_ref[h] * _LOG2E for n, h in items]
    m = [jnp.maximum(jnp.max(s[i], axis=-1, keepdims=True), sink[i]) for i in range(len(items))]
    e = [jnp.exp2(s[i] - m[i]) for i in range(len(items))]
    den = [jnp.sum(e[i], axis=-1, keepdims=True) + jnp.exp2(sink[i] - m[i]) for i in range(len(items))]
    p = [(e[i] / den[i]).astype(BF16) for i in range(len(items))]
    outs = [jnp.dot(p[i], vg[(n, h // gsz)], preferred_element_type=F32) for i, (n, h) in enumerate(items)]
    o_ref[...] = jnp.concatenate(
        [jnp.concatenate(outs[n * N_HEADS:(n + 1) * N_HEADS], axis=1) for n in range(nblk)], axis=0)


def _attn_prompt(qkv, sinks, nb_seq, seq):
    w = WINDOW
    nblk = _ATTN_BLOCKS_PER_STEP if seq % (_ATTN_BLOCKS_PER_STEP * w) == 0 else 1
    tq = nblk * w
    nb = seq // tq
    kcol = OFF_K // KV_W
    vcol = OFF_V // KV_W
    bias = _attn_prompt_bias()

    def cur(c):
        return lambda b, j: (b * nb + j, c)

    def prev(c):
        return lambda b, j: (jnp.maximum((b * nb + j) * nblk - 1, 0), c)

    return pl.pallas_call(
        _attn_prompt_kernel,
        grid=(nb_seq, nb),
        in_specs=[pl.BlockSpec(memory_space=pltpu.SMEM),
                  pl.BlockSpec(bias.shape, lambda b, j: (0, 0, 0, 0), pipeline_mode=pl.Buffered(1)),
                  pl.BlockSpec((tq, D_MODEL), cur(0)),
                  pl.BlockSpec((tq, KV_W), cur(kcol)),
                  pl.BlockSpec((tq, KV_W), cur(vcol)),
                  pl.BlockSpec((w, KV_W), prev(kcol)),
                  pl.BlockSpec((w, KV_W), prev(vcol))],
        out_specs=pl.BlockSpec((tq, D_MODEL), cur(0)),
        out_shape=jax.ShapeDtypeStruct((nb_seq * seq, D_MODEL), F32),
        compiler_params=_cparams("parallel", "parallel"),
        name="attn_prompt",
    )(sinks, bias, qkv, qkv, qkv, qkv, qkv)


def _attn_sample_kernel(sink_ref, q_ref, kn_ref, vn_ref, ck_ref, cv_ref, o_ref, nk_ref, nv_ref, *, t):
    sn = q_ref.shape[0]
    wb = ck_ref.shape[1]
    gsz = N_HEADS // N_KV_HEADS
    rows = gsz * t
    q = q_ref[...]
    kn = kn_ref[...]
    vn = vn_ref[...]
    ck = ck_ref[...]
    cv = cv_ref[...]
    tq1 = lax.broadcasted_iota(jnp.int32, (rows, wb), 0) % t
    j1 = lax.broadcasted_iota(jnp.int32, (rows, wb), 1)
    rel1 = wb + tq1 - j1
    valid1 = (rel1 >= 0) & (rel1 < WINDOW)
    tq2 = lax.broadcasted_iota(jnp.int32, (rows, t), 0) % t
    s2i = lax.broadcasted_iota(jnp.int32, (rows, t), 1)
    rel2 = tq2 - s2i
    valid2 = (rel2 >= 0) & (rel2 < WINDOW)
    hrow = lax.broadcasted_iota(jnp.int32, (rows, 1), 0) // t
    outs = []
    for g in range(N_KV_HEADS):
        slope = jnp.zeros((rows, 1), F32)
        sink = jnp.zeros((rows, 1), F32)
        for hh in range(gsz):
            h = g * gsz + hh
            slope = jnp.where(hrow == hh, _alibi_slope(h), slope)
            sink = jnp.where(hrow == hh, sink_ref[h], sink)
        qg = jnp.concatenate([q[:, :, (g * gsz + hh) * HEAD_DIM:(g * gsz + hh + 1) * HEAD_DIM]
                              for hh in range(gsz)], axis=1).astype(BF16)
        lo, hi = g * HEAD_DIM, (g + 1) * HEAD_DIM
        ckg = ck[:, :, lo:hi].astype(BF16)
        cvg = cv[:, :, lo:hi].astype(BF16)
        kng = kn[:, :, lo:hi].astype(BF16)
        vng = vn[:, :, lo:hi].astype(BF16)
        nt = (((2,), (2,)), ((0,), (0,)))
        nn = (((2,), (1,)), ((0,), (0,)))
        s1 = lax.dot_general(qg, ckg, nt, preferred_element_type=F32) * (HEAD_DIM ** -0.5)
        s2 = lax.dot_general(qg, kng, nt, preferred_element_type=F32) * (HEAD_DIM ** -0.5)
        s1 = jnp.where(valid1[None], s1 - (slope * rel1.astype(F32))[None], -jnp.inf)
        s2 = jnp.where(valid2[None], s2 - (slope * rel2.astype(F32))[None], -jnp.inf)
        m = jnp.maximum(jnp.maximum(jnp.max(s1, axis=-1, keepdims=True),
                                    jnp.max(s2, axis=-1, keepdims=True)), sink[None])
        e1 = jnp.exp(s1 - m)
        e2 = jnp.exp(s2 - m)
        den = (jnp.sum(e1, axis=-1, keepdims=True) + jnp.sum(e2, axis=-1, keepdims=True)
               + jnp.exp(sink[None] - m))
        p1 = (e1 / den).astype(BF16)
        p2 = (e2 / den).astype(BF16)
        og = (lax.dot_general(p1, cvg, nn, preferred_element_type=F32)
              + lax.dot_general(p2, vng, nn, preferred_element_type=F32))
        outs.extend(og[:, hh * t:(hh + 1) * t, :] for hh in range(gsz))
    o_ref[...] = jnp.concatenate(outs, axis=2)
    nk_ref[...] = jnp.concatenate([ck[:, t:, :], kn], axis=1)
    nv_ref[...] = jnp.concatenate([cv[:, t:, :], vn], axis=1)


def _attn_sample(q, kn, vn, ck, cv, sinks):
    n, t, _ = q.shape
    wb = ck.shape[1]
    sn = _tile(n, 16)
    blk = lambda d1, d2: pl.BlockSpec((sn, d1, d2), lambda i: (i, 0, 0))
    return pl.pallas_call(
        functools.partial(_attn_sample_kernel, t=t),
        grid=(n // sn,),
        in_specs=[pl.BlockSpec(memory_space=pltpu.SMEM),
                  blk(t, D_MODEL), blk(t, KV_W), blk(t, KV_W), blk(wb, KV_W), blk(wb, KV_W)],
        out_specs=[blk(t, D_MODEL), blk(wb, KV_W), blk(wb, KV_W)],
        out_shape=[jax.ShapeDtypeStruct((n, t, D_MODEL), F32),
                   jax.ShapeDtypeStruct((n, wb, KV_W), F32),
                   jax.ShapeDtypeStruct((n, wb, KV_W), F32)],
        compiler_params=_cparams("parallel"),
        name="attn_sample",
    )(sinks, q, kn, vn, ck, cv)


def _rglru_gates(xc, wa_ref, ba, wx_ref, bx, lam):
    xb = xc.astype(BF16)
    npair = D_MODEL // LANES
    ga = jnp.concatenate([jnp.dot(xb[:, j * LANES:(j + 1) * LANES], wa_ref[j], preferred_element_type=F32)
                          for j in range(npair)], axis=1)
    gx = jnp.concatenate([jnp.dot(xb[:, j * LANES:(j + 1) * LANES], wx_ref[j], preferred_element_type=F32)
                          for j in range(npair)], axis=1)
    r = _sigmoid(ga + ba)
    ig = _sigmoid(gx + bx)
    z = r * (RG_C * _softplus(-lam))
    a = jnp.exp(-z)
    u = jnp.sqrt(jnp.tanh(z) * (a * a + 1.0)) * (ig * xc)
    return a, u


def _proj_rglru_prompt_kernel(xin_ref, w_ref, cw_ref, cb_ref, wa_ref, ba_ref, wx_ref, bx_ref, lam_ref,
                              qkv_ref, h_ref, tail_out_ref, tail_ref, hc_ref):
    i = pl.program_id(1)
    tc = xin_ref.shape[0]

    @pl.when(i == 0)
    def _():
        tail_ref[...] = jnp.zeros_like(tail_ref)
        hc_ref[...] = jnp.zeros_like(hc_ref)

    xb = xin_ref[...].astype(BF16)
    qkv_ref[...] = jnp.dot(xb, w_ref[:, :OFF_RNN], preferred_element_type=F32)
    x = jnp.dot(xb, w_ref[:, OFF_RNN:], preferred_element_type=F32)
    xp = jnp.concatenate([tail_ref[...], x], axis=0)
    cw = cw_ref[...]
    xc = cb_ref[...] + x * cw[CONV_W - 1:CONV_W]
    for d in range(1, CONV_W):
        xc = xc + xp[8 - d:8 - d + tc] * cw[CONV_W - 1 - d:CONV_W - d]
    a, u = _rglru_gates(xc, wa_ref, ba_ref[...], wx_ref, bx_ref[...], lam_ref[...])
    ng = tc // 8
    acc_a = a.reshape(ng, 8, D_MODEL)
    acc_h = u.reshape(ng, 8, D_MODEL)
    sub = lax.broadcasted_iota(jnp.int32, (1, 8, D_MODEL), 1)
    for d in (1, 2, 4):
        keep = sub >= d
        sh_a = jnp.where(keep, pltpu.roll(acc_a, d, axis=1), 1.0)
        sh_h = jnp.where(keep, pltpu.roll(acc_h, d, axis=1), 0.0)
        acc_h = acc_h + acc_a * sh_h
        acc_a = acc_a * sh_a
    carry = hc_ref[...]
    for g in range(ng):
        hg = acc_h[g] + acc_a[g] * carry
        h_ref[g * 8:(g + 1) * 8, :] = hg
        carry = hg[7:8]
    hc_ref[...] = carry
    tail_ref[...] = x[tc - 8:tc]

    @pl.when(i == pl.num_programs(1) - 1)
    def _():
        tail_out_ref[0] = x[tc - 8:tc]


def _proj_rglru_prompt(x, w, nb_seq, seq, cw, cb, wa2, ba, wx2, bx, lam):
    tc = _tile(seq, 256)
    nt = seq // tc
    row = lambda b, i: (b * nt + i, 0)
    full2 = lambda b, i: (0, 0)
    full3 = lambda b, i: (0, 0, 0)
    return pl.pallas_call(
        _proj_rglru_prompt_kernel,
        grid=(nb_seq, nt),
        in_specs=[pl.BlockSpec((tc, D_MODEL), row),
                  pl.BlockSpec(w.shape, full2, pipeline_mode=pl.Buffered(1)),
                  pl.BlockSpec((CONV_W, D_MODEL), full2), pl.BlockSpec((1, D_MODEL), full2),
                  pl.BlockSpec(wa2.shape, full3), pl.BlockSpec((1, D_MODEL), full2),
                  pl.BlockSpec(wx2.shape, full3), pl.BlockSpec((1, D_MODEL), full2),
                  pl.BlockSpec((1, D_MODEL), full2)],
        out_specs=[pl.BlockSpec((tc, OFF_RNN), row), pl.BlockSpec((tc, D_MODEL), row),
                   pl.BlockSpec((1, 8, D_MODEL), lambda b, i: (b, 0, 0))],
        out_shape=[jax.ShapeDtypeStruct((nb_seq * seq, OFF_RNN), F32),
                   jax.ShapeDtypeStruct((nb_seq * seq, D_MODEL), F32),
                   jax.ShapeDtypeStruct((nb_seq, 8, D_MODEL), F32)],
        scratch_shapes=[pltpu.VMEM((8, D_MODEL), F32), pltpu.VMEM((1, D_MODEL), F32)],
        compiler_params=_cparams("parallel", "arbitrary"),
        name="proj_rglru_prompt",
    )(x, w, cw, cb, wa2, ba, wx2, bx, lam)


def _rglru_sample_kernel(x_ref, cs_ref, h0_ref, cw_ref, cb_ref, wa_ref, ba_ref, wx_ref, bx_ref, lam_ref,
                         h_ref):
    t, sn, _ = x_ref.shape
    cw = cw_ref[...]
    nst = CONV_W - 1

    def slab(s):
        return x_ref[s] if s >= 0 else cs_ref[nst + s]

    xcs = []
    for s in range(t):
        xc = cb_ref[...] + slab(s) * cw[CONV_W - 1:CONV_W]
        for d in range(1, CONV_W):
            xc = xc + slab(s - d) * cw[CONV_W - 1 - d:CONV_W - d]
        xcs.append(xc)
    xc_all = jnp.concatenate(xcs, axis=0)
    a, u = _rglru_gates(xc_all, wa_ref, ba_ref[...], wx_ref, bx_ref[...], lam_ref[...])
    h = h0_ref[...]
    for s in range(t):
        h = a[s * sn:(s + 1) * sn] * h + u[s * sn:(s + 1) * sn]
        h_ref[s] = h


def _rglru_sample(x, cs, h0, cw, cb, wa2, ba, wx2, bx, lam):
    t, n, _ = x.shape
    sn = _tile(n, 32)
    full2 = lambda i: (0, 0)
    full3 = lambda i: (0, 0, 0)
    return pl.pallas_call(
        _rglru_sample_kernel,
        grid=(n // sn,),
        in_specs=[pl.BlockSpec((t, sn, D_MODEL), lambda i: (0, i, 0)),
                  pl.BlockSpec((CONV_W - 1, sn, D_MODEL), lambda i: (0, i, 0)),
                  pl.BlockSpec((sn, D_MODEL), lambda i: (i, 0)),
                  pl.BlockSpec((CONV_W, D_MODEL), full2), pl.BlockSpec((1, D_MODEL), full2),
                  pl.BlockSpec(wa2.shape, full3), pl.BlockSpec((1, D_MODEL), full2),
                  pl.BlockSpec(wx2.shape, full3), pl.BlockSpec((1, D_MODEL), full2),
                  pl.BlockSpec((1, D_MODEL), full2)],
        out_specs=pl.BlockSpec((t, sn, D_MODEL), lambda i: (0, i, 0)),
        out_shape=jax.ShapeDtypeStruct((t, n, D_MODEL), F32),
        compiler_params=_cparams("parallel"),
        name="rglru_sample",
    )(x, cs, h0, cw, cb, wa2, ba, wx2, bx, lam)


def _rwkv_prep_math(pc, prev, mu, w0, a0, kkp, ka, wwa, gup, e, et):
    d = D_MODEL
    ps = pc + mu * (prev - pc)
    r = ps[:, 0:d]
    k = ps[:, d:2 * d]
    v = ps[:, 2 * d:3 * d]
    l01 = ps[:, 3 * d:3 * d + LANES]
    lane = lax.broadcasted_iota(jnp.int32, (1, LANES), 1)
    t01 = jnp.where(lane < LORA_W, jnp.tanh(l01), l01)
    wa = _dot(t01, wwa)
    g = _dot(_sigmoid(ps[:, 3 * d + LANES:]), gup)
    logw = -math.exp(-0.5) * _sigmoid(w0 + wa[:, :d])
    a = _sigmoid(a0 + wa[:, d:])
    kk = k * kkp
    kkn = kk * lax.rsqrt(jnp.maximum(_head_sum(kk * kk, e, et), 1e-24))
    kmod = k * (1.0 + (a - 1.0) * ka)
    return r, kmod, v, logw, kkn, kkn * a, g


_P_R, _P_K, _P_V, _P_LW, _P_KK, _P_B, _P_G = range(7)
_N_PREP_OUT = 7


def _plane_spec(plane, rows, index):
    return pl.BlockSpec((None, rows, D_MODEL), lambda *g: (plane, index(*g), 0))


def _rwkv_front_kernel(x_ref, w_ref, sh0_ref, mu_ref, w0_ref, a0_ref, kkp_ref, ka_ref, wwa_ref, gup_ref,
                       e_ref, et_ref, out_ref, sh_out_ref, carry_ref, *, shift_rows):
    i = pl.program_id(1)
    tm = x_ref.shape[0]
    ncar = carry_ref.shape[0]

    @pl.when(i == 0)
    def _():
        carry_ref[...] = sh0_ref[0]

    pc = jnp.dot(x_ref[...].astype(BF16), w_ref[...], preferred_element_type=F32)
    prev = jnp.concatenate([carry_ref[ncar - shift_rows:, :], pc[:tm - shift_rows]], axis=0)
    outs = _rwkv_prep_math(pc, prev, mu_ref[...], w0_ref[...], a0_ref[...], kkp_ref[...], ka_ref[...],
                           wwa_ref[...], gup_ref[...], e_ref[...], et_ref[...])
    for j, o in enumerate(outs):
        out_ref[j] = o
    carry_ref[...] = pc[tm - ncar:]

    @pl.when(i == pl.num_programs(1) - 1)
    def _():
        sh_out_ref[0] = pc[tm - ncar:]


def _rwkv_front(x, w_rw, shift0, n_groups, shift_rows, params):
    rows = x.shape[0] // n_groups
    ncar = shift0.shape[1]
    tm = _tile(rows, 256)
    assert tm >= ncar >= shift_rows and tm % shift_rows == 0
    nt = rows // tm
    row = lambda g, i: (g * nt + i, 0)
    full = lambda g, i: (0, 0)
    grp = pl.BlockSpec((1, ncar, SHIFT_PAD), lambda g, i: (g, 0, 0))
    return pl.pallas_call(
        functools.partial(_rwkv_front_kernel, shift_rows=shift_rows),
        grid=(n_groups, nt),
        in_specs=[pl.BlockSpec((tm, D_MODEL), row),
                  pl.BlockSpec(w_rw.shape, full, pipeline_mode=pl.Buffered(1)), grp]
                 + [pl.BlockSpec(p.shape, full) for p in params],
        out_specs=[pl.BlockSpec((_N_PREP_OUT, tm, D_MODEL), lambda g, i: (0, g * nt + i, 0)), grp],
        out_shape=[jax.ShapeDtypeStruct((_N_PREP_OUT, n_groups * rows, D_MODEL), F32),
                   jax.ShapeDtypeStruct((n_groups, ncar, SHIFT_PAD), F32)],
        scratch_shapes=[pltpu.VMEM((ncar, SHIFT_PAD), F32)],
        compiler_params=_cparams("parallel", "arbitrary"),
        name="rwkv_front",
    )(x, w_rw, shift0, *params)


_PAIR_W = 2 * HEAD_DIM
_N_PAIRS = D_MODEL // _PAIR_W
_WKV_STATE_PASSES = 1


def _dot_tn(a, b):
    return lax.dot_general(a.astype(BF16), b.astype(BF16), (((0,), (0,)), ((), ())),
                           preferred_element_type=F32)


def _dot_nt(a, b):
    return lax.dot_general(a.astype(BF16), b.astype(BF16), (((1,), (1,)), ((), ())),
                           preferred_element_type=F32)


def _dot_passes(a, b, passes):
    if passes == 1:
        return _dot(a, b)
    ah, al = _split_terms(a, 2)
    bh, bl = _split_terms(b, 2)
    return (jnp.dot(ah, bh, preferred_element_type=F32) + jnp.dot(ah, bl, preferred_element_type=F32)
            + jnp.dot(al, bh, preferred_element_type=F32))


def _wkv_prompt_kernel(r_ref, k_ref, v_ref, lw_ref, kk_ref, b_ref, y_ref, s_ref, h_ref):
    i = pl.program_id(1)
    rows = r_ref.shape[0]
    c = WKV_CHUNK
    nck = rows // c
    hd = HEAD_DIM
    pw = _PAIR_W

    @pl.when(i == 0)
    def _():
        h_ref[...] = jnp.zeros_like(h_ref)

    row_t = lax.broadcasted_iota(jnp.int32, (rows, rows), 0)
    col_t = lax.broadcasted_iota(jnp.int32, (rows, rows), 1)
    tri = (row_t >= col_t) & (row_t // c == col_t // c)
    lw = lw_ref[...]
    cum = _dot_rhs_split(jnp.where(tri, 1.0, 0.0).astype(BF16), lw, 3)
    last = jnp.concatenate([jnp.broadcast_to(cum[(ci + 1) * c - 1:(ci + 1) * c], (c, D_MODEL))
                            for ci in range(nck)], axis=0)
    g_inv = jnp.exp(-cum)
    g_end = jnp.exp(last - cum)
    kk = kk_ref[...]
    b = b_ref[...]
    k = k_ref[...]
    at_all = -(kk * jnp.exp(cum - lw))
    rt_all = r_ref[...] * jnp.exp(cum)
    bt_all = b * g_inv
    kt_all = k * g_inv
    bh_all = b * g_end
    kh_all = k * g_end
    gc_all = jnp.exp(last)

    rowp = lax.broadcasted_iota(jnp.int32, (c, pw), 0)
    colp = lax.broadcasted_iota(jnp.int32, (c, pw), 1) % hd
    strict = rowp > colp
    incl = rowp >= colp
    eye_p = jnp.where(rowp == colp, 1.0, 0.0)
    left = lax.broadcasted_iota(jnp.int32, (1, pw), 1) < hd
    r2 = lax.broadcasted_iota(jnp.int32, (pw, pw), 0)
    c2 = lax.broadcasted_iota(jnp.int32, (pw, pw), 1)
    same_head = (r2 < hd) == (c2 < hd)
    eye2 = r2 == c2

    def bd(w):
        return jnp.concatenate([jnp.where(left, w, 0.0), jnp.where(left, 0.0, w)], axis=0)

    pairs = range(_N_PAIRS)
    items = [(ci, p) for ci in range(nck) for p in pairs]
    idx = range(len(items))
    sub = [(slice(ci * c, (ci + 1) * c), slice(p * pw, (p + 1) * pw)) for ci, p in items]
    v_all = v_ref[...]
    at = [at_all[s] for s in sub]
    rt = [rt_all[s] for s in sub]
    v = [v_all[s] for s in sub]
    amat = [_dot_nt(jnp.concatenate([at[j], rt[j]], axis=0),
                    jnp.concatenate([bd(bt_all[sub[j]]), bd(kt_all[sub[j]])], axis=0))
            for j in idx]
    a_ab = [jnp.where(strict, a[:c, :pw], 0.0) for a in amat]
    a_ak = [jnp.where(strict, a[:c, pw:], 0.0) for a in amat]
    a_rb = [jnp.where(incl, a[c:, :pw], 0.0) for a in amat]
    a_rk = [jnp.where(incl, a[c:, pw:], 0.0) for a in amat]
    tinv = [eye_p + a for a in a_ab]
    npow = [_dot(a, bd(a)) for a in a_ab]
    span = 4
    while span < c:
        both = [_dot(jnp.concatenate([tinv[j], npow[j]], axis=0), bd(npow[j])) for j in idx]
        tinv = [tinv[j] + both[j][:c] for j in idx]
        npow = [x[c:] for x in both]
        span *= 2
    tinv = [tinv[j] + _dot(tinv[j], bd(npow[j])) for j in idx]
    w12 = [_dot(jnp.concatenate([a_ak[j], a_rk[j]], axis=0), bd(v[j])) for j in idx]
    pq = [_dot(tinv[j], jnp.concatenate([bd(at[j]), bd(w12[j][:c])], axis=1)) for j in idx]
    y10 = [jnp.concatenate([rt[j], w12[j][c:]], axis=1)
           + _dot(a_rb[j], jnp.concatenate([bd(pq[j][:, :pw]), bd(pq[j][:, pw:])], axis=1))
           for j in idx]
    mn = [_dot_tn(bh_all[sub[j]], pq[j]) for j in idx]
    kv = [_dot_tn(kh_all[sub[j]], v[j]) for j in idx]
    m = [jnp.where(same_head, mn[j][:, :pw], 0.0)
         + jnp.where(eye2, gc_all[sub[j][0], sub[j][1]][:1], 0.0) for j in idx]
    n = [jnp.where(same_head, mn[j][:, pw:] + kv[j], 0.0) for j in idx]
    h = [h_ref[p] for p in pairs]
    y_rows = []
    for ci in range(nck):
        js = [ci * _N_PAIRS + p for p in pairs]
        my = [_dot_passes(jnp.concatenate([m[j], y10[j][:, :pw]], axis=0), h[p], _WKV_STATE_PASSES)
              for p, j in zip(pairs, js)]
        h = [my[p][:pw] + n[j] for p, j in zip(pairs, js)]
        y_rows.append(jnp.concatenate([my[p][pw:] + y10[j][:, pw:] for p, j in zip(pairs, js)], axis=1))
    for p in pairs:
        h_ref[p] = h[p]
    y_ref[...] = jnp.concatenate(y_rows, axis=0)

    @pl.when(i == pl.num_programs(1) - 1)
    def _():
        for p in pairs:
            s_ref[0, p] = h[p]


def _wkv_prompt(prep, nb_seq, seq):
    c = WKV_CHUNK * WKV_CHUNKS_PER_TILE
    assert WKV_CHUNK == HEAD_DIM and seq % c == 0
    nc = seq // c
    row = lambda bb, i: (bb * nc + i, 0)
    planes = (_P_R, _P_K, _P_V, _P_LW, _P_KK, _P_B)
    return pl.pallas_call(
        _wkv_prompt_kernel,
        grid=(nb_seq, nc),
        in_specs=[_plane_spec(p, c, lambda bb, i: bb * nc + i) for p in planes],
        out_specs=[pl.BlockSpec((c, D_MODEL), row),
                   pl.BlockSpec((1, _N_PAIRS, _PAIR_W, _PAIR_W), lambda bb, i: (bb, 0, 0, 0))],
        out_shape=[jax.ShapeDtypeStruct((nb_seq * seq, D_MODEL), F32),
                   jax.ShapeDtypeStruct((nb_seq, _N_PAIRS, _PAIR_W, _PAIR_W), F32)],
        scratch_shapes=[pltpu.VMEM((_N_PAIRS, _PAIR_W, _PAIR_W), F32)],
        compiler_params=_cparams("parallel", "arbitrary"),
        name="wkv_prompt",
    )(*([prep] * len(planes)))


def _unpair_state(hb):
    n = hb.shape[0]
    h6 = hb.reshape(n, _N_PAIRS, 2, HEAD_DIM, 2, HEAD_DIM)
    blocks = jnp.stack([h6[:, :, 0, :, 0, :], h6[:, :, 1, :, 1, :]], axis=2)
    return jnp.swapaxes(blocks, -1, -2).reshape(n, N_HEADS, HEAD_DIM, HEAD_DIM)


_WKV_VROWS = 4


def _wkv_sample_kernel(r_ref, lw_ref, k_ref, v_ref, kk_ref, b_ref, s0_ref, y_ref, s_ref, w_ref):
    t = r_ref.shape[0]
    vr = _WKV_VROWS
    w_ref[...] = jnp.exp(lw_ref[...])
    for v0 in range(0, HEAD_DIM, vr):
        s = [s0_ref[0, v0 + j] for j in range(vr)]
        for st in range(t):
            kk, w, b, k, r = kk_ref[st], w_ref[st], b_ref[st], k_ref[st], r_ref[st]
            vrow = v_ref[st, v0:v0 + vr, :]
            ys = []
            for j in range(vr):
                sa = -jnp.sum(s[j] * kk, axis=0, keepdims=True)
                s[j] = s[j] * w + sa * b + vrow[j:j + 1] * k
                ys.append(jnp.sum(s[j] * r, axis=0, keepdims=True))
            y_ref[st, v0:v0 + vr, :] = jnp.concatenate(ys, axis=0)
        for j in range(vr):
            s_ref[0, v0 + j] = s[j]


def _wkv_sample(vecs, s0):
    _, t, nh, hd, n = vecs.shape
    vec = lambda p: pl.BlockSpec((None, t, None, hd, n), lambda h: (p, 0, h, 0, 0))
    st = pl.BlockSpec((1, hd, hd, n), lambda h: (h, 0, 0, 0))
    planes = (_P_R, _P_LW, _P_K, _P_V, _P_KK, _P_B)
    return pl.pallas_call(
        _wkv_sample_kernel,
        grid=(nh,),
        in_specs=[vec(p) for p in planes] + [st],
        out_specs=[pl.BlockSpec((t, None, hd, n), lambda h: (0, h, 0, 0)), st],
        out_shape=[jax.ShapeDtypeStruct((t, nh, hd, n), F32),
                   jax.ShapeDtypeStruct((nh, hd, hd, n), F32)],
        scratch_shapes=[pltpu.VMEM((t, hd, n), F32)],
        compiler_params=_cparams("parallel"),
        name="wkv_sample",
    )(*([vecs] * len(planes)), s0)


def _mix_out_kernel(x_ref, oa_ref, ob_ref, y_ref, r_ref, k_ref, v_ref, g_ref, wgate_ref,
                    lnw_ref, lnb_ref, rk_ref, wout_ref, g1_ref, b1_ref, e_ref, et_ref, o_ref, *, alpha):
    d = D_MODEL
    x = x_ref[...]
    gate = jnp.dot(x.astype(BF16), wgate_ref[...], preferred_element_type=F32)
    e, et = e_ref[...], et_ref[...]
    y = y_ref[...]
    mu = _head_sum(y, e, et) * (1.0 / HEAD_DIM)
    yc = y - mu
    var = _head_sum(yc * yc, e, et) * (1.0 / HEAD_DIM)
    yn = yc * lax.rsqrt(var + GN_EPS) * lnw_ref[...] + lnb_ref[...]
    v = v_ref[...]
    bonus = _head_sum(r_ref[...] * k_ref[...] * rk_ref[...], e, et) * v
    oc = (yn + bonus) * g_ref[...]
    mixed = (_sigmoid(gate[:, 0:d]) * oa_ref[...] + _sigmoid(gate[:, d:2 * d]) * ob_ref[...]
             + _sigmoid(gate[:, 2 * d:3 * d]) * oc)
    z = alpha * x + jnp.dot(mixed.astype(BF16), wout_ref[...], preferred_element_type=F32)
    o_ref[...] = _layer_norm(z, g1_ref[...], b1_ref[...])


def _mix_out(x, oa, ob, y, prep, wgate, lnw, lnb, rk, wout, g1, b1, e, et, alpha):
    m = x.shape[0]
    tm = _tile(m, 256)
    row = lambda i: (i, 0)
    full = lambda i: (0, 0)
    rows = pl.BlockSpec((tm, D_MODEL), row)
    vec = pl.BlockSpec((1, D_MODEL), full)
    resident = lambda w: pl.BlockSpec(w.shape, full, pipeline_mode=pl.Buffered(1))
    planes = (_P_R, _P_K, _P_V, _P_G)
    return pl.pallas_call(
        functools.partial(_mix_out_kernel, alpha=alpha),
        grid=(m // tm,),
        in_specs=[rows] * 4 + [_plane_spec(p, tm, lambda i: i) for p in planes]
                 + [resident(wgate), vec, vec, vec, resident(wout), vec, vec,
                    pl.BlockSpec(e.shape, full), pl.BlockSpec(et.shape, full)],
        out_specs=rows,
        out_shape=jax.ShapeDtypeStruct((m, D_MODEL), F32),
        compiler_params=_cparams("parallel"),
        name="mix_out",
    )(x, oa, ob, y, *([prep] * len(planes)), wgate, lnw, lnb, rk, wout, g1, b1, e, et)


def _ffn_kernel(x_ref, wgu_ref, wd_ref, g2_ref, b2_ref, o_ref, *, alpha):
    dff = wd_ref.shape[0]
    x = x_ref[...]
    xb = x.astype(BF16)
    gt = jnp.dot(xb, wgu_ref[:, :dff], preferred_element_type=F32)
    up = jnp.dot(xb, wgu_ref[:, dff:], preferred_element_type=F32)
    ff = jnp.dot((gt * _sigmoid(gt) * up).astype(BF16), wd_ref[...], preferred_element_type=F32)
    o_ref[...] = _layer_norm(alpha * x + ff, g2_ref[...], b2_ref[...])


def _ffn(x, wgu, wd, g2, b2, alpha):
    m = x.shape[0]
    tm = _tile(m, 256)
    row = lambda i: (i, 0)
    full = lambda i: (0, 0)
    resident = lambda w: pl.BlockSpec(w.shape, full, pipeline_mode=pl.Buffered(1))
    return pl.pallas_call(
        functools.partial(_ffn_kernel, alpha=alpha),
        grid=(m // tm,),
        in_specs=[pl.BlockSpec((tm, D_MODEL), row), resident(wgu), resident(wd),
                  pl.BlockSpec((1, D_MODEL), full), pl.BlockSpec((1, D_MODEL), full)],
        out_specs=pl.BlockSpec((tm, D_MODEL), row),
        out_shape=jax.ShapeDtypeStruct((m, D_MODEL), F32),
        compiler_params=_cparams("parallel"),
        name="ffn",
    )(x, wgu, wd, g2, b2)


def _blockdiag_pairs(w):
    nb, bs, _ = w.shape
    w = w.reshape(nb // 2, 2, bs, bs)
    z = jnp.zeros((nb // 2, bs, bs), w.dtype)
    top = jnp.concatenate([w[:, 0], z], axis=2)
    bot = jnp.concatenate([z, w[:, 1]], axis=2)
    return jnp.concatenate([top, bot], axis=1)


def _row(p):
    return p.reshape(1, -1)


def kernel(x_prompt, x_sample, cache_k, cache_v, state_conv, state_rglru, state_shift, state_wkv, w_in, attn_sinks, conv_w, conv_b, rg_wa, rg_ba, rg_wx, rg_bx, rg_lambda, rw_mu, rw_w0, rw_wup, rw_a0, rw_aup, rw_gup, rw_kk, rw_ka, rw_rk, rw_lnw, rw_lnb, w_out, ln1_g, ln1_b, w_gu, w_down, ln2_g, ln2_b):
    depth = w_in.shape[0]
    bp, seq, d = x_prompt.shape
    ns, ts, _ = x_sample.shape
    wb = cache_k.shape[2]
    alpha = (2 * depth) ** 0.25
    dff = w_down.shape[1]
    pad = SHIFT_PAD - SHIFT_W

    xp = x_prompt.reshape(bp * seq, d)
    xs = jnp.swapaxes(x_sample, 0, 1).reshape(ts * ns, d)
    st_p = [[] for _ in range(6)]
    st_s = [[] for _ in range(6)]
    head_e, head_et = _head_indicator()
    w_in_bf = w_in.astype(BF16)
    w_gu_bf = w_gu.astype(BF16)
    w_down_bf = w_down.astype(BF16)

    for l in range(depth):
        wl = w_in_bf[l]
        w_all = wl[:, :OFF_RW]
        w_rw = jnp.pad(wl[:, OFF_RW:OFF_GATE], ((0, 0), (0, pad)))
        w_gate = wl[:, OFF_GATE:]
        wa2 = _blockdiag_pairs(rg_wa[l]).astype(BF16)
        wx2 = _blockdiag_pairs(rg_wx[l]).astype(BF16)
        rg_args = (conv_w[l], _row(conv_b[l]), wa2, _row(rg_ba[l]), wx2, _row(rg_bx[l]), _row(rg_lambda[l]))
        zw = jnp.zeros((LORA_W, d), F32)
        wwa = jnp.concatenate([jnp.concatenate([rw_wup[l], zw], axis=1),
                               jnp.concatenate([zw, rw_aup[l]], axis=1)], axis=0).astype(BF16)
        gup = jnp.pad(rw_gup[l], ((0, 2 * LANES - LORA_G), (0, 0))).astype(BF16)
        prep_params = (_row(jnp.pad(rw_mu[l], (0, pad))), _row(rw_w0[l]), _row(rw_a0[l]), _row(rw_kk[l]),
                       _row(rw_ka[l]), wwa, gup, head_e, head_et)
        mix_params = (_row(rw_lnw[l]), _row(rw_lnb[l]), _row(rw_rk[l]), w_out[l].astype(BF16),
                      _row(ln1_g[l]), _row(ln1_b[l]), head_e, head_et)
        ffn_params = (w_gu_bf[l], w_down_bf[l], _row(ln2_g[l]), _row(ln2_b[l]))
        sinks = attn_sinks[l]

        qkv, ob, rnn_tail = _proj_rglru_prompt(xp, w_all, bp, seq, *rg_args)
        oa = _attn_prompt(qkv, sinks, bp, seq)
        prep, sh_p = _rwkv_front(xp, w_rw, jnp.zeros((bp, 8, SHIFT_PAD), F32), bp, 1, prep_params)
        y, hfin = _wkv_prompt(prep, bp, seq)
        x1 = _mix_out(xp, oa, ob, y, prep, w_gate, *mix_params, alpha)
        xp = _ffn(x1, *ffn_params, alpha)
        qkv3 = qkv.reshape(bp, seq, OFF_RNN)
        st_p[0].append(qkv3[:, seq - wb:, OFF_K:OFF_V].reshape(bp, wb, N_KV_HEADS, HEAD_DIM))
        st_p[1].append(qkv3[:, seq - wb:, OFF_V:OFF_RNN].reshape(bp, wb, N_KV_HEADS, HEAD_DIM))
        st_p[2].append(rnn_tail[:, 8 - (CONV_W - 1):])
        st_p[3].append(ob.reshape(bp, seq, d)[:, seq - 1])
        st_p[4].append(sh_p[:, 7, :SHIFT_W])
        st_p[5].append(_unpair_state(hfin))

        qkv, rnn = _in_proj(xs, w_all)
        qkv_n = jnp.swapaxes(qkv.reshape(ts, ns, OFF_RNN), 0, 1)
        oa_n, new_k, new_v = _attn_sample(qkv_n[:, :, :OFF_K], qkv_n[:, :, OFF_K:OFF_V], qkv_n[:, :, OFF_V:],
                                          cache_k[l].reshape(ns, wb, KV_W), cache_v[l].reshape(ns, wb, KV_W),
                                          sinks)
        oa = jnp.swapaxes(oa_n, 0, 1).reshape(ts * ns, d)
        rnn3 = rnn.reshape(ts, ns, d)
        ob3 = _rglru_sample(rnn3, jnp.swapaxes(state_conv[l], 0, 1), state_rglru[l], *rg_args)
        ob = ob3.reshape(ts * ns, d)
        shift = jnp.pad(state_shift[l], ((0, 0), (0, pad)))[None]
        prep, sh_s = _rwkv_front(xs, w_rw, shift, 1, ns, prep_params)
        vecs = jnp.swapaxes(prep.reshape(_N_PREP_OUT, ts, ns, d), 2, 3).reshape(
            _N_PREP_OUT, ts, N_HEADS, HEAD_DIM, ns)
        s0 = jnp.transpose(state_wkv[l], (1, 2, 3, 0))
        y4, s_new = _wkv_sample(vecs, s0)
        y = jnp.swapaxes(y4.reshape(ts, d, ns), 1, 2).reshape(ts * ns, d)
        x1 = _mix_out(xs, oa, ob, y, prep, w_gate, *mix_params, alpha)
        xs = _ffn(x1, *ffn_params, alpha)
        st_s[0].append(new_k.reshape(ns, wb, N_KV_HEADS, HEAD_DIM))
        st_s[1].append(new_v.reshape(ns, wb, N_KV_HEADS, HEAD_DIM))
        st_s[2].append(jnp.swapaxes(rnn3[ts - (CONV_W - 1):], 0, 1))
        st_s[3].append(ob3[ts - 1])
        st_s[4].append(sh_s[0, :, :SHIFT_W])
        st_s[5].append(jnp.transpose(s_new, (3, 0, 1, 2)))

    yp = xp.reshape(bp, seq, d)
    ys = jnp.swapaxes(xs.reshape(ts, ns, d), 0, 1)
    return (yp, ys, *(jnp.stack(s) for s in st_p), *(jnp.stack(s) for s in st_s))
```

```python
import functools
import math

import jax
import jax.numpy as jnp
from jax import lax
from jax.experimental import pallas as pl
from jax.experimental.pallas import tpu as pltpu

F32 = jnp.float32
BF16 = jnp.bfloat16

D_MODEL = 1024
HEAD_DIM = 64
N_HEADS = 16
N_KV_HEADS = 4
KV_W = N_KV_HEADS * HEAD_DIM
WINDOW = 128
CONV_W = 4
RG_C = 8.0
LORA_W = 64
LORA_A = 64
LORA_G = 160
SHIFT_W = 3 * D_MODEL + LORA_W + LORA_A + LORA_G
LANES = 128
SHIFT_PAD = -(-SHIFT_W // LANES) * LANES
GN_EPS = 64e-5
LN_EPS = 1e-5
OFF_K = D_MODEL
OFF_V = OFF_K + KV_W
OFF_RNN = OFF_V + KV_W
OFF_RW = OFF_RNN + D_MODEL
OFF_GATE = OFF_RW + SHIFT_W
WKV_CHUNK = 64
WKV_CHUNKS_PER_TILE = 4
VMEM_LIMIT = 48 * 1024 * 1024


def _cparams(*sem):
    return pltpu.CompilerParams(dimension_semantics=sem, vmem_limit_bytes=VMEM_LIMIT)


def _tile(n, pref):
    if n <= pref:
        return n
    for t in range(pref, 7, -1):
        if n % t == 0 and t % 8 == 0:
            return t
    return n


def _dot(a, b):
    return jnp.dot(a.astype(BF16), b.astype(BF16), preferred_element_type=F32)


def _split_terms(x, n):
    terms = []
    rem = x
    for _ in range(n):
        hi = rem.astype(BF16)
        terms.append(hi)
        rem = rem - hi.astype(F32)
    return terms


def _dot_rhs_split(w_bf16, x, n):
    return jnp.dot(jnp.concatenate([w_bf16] * n, axis=1), jnp.concatenate(_split_terms(x, n), axis=0),
                   preferred_element_type=F32)


def _sigmoid(x):
    return 0.5 * jnp.tanh(0.5 * x) + 0.5


def _softplus(x):
    return jnp.maximum(x, 0.0) + jnp.log(1.0 + jnp.exp(-jnp.abs(x)))


def _layer_norm(z, g, b):
    mu = jnp.mean(z, axis=-1, keepdims=True)
    zc = z - mu
    var = jnp.mean(zc * zc, axis=-1, keepdims=True)
    return zc * lax.rsqrt(var + LN_EPS) * g + b


def _head_indicator():
    c = lax.broadcasted_iota(jnp.int32, (D_MODEL, LANES), 0) // HEAD_DIM
    h = lax.broadcasted_iota(jnp.int32, (D_MODEL, LANES), 1)
    e = jnp.where(c == h, 1.0, 0.0).astype(BF16)
    ct = lax.broadcasted_iota(jnp.int32, (LANES, D_MODEL), 1) // HEAD_DIM
    ht = lax.broadcasted_iota(jnp.int32, (LANES, D_MODEL), 0)
    et = jnp.where(ct == ht, 1.0, 0.0).astype(BF16)
    return e, jnp.concatenate([et, et], axis=0)


def _head_sum(x, e, et2, scale=None):
    s = jnp.dot(x.astype(BF16), e, preferred_element_type=F32)
    if scale is not None:
        s = s * scale
    hi = s.astype(BF16)
    lo = (s - hi.astype(F32)).astype(BF16)
    return jnp.dot(jnp.concatenate([hi, lo], axis=1), et2, preferred_element_type=F32)


_PROJ_WIDTHS = (OFF_RNN, D_MODEL)


def _in_proj_kernel(x_ref, w_ref, *o_refs):
    xb = x_ref[...].astype(BF16)
    off = 0
    for o_ref in o_refs:
        n = o_ref.shape[1]
        o_ref[...] = jnp.dot(xb, w_ref[:, off:off + n], preferred_element_type=F32)
        off += n


def _in_proj(x, w):
    m, k = x.shape
    tm = _tile(m, 256)
    return pl.pallas_call(
        _in_proj_kernel,
        grid=(m // tm,),
        in_specs=[pl.BlockSpec((tm, k), lambda i: (i, 0)),
                  pl.BlockSpec(w.shape, lambda i: (0, 0), pipeline_mode=pl.Buffered(1))],
        out_specs=[pl.BlockSpec((tm, n), lambda i: (i, 0)) for n in _PROJ_WIDTHS],
        out_shape=[jax.ShapeDtypeStruct((m, n), F32) for n in _PROJ_WIDTHS],
        compiler_params=_cparams("parallel"),
        name="in_proj",
    )(x, w)


_LOG2E = math.log2(math.e)


def _alibi_slope(h):
    return 2.0 ** (-8.0 * (h + 1) / N_HEADS)


def _attn_prompt_bias():
    w = WINDOW
    rel = w + jnp.arange(w)[:, None] - jnp.arange(2 * w)[None, :]
    band = (rel >= 0) & (rel < WINDOW)
    has_prev = jnp.arange(2)[:, None, None] > 0
    valid = band[None] & (has_prev | (jnp.arange(2 * w) >= w)[None, None, :])
    slopes = jnp.array([_alibi_slope(h) for h in range(N_HEADS)], F32)
    bias = -slopes[None, :, None, None] * rel.astype(F32)[None, None]
    return jnp.where(valid[:, None], bias * _LOG2E, -jnp.inf)


_ATTN_BLOCKS_PER_STEP = 2


def _attn_prompt_kernel(sink_ref, bias_ref, q_ref, kc_ref, vc_ref, kp_ref, vp_ref, o_ref):
    j = pl.program_id(1)
    w = WINDOW
    nblk = q_ref.shape[0] // w
    q = q_ref[...] * (HEAD_DIM ** -0.5 * _LOG2E)
    kall = jnp.concatenate([kp_ref[...], kc_ref[...]], axis=0).astype(BF16)
    vall = jnp.concatenate([vp_ref[...], vc_ref[...]], axis=0).astype(BF16)
    first = jnp.minimum(j, 1)
    gsz = N_HEADS // N_KV_HEADS
    items = [(n, h) for n in range(nblk) for h in range(N_HEADS)]
    kg = {(n, g): kall[n * w:(n + 2) * w, g * HEAD_DIM:(g + 1) * HEAD_DIM]
          for n in range(nblk) for g in range(N_KV_HEADS)}
    vg = {(n, g): vall[n * w:(n + 2) * w, g * HEAD_DIM:(g + 1) * HEAD_DIM]
          for n in range(nblk) for g in range(N_KV_HEADS)}
    qh = [q[n * w:(n + 1) * w, h * HEAD_DIM:(h + 1) * HEAD_DIM].astype(BF16) for n, h in items]
    bias = [bias_ref[first, h] if n == 0 else bias_ref[1, h] for n, h in items]
    s = [lax.dot_general(qh[i], kg[(n, h // gsz)], (((1,), (1,)), ((), ())), preferred_element_type=F32)
         + bias[i] for i, (n, h) in enumerate(items)]
    sink = [sink_ref[h] * _LOG2E for n, h in items]
    m = [jnp.maximum(jnp.max(s[i], axis=-1, keepdims=True), sink[i]) for i in range(len(items))]
    e = [jnp.exp2(s[i] - m[i]) for i in range(len(items))]
    den = [jnp.sum(e[i], axis=-1, keepdims=True) + jnp.exp2(sink[i] - m[i]) for i in range(len(items))]
    p = [(e[i] / den[i]).astype(BF16) for i in range(len(items))]
    outs = [jnp.dot(p[i], vg[(n, h // gsz)], preferred_element_type=F32) for i, (n, h) in enumerate(items)]
    o_ref[...] = jnp.concatenate(
        [jnp.concatenate(outs[n * N_HEADS:(n + 1) * N_HEADS], axis=1) for n in range(nblk)], axis=0)


def _attn_prompt(qkv, sinks, nb_seq, seq):
    w = WINDOW
    nblk = _ATTN_BLOCKS_PER_STEP if seq % (_ATTN_BLOCKS_PER_STEP * w) == 0 else 1
    tq = nblk * w
    nb = seq // tq
    kcol = OFF_K // KV_W
    vcol = OFF_V // KV_W
    bias = _attn_prompt_bias()

    def cur(c):
        return lambda b, j: (b * nb + j, c)

    def prev(c):
        return lambda b, j: (jnp.maximum((b * nb + j) * nblk - 1, 0), c)

    return pl.pallas_call(
        _attn_prompt_kernel,
        grid=(nb_seq, nb),
        in_specs=[pl.BlockSpec(memory_space=pltpu.SMEM),
                  pl.BlockSpec(bias.shape, lambda b, j: (0, 0, 0, 0), pipeline_mode=pl.Buffered(1)),
                  pl.BlockSpec((tq, D_MODEL), cur(0)),
                  pl.BlockSpec((tq, KV_W), cur(kcol)),
                  pl.BlockSpec((tq, KV_W), cur(vcol)),
                  pl.BlockSpec((w, KV_W), prev(kcol)),
                  pl.BlockSpec((w, KV_W), prev(vcol))],
        out_specs=pl.BlockSpec((tq, D_MODEL), cur(0)),
        out_shape=jax.ShapeDtypeStruct((nb_seq * seq, D_MODEL), F32),
        compiler_params=_cparams("parallel", "parallel"),
        name="attn_prompt",
    )(sinks, bias, qkv, qkv, qkv, qkv, qkv)


def _attn_sample_kernel(sink_ref, q_ref, kn_ref, vn_ref, ck_ref, cv_ref, o_ref, nk_ref, nv_ref, *, t):
    sn = q_ref.shape[0]
    wb = ck_ref.shape[1]
    gsz = N_HEADS // N_KV_HEADS
    rows = gsz * t
    q = q_ref[...]
    kn = kn_ref[...]
    vn = vn_ref[...]
    ck = ck_ref[...]
    cv = cv_ref[...]
    tq1 = lax.broadcasted_iota(jnp.int32, (rows, wb), 0) % t
    j1 = lax.broadcasted_iota(jnp.int32, (rows, wb), 1)
    rel1 = wb + tq1 - j1
    valid1 = (rel1 >= 0) & (rel1 < WINDOW)
    tq2 = lax.broadcasted_iota(jnp.int32, (rows, t), 0) % t
    s2i = lax.broadcasted_iota(jnp.int32, (rows, t), 1)
    rel2 = tq2 - s2i
    valid2 = (rel2 >= 0) & (rel2 < WINDOW)
    hrow = lax.broadcasted_iota(jnp.int32, (rows, 1), 0) // t
    outs = []
    for g in range(N_KV_HEADS):
        slope = jnp.zeros((rows, 1), F32)
        sink = jnp.zeros((rows, 1), F32)
        for hh in range(gsz):
            h = g * gsz + hh
            slope = jnp.where(hrow == hh, _alibi_slope(h), slope)
            sink = jnp.where(hrow == hh, sink_ref[h], sink)
        qg = jnp.concatenate([q[:, :, (g * gsz + hh) * HEAD_DIM:(g * gsz + hh + 1) * HEAD_DIM]
                              for hh in range(gsz)], axis=1).astype(BF16)
        lo, hi = g * HEAD_DIM, (g + 1) * HEAD_DIM
        ckg = ck[:, :, lo:hi].astype(BF16)
        cvg = cv[:, :, lo:hi].astype(BF16)
        kng = kn[:, :, lo:hi].astype(BF16)
        vng = vn[:, :, lo:hi].astype(BF16)
        nt = (((2,), (2,)), ((0,), (0,)))
        nn = (((2,), (1,)), ((0,), (0,)))
        s1 = lax.dot_general(qg, ckg, nt, preferred_element_type=F32) * (HEAD_DIM ** -0.5)
        s2 = lax.dot_general(qg, kng, nt, preferred_element_type=F32) * (HEAD_DIM ** -0.5)
        s1 = jnp.where(valid1[None], s1 - (slope * rel1.astype(F32))[None], -jnp.inf)
        s2 = jnp.where(valid2[None], s2 - (slope * rel2.astype(F32))[None], -jnp.inf)
        m = jnp.maximum(jnp.maximum(jnp.max(s1, axis=-1, keepdims=True),
                                    jnp.max(s2, axis=-1, keepdims=True)), sink[None])
        e1 = jnp.exp(s1 - m)
        e2 = jnp.exp(s2 - m)
        den = (jnp.sum(e1, axis=-1, keepdims=True) + jnp.sum(e2, axis=-1, keepdims=True)
               + jnp.exp(sink[None] - m))
        p1 = (e1 / den).astype(BF16)
        p2 = (e2 / den).astype(BF16)
        og = (lax.dot_general(p1, cvg, nn, preferred_element_type=F32)
              + lax.dot_general(p2, vng, nn, preferred_element_type=F32))
        outs.extend(og[:, hh * t:(hh + 1) * t, :] for hh in range(gsz))
    o_ref[...] = jnp.concatenate(outs, axis=2)
    nk_ref[...] = jnp.concatenate([ck[:, t:, :], kn], axis=1)
    nv_ref[...] = jnp.concatenate([cv[:, t:, :], vn], axis=1)


def _attn_sample(q, kn, vn, ck, cv, sinks):
    n, t, _ = q.shape
    wb = ck.shape[1]
    sn = _tile(n, 16)
    blk = lambda d1, d2: pl.BlockSpec((sn, d1, d2), lambda i: (i, 0, 0))
    return pl.pallas_call(
        functools.partial(_attn_sample_kernel, t=t),
        grid=(n // sn,),
        in_specs=[pl.BlockSpec(memory_space=pltpu.SMEM),
                  blk(t, D_MODEL), blk(t, KV_W), blk(t, KV_W), blk(wb, KV_W), blk(wb, KV_W)],
        out_specs=[blk(t, D_MODEL), blk(wb, KV_W), blk(wb, KV_W)],
        out_shape=[jax.ShapeDtypeStruct((n, t, D_MODEL), F32),
                   jax.ShapeDtypeStruct((n, wb, KV_W), F32),
                   jax.ShapeDtypeStruct((n, wb, KV_W), F32)],
        compiler_params=_cparams("parallel"),
        name="attn_sample",
    )(sinks, q, kn, vn, ck, cv)


def _rglru_gates(xc, wa_ref, ba, wx_ref, bx, lam):
    xb = xc.astype(BF16)
    npair = D_MODEL // LANES
    ga = jnp.concatenate([jnp.dot(xb[:, j * LANES:(j + 1) * LANES], wa_ref[j], preferred_element_type=F32)
                          for j in range(npair)], axis=1)
    gx = jnp.concatenate([jnp.dot(xb[:, j * LANES:(j + 1) * LANES], wx_ref[j], preferred_element_type=F32)
                          for j in range(npair)], axis=1)
    r = _sigmoid(ga + ba)
    ig = _sigmoid(gx + bx)
    z = r * (RG_C * _softplus(-lam))
    a = jnp.exp(-z)
    u = jnp.sqrt(jnp.tanh(z) * (a * a + 1.0)) * (ig * xc)
    return a, u


def _proj_rglru_prompt_kernel(xin_ref, w_ref, cw_ref, cb_ref, wa_ref, ba_ref, wx_ref, bx_ref, lam_ref,
                              qkv_ref, h_ref, tail_out_ref, tail_ref, hc_ref):
    i = pl.program_id(1)
    tc = xin_ref.shape[0]

    @pl.when(i == 0)
    def _():
        tail_ref[...] = jnp.zeros_like(tail_ref)
        hc_ref[...] = jnp.zeros_like(hc_ref)

    xb = xin_ref[...].astype(BF16)
    qkv_ref[...] = jnp.dot(xb, w_ref[:, :OFF_RNN], preferred_element_type=F32)
    x = jnp.dot(xb, w_ref[:, OFF_RNN:], preferred_element_type=F32)
    xp = jnp.concatenate([tail_ref[...], x], axis=0)
    cw = cw_ref[...]
    xc = cb_ref[...] + x * cw[CONV_W - 1:CONV_W]
    for d in range(1, CONV_W):
        xc = xc + xp[8 - d:8 - d + tc] * cw[CONV_W - 1 - d:CONV_W - d]
    a, u = _rglru_gates(xc, wa_ref, ba_ref[...], wx_ref, bx_ref[...], lam_ref[...])
    ng = tc // 8
    acc_a = a.reshape(ng, 8, D_MODEL)
    acc_h = u.reshape(ng, 8, D_MODEL)
    sub = lax.broadcasted_iota(jnp.int32, (1, 8, D_MODEL), 1)
    for d in (1, 2, 4):
        keep = sub >= d
        sh_a = jnp.where(keep, pltpu.roll(acc_a, d, axis=1), 1.0)
        sh_h = jnp.where(keep, pltpu.roll(acc_h, d, axis=1), 0.0)
        acc_h = acc_h + acc_a * sh_h
        acc_a = acc_a * sh_a
    carry = hc_ref[...]
    for g in range(ng):
        hg = acc_h[g] + acc_a[g] * carry
        h_ref[g * 8:(g + 1) * 8, :] = hg
        carry = hg[7:8]
    hc_ref[...] = carry
    tail_ref[...] = x[tc - 8:tc]

    @pl.when(i == pl.num_programs(1) - 1)
    def _():
        tail_out_ref[0] = x[tc - 8:tc]


def _proj_rglru_prompt(x, w, nb_seq, seq, cw, cb, wa2, ba, wx2, bx, lam):
    tc = _tile(seq, 256)
    nt = seq // tc
    row = lambda b, i: (b * nt + i, 0)
    full2 = lambda b, i: (0, 0)
    full3 = lambda b, i: (0, 0, 0)
    return pl.pallas_call(
        _proj_rglru_prompt_kernel,
        grid=(nb_seq, nt),
        in_specs=[pl.BlockSpec((tc, D_MODEL), row),
                  pl.BlockSpec(w.shape, full2, pipeline_mode=pl.Buffered(1)),
                  pl.BlockSpec((CONV_W, D_MODEL), full2), pl.BlockSpec((1, D_MODEL), full2),
                  pl.BlockSpec(wa2.shape, full3), pl.BlockSpec((1, D_MODEL), full2),
                  pl.BlockSpec(wx2.shape, full3), pl.BlockSpec((1, D_MODEL), full2),
                  pl.BlockSpec((1, D_MODEL), full2)],
        out_specs=[pl.BlockSpec((tc, OFF_RNN), row), pl.BlockSpec((tc, D_MODEL), row),
                   pl.BlockSpec((1, 8, D_MODEL), lambda b, i: (b, 0, 0))],
        out_shape=[jax.ShapeDtypeStruct((nb_seq * seq, OFF_RNN), F32),
                   jax.ShapeDtypeStruct((nb_seq * seq, D_MODEL), F32),
                   jax.ShapeDtypeStruct((nb_seq, 8, D_MODEL), F32)],
        scratch_shapes=[pltpu.VMEM((8, D_MODEL), F32), pltpu.VMEM((1, D_MODEL), F32)],
        compiler_params=_cparams("parallel", "arbitrary"),
        name="proj_rglru_prompt",
    )(x, w, cw, cb, wa2, ba, wx2, bx, lam)


def _rglru_sample_kernel(x_ref, cs_ref, h0_ref, cw_ref, cb_ref, wa_ref, ba_ref, wx_ref, bx_ref, lam_ref,
                         h_ref):
    t, sn, _ = x_ref.shape
    cw = cw_ref[...]
    nst = CONV_W - 1

    def slab(s):
        return x_ref[s] if s >= 0 else cs_ref[nst + s]

    xcs = []
    for s in range(t):
        xc = cb_ref[...] + slab(s) * cw[CONV_W - 1:CONV_W]
        for d in range(1, CONV_W):
            xc = xc + slab(s - d) * cw[CONV_W - 1 - d:CONV_W - d]
        xcs.append(xc)
    xc_all = jnp.concatenate(xcs, axis=0)
    a, u = _rglru_gates(xc_all, wa_ref, ba_ref[...], wx_ref, bx_ref[...], lam_ref[...])
    h = h0_ref[...]
    for s in range(t):
        h = a[s * sn:(s + 1) * sn] * h + u[s * sn:(s + 1) * sn]
        h_ref[s] = h


def _rglru_sample(x, cs, h0, cw, cb, wa2, ba, wx2, bx, lam):
    t, n, _ = x.shape
    sn = _tile(n, 32)
    full2 = lambda i: (0, 0)
    full3 = lambda i: (0, 0, 0)
    return pl.pallas_call(
        _rglru_sample_kernel,
        grid=(n // sn,),
        in_specs=[pl.BlockSpec((t, sn, D_MODEL), lambda i: (0, i, 0)),
                  pl.BlockSpec((CONV_W - 1, sn, D_MODEL), lambda i: (0, i, 0)),
                  pl.BlockSpec((sn, D_MODEL), lambda i: (i, 0)),
                  pl.BlockSpec((CONV_W, D_MODEL), full2), pl.BlockSpec((1, D_MODEL), full2),
                  pl.BlockSpec(wa2.shape, full3), pl.BlockSpec((1, D_MODEL), full2),
                  pl.BlockSpec(wx2.shape, full3), pl.BlockSpec((1, D_MODEL), full2),
                  pl.BlockSpec((1, D_MODEL), full2)],
        out_specs=pl.BlockSpec((t, sn, D_MODEL), lambda i: (0, i, 0)),
        out_shape=jax.ShapeDtypeStruct((t, n, D_MODEL), F32),
        compiler_params=_cparams("parallel"),
        name="rglru_sample",
    )(x, cs, h0, cw, cb, wa2, ba, wx2, bx, lam)


def _rwkv_prep_math(pc, prev, mu, w0, a0, kkp, ka, wwa, gup, e, et):
    d = D_MODEL
    ps = pc + mu * (prev - pc)
    r = ps[:, 0:d]
    k = ps[:, d:2 * d]
    v = ps[:, 2 * d:3 * d]
    l01 = ps[:, 3 * d:3 * d + LANES]
    lane = lax.broadcasted_iota(jnp.int32, (1, LANES), 1)
    t01 = jnp.where(lane < LORA_W, jnp.tanh(l01), l01)
    wa = _dot(t01, wwa)
    g = _dot(_sigmoid(ps[:, 3 * d + LANES:]), gup)
    logw = -math.exp(-0.5) * _sigmoid(w0 + wa[:, :d])
    a = _sigmoid(a0 + wa[:, d:])
    kk = k * kkp
    kkn = kk * lax.rsqrt(jnp.maximum(_head_sum(kk * kk, e, et), 1e-24))
    kmod = k * (1.0 + (a - 1.0) * ka)
    return r, kmod, v, logw, kkn, kkn * a, g


_P_R, _P_K, _P_V, _P_LW, _P_KK, _P_B, _P_G = range(7)
_N_PREP_OUT = 7


def _plane_spec(plane, rows, index):
    return pl.BlockSpec((None, rows, D_MODEL), lambda *g: (plane, index(*g), 0))


def _rwkv_front_kernel(x_ref, w_ref, sh0_ref, mu_ref, w0_ref, a0_ref, kkp_ref, ka_ref, wwa_ref, gup_ref,
                       e_ref, et_ref, out_ref, sh_out_ref, carry_ref, *, shift_rows):
    i = pl.program_id(1)
    tm = x_ref.shape[0]
    ncar = carry_ref.shape[0]

    @pl.when(i == 0)
    def _():
        carry_ref[...] = sh0_ref[0]

    pc = jnp.dot(x_ref[...].astype(BF16), w_ref[...], preferred_element_type=F32)
    prev = jnp.concatenate([carry_ref[ncar - shift_rows:, :], pc[:tm - shift_rows]], axis=0)
    outs = _rwkv_prep_math(pc, prev, mu_ref[...], w0_ref[...], a0_ref[...], kkp_ref[...], ka_ref[...],
                           wwa_ref[...], gup_ref[...], e_ref[...], et_ref[...])
    for j, o in enumerate(outs):
        out_ref[j] = o
    carry_ref[...] = pc[tm - ncar:]

    @pl.when(i == pl.num_programs(1) - 1)
    def _():
        sh_out_ref[0] = pc[tm - ncar:]


def _rwkv_front(x, w_rw, shift0, n_groups, shift_rows, params):
    rows = x.shape[0] // n_groups
    ncar = shift0.shape[1]
    tm = _tile(rows, 256)
    assert tm >= ncar >= shift_rows and tm % shift_rows == 0
    nt = rows // tm
    row = lambda g, i: (g * nt + i, 0)
    full = lambda g, i: (0, 0)
    grp = pl.BlockSpec((1, ncar, SHIFT_PAD), lambda g, i: (g, 0, 0))
    return pl.pallas_call(
        functools.partial(_rwkv_front_kernel, shift_rows=shift_rows),
        grid=(n_groups, nt),
        in_specs=[pl.BlockSpec((tm, D_MODEL), row),
                  pl.BlockSpec(w_rw.shape, full, pipeline_mode=pl.Buffered(1)), grp]
                 + [pl.BlockSpec(p.shape, full) for p in params],
        out_specs=[pl.BlockSpec((_N_PREP_OUT, tm, D_MODEL), lambda g, i: (0, g * nt + i, 0)), grp],
        out_shape=[jax.ShapeDtypeStruct((_N_PREP_OUT, n_groups * rows, D_MODEL), F32),
                   jax.ShapeDtypeStruct((n_groups, ncar, SHIFT_PAD), F32)],
        scratch_shapes=[pltpu.VMEM((ncar, SHIFT_PAD), F32)],
        compiler_params=_cparams("parallel", "arbitrary"),
        name="rwkv_front",
    )(x, w_rw, shift0, *params)


_PAIR_W = 2 * HEAD_DIM
_N_PAIRS = D_MODEL // _PAIR_W
_WKV_STATE_PASSES = 1


def _dot_tn(a, b):
    return lax.dot_general(a.astype(BF16), b.astype(BF16), (((0,), (0,)), ((), ())),
                           preferred_element_type=F32)


def _dot_nt(a, b):
    return lax.dot_general(a.astype(BF16), b.astype(BF16), (((1,), (1,)), ((), ())),
                           preferred_element_type=F32)


def _dot_passes(a, b, passes):
    if passes == 1:
        return _dot(a, b)
    ah, al = _split_terms(a, 2)
    bh, bl = _split_terms(b, 2)
    return (jnp.dot(ah, bh, preferred_element_type=F32) + jnp.dot(ah, bl, preferred_element_type=F32)
            + jnp.dot(al, bh, preferred_element_type=F32))


def _wkv_prompt_kernel(r_ref, k_ref, v_ref, lw_ref, kk_ref, b_ref, y_ref, s_ref, h_ref):
    i = pl.program_id(1)
    rows = r_ref.shape[0]
    c = WKV_CHUNK
    nck = rows // c
    hd = HEAD_DIM
    pw = _PAIR_W

    @pl.when(i == 0)
    def _():
        h_ref[...] = jnp.zeros_like(h_ref)

    row_t = lax.broadcasted_iota(jnp.int32, (rows, rows), 0)
    col_t = lax.broadcasted_iota(jnp.int32, (rows, rows), 1)
    tri = (row_t >= col_t) & (row_t // c == col_t // c)
    lw = lw_ref[...]
    cum = _dot_rhs_split(jnp.where(tri, 1.0, 0.0).astype(BF16), lw, 3)
    last = jnp.concatenate([jnp.broadcast_to(cum[(ci + 1) * c - 1:(ci + 1) * c], (c, D_MODEL))
                            for ci in range(nck)], axis=0)
    g_inv = jnp.exp(-cum)
    g_end = jnp.exp(last - cum)
    kk = kk_ref[...]
    b = b_ref[...]
    k = k_ref[...]
    at_all = -(kk * jnp.exp(cum - lw))
    rt_all = r_ref[...] * jnp.exp(cum)
    bt_all = b * g_inv
    kt_all = k * g_inv
    bh_all = b * g_end
    kh_all = k * g_end
    gc_all = jnp.exp(last)

    rowp = lax.broadcasted_iota(jnp.int32, (c, pw), 0)
    colp = lax.broadcasted_iota(jnp.int32, (c, pw), 1) % hd
    strict = rowp > colp
    incl = rowp >= colp
    eye_p = jnp.where(rowp == colp, 1.0, 0.0)
    left = lax.broadcasted_iota(jnp.int32, (1, pw), 1) < hd
    r2 = lax.broadcasted_iota(jnp.int32, (pw, pw), 0)
    c2 = lax.broadcasted_iota(jnp.int32, (pw, pw), 1)
    same_head = (r2 < hd) == (c2 < hd)
    eye2 = r2 == c2

    def bd(w):
        return jnp.concatenate([jnp.where(left, w, 0.0), jnp.where(left, 0.0, w)], axis=0)

    pairs = range(_N_PAIRS)
    items = [(ci, p) for ci in range(nck) for p in pairs]
    idx = range(len(items))
    sub = [(slice(ci * c, (ci + 1) * c), slice(p * pw, (p + 1) * pw)) for ci, p in items]
    v_all = v_ref[...]
    at = [at_all[s] for s in sub]
    rt = [rt_all[s] for s in sub]
    v = [v_all[s] for s in sub]
    amat = [_dot_nt(jnp.concatenate([at[j], rt[j]], axis=0),
                    jnp.concatenate([bd(bt_all[sub[j]]), bd(kt_all[sub[j]])], axis=0))
            for j in idx]
    a_ab = [jnp.where(strict, a[:c, :pw], 0.0) for a in amat]
    a_ak = [jnp.where(strict, a[:c, pw:], 0.0) for a in amat]
    a_rb = [jnp.where(incl, a[c:, :pw], 0.0) for a in amat]
    a_rk = [jnp.where(incl, a[c:, pw:], 0.0) for a in amat]
    tinv = [eye_p + a for a in a_ab]
    npow = [_dot(a, bd(a)) for a in a_ab]
    span = 4
    while span < c:
        both = [_dot(jnp.concatenate([tinv[j], npow[j]], axis=0), bd(npow[j])) for j in idx]
        tinv = [tinv[j] + both[j][:c] for j in idx]
        npow = [x[c:] for x in both]
        span *= 2
    tinv = [tinv[j] + _dot(tinv[j], bd(npow[j])) for j in idx]
    w12 = [_dot(jnp.concatenate([a_ak[j], a_rk[j]], axis=0), bd(v[j])) for j in idx]
    pq = [_dot(tinv[j], jnp.concatenate([bd(at[j]), bd(w12[j][:c])], axis=1)) for j in idx]
    y10 = [jnp.concatenate([rt[j], w12[j][c:]], axis=1)
           + _dot(a_rb[j], jnp.concatenate([bd(pq[j][:, :pw]), bd(pq[j][:, pw:])], axis=1))
           for j in idx]
    mn = [_dot_tn(bh_all[sub[j]], pq[j]) for j in idx]
    kv = [_dot_tn(kh_all[sub[j]], v[j]) for j in idx]
    m = [jnp.where(same_head, mn[j][:, :pw], 0.0)
         + jnp.where(eye2, gc_all[sub[j][0], sub[j][1]][:1], 0.0) for j in idx]
    n = [jnp.where(same_head, mn[j][:, pw:] + kv[j], 0.0) for j in idx]
    h = [h_ref[p] for p in pairs]
    y_rows = []
    for ci in range(nck):
        js = [ci * _N_PAIRS + p for p in pairs]
        my = [_dot_passes(jnp.concatenate([m[j], y10[j][:, :pw]], axis=0), h[p], _WKV_STATE_PASSES)
              for p, j in zip(pairs, js)]
        h = [my[p][:pw] + n[j] for p, j in zip(pairs, js)]
        y_rows.append(jnp.concatenate([my[p][pw:] + y10[j][:, pw:] for p, j in zip(pairs, js)], axis=1))
    for p in pairs:
        h_ref[p] = h[p]
    y_ref[...] = jnp.concatenate(y_rows, axis=0)

    @pl.when(i == pl.num_programs(1) - 1)
    def _():
        for p in pairs:
            s_ref[0, p] = h[p]


def _wkv_prompt(prep, nb_seq, seq):
    c = WKV_CHUNK * WKV_CHUNKS_PER_TILE
    assert WKV_CHUNK == HEAD_DIM and seq % c == 0
    nc = seq // c
    row = lambda bb, i: (bb * nc + i, 0)
    planes = (_P_R, _P_K, _P_V, _P_LW, _P_KK, _P_B)
    return pl.pallas_call(
        _wkv_prompt_kernel,
        grid=(nb_seq, nc),
        in_specs=[_plane_spec(p, c, lambda bb, i: bb * nc + i) for p in planes],
        out_specs=[pl.BlockSpec((c, D_MODEL), row),
                   pl.BlockSpec((1, _N_PAIRS, _PAIR_W, _PAIR_W), lambda bb, i: (bb, 0, 0, 0))],
        out_shape=[jax.ShapeDtypeStruct((nb_seq * seq, D_MODEL), F32),
                   jax.ShapeDtypeStruct((nb_seq, _N_PAIRS, _PAIR_W, _PAIR_W), F32)],
        scratch_shapes=[pltpu.VMEM((_N_PAIRS, _PAIR_W, _PAIR_W), F32)],
        compiler_params=_cparams("parallel", "arbitrary"),
        name="wkv_prompt",
    )(*([prep] * len(planes)))


def _unpair_state(hb):
    n = hb.shape[0]
    h6 = hb.reshape(n, _N_PAIRS, 2, HEAD_DIM, 2, HEAD_DIM)
    blocks = jnp.stack([h6[:, :, 0, :, 0, :], h6[:, :, 1, :, 1, :]], axis=2)
    return jnp.swapaxes(blocks, -1, -2).reshape(n, N_HEADS, HEAD_DIM, HEAD_DIM)


_WKV_VROWS = 4


def _wkv_sample_kernel(r_ref, lw_ref, k_ref, v_ref, kk_ref, b_ref, s0_ref, y_ref, s_ref, w_ref):
    t = r_ref.shape[0]
    vr = _WKV_VROWS
    w_ref[...] = jnp.exp(lw_ref[...])
    for v0 in range(0, HEAD_DIM, vr):
        s = [s0_ref[0, v0 + j] for j in range(vr)]
        for st in range(t):
            kk, w, b, k, r = kk_ref[st], w_ref[st], b_ref[st], k_ref[st], r_ref[st]
            vrow = v_ref[st, v0:v0 + vr, :]
            ys = []
            for j in range(vr):
                sa = -jnp.sum(s[j] * kk, axis=0, keepdims=True)
                s[j] = s[j] * w + sa * b + vrow[j:j + 1] * k
                ys.append(jnp.sum(s[j] * r, axis=0, keepdims=True))
            y_ref[st, v0:v0 + vr, :] = jnp.concatenate(ys, axis=0)
        for j in range(vr):
            s_ref[0, v0 + j] = s[j]


def _wkv_sample(vecs, s0):
    _, t, nh, hd, n = vecs.shape
    vec = lambda p: pl.BlockSpec((None, t, None, hd, n), lambda h: (p, 0, h, 0, 0))
    st = pl.BlockSpec((1, hd, hd, n), lambda h: (h, 0, 0, 0))
    planes = (_P_R, _P_LW, _P_K, _P_V, _P_KK, _P_B)
    return pl.pallas_call(
        _wkv_sample_kernel,
        grid=(nh,),
        in_specs=[vec(p) for p in planes] + [st],
        out_specs=[pl.BlockSpec((t, None, hd, n), lambda h: (0, h, 0, 0)), st],
        out_shape=[jax.ShapeDtypeStruct((t, nh, hd, n), F32),
                   jax.ShapeDtypeStruct((nh, hd, hd, n), F32)],
        scratch_shapes=[pltpu.VMEM((t, hd, n), F32)],
        compiler_params=_cparams("parallel"),
        name="wkv_sample",
    )(*([vecs] * len(planes)), s0)


def _mix_out_kernel(x_ref, oa_ref, ob_ref, y_ref, r_ref, k_ref, v_ref, g_ref, wgate_ref,
                    lnw_ref, lnb_ref, rk_ref, wout_ref, g1_ref, b1_ref, e_ref, et_ref, o_ref, *, alpha):
    d = D_MODEL
    x = x_ref[...]
    gate = jnp.dot(x.astype(BF16), wgate_ref[...], preferred_element_type=F32)
    e, et = e_ref[...], et_ref[...]
    y = y_ref[...]
    mu = _head_sum(y, e, et, 1.0 / HEAD_DIM)
    yc = y - mu
    var = _head_sum(yc * yc, e, et, 1.0 / HEAD_DIM)
    yn = yc * lax.rsqrt(var + GN_EPS) * lnw_ref[...] + lnb_ref[...]
    v = v_ref[...]
    bonus = _head_sum(r_ref[...] * k_ref[...] * rk_ref[...], e, et) * v
    oc = (yn + bonus) * g_ref[...]
    mixed = (_sigmoid(gate[:, 0:d]) * oa_ref[...] + _sigmoid(gate[:, d:2 * d]) * ob_ref[...]
             + _sigmoid(gate[:, 2 * d:3 * d]) * oc)
    z = alpha * x + jnp.dot(mixed.astype(BF16), wout_ref[...], preferred_element_type=F32)
    o_ref[...] = _layer_norm(z, g1_ref[...], b1_ref[...])


def _mix_out(x, oa, ob, y, prep, wgate, lnw, lnb, rk, wout, g1, b1, e, et, alpha):
    m = x.shape[0]
    tm = _tile(m, 256)
    row = lambda i: (i, 0)
    full = lambda i: (0, 0)
    rows = pl.BlockSpec((tm, D_MODEL), row)
    vec = pl.BlockSpec((1, D_MODEL), full)
    resident = lambda w: pl.BlockSpec(w.shape, full, pipeline_mode=pl.Buffered(1))
    planes = (_P_R, _P_K, _P_V, _P_G)
    return pl.pallas_call(
        functools.partial(_mix_out_kernel, alpha=alpha),
        grid=(m // tm,),
        in_specs=[rows] * 4 + [_plane_spec(p, tm, lambda i: i) for p in planes]
                 + [resident(wgate), vec, vec, vec, resident(wout), vec, vec,
                    pl.BlockSpec(e.shape, full), pl.BlockSpec(et.shape, full)],
        out_specs=rows,
        out_shape=jax.ShapeDtypeStruct((m, D_MODEL), F32),
        compiler_params=_cparams("parallel"),
        name="mix_out",
    )(x, oa, ob, y, *([prep] * len(planes)), wgate, lnw, lnb, rk, wout, g1, b1, e, et)


def _ffn_kernel(x_ref, wgu_ref, wd_ref, g2_ref, b2_ref, o_ref, *, alpha):
    dff = wd_ref.shape[0]
    x = x_ref[...]
    xb = x.astype(BF16)
    gt = jnp.dot(xb, wgu_ref[:, :dff], preferred_element_type=F32)
    up = jnp.dot(xb, wgu_ref[:, dff:], preferred_element_type=F32)
    ff = jnp.dot((gt * _sigmoid(gt) * up).astype(BF16), wd_ref[...], preferred_element_type=F32)
    o_ref[...] = _layer_norm(alpha * x + ff, g2_ref[...], b2_ref[...])


def _ffn(x, wgu, wd, g2, b2, alpha):
    m = x.shape[0]
    tm = _tile(m, 256)
    row = lambda i: (i, 0)
    full = lambda i: (0, 0)
    resident = lambda w: pl.BlockSpec(w.shape, full, pipeline_mode=pl.Buffered(1))
    return pl.pallas_call(
        functools.partial(_ffn_kernel, alpha=alpha),
        grid=(m // tm,),
        in_specs=[pl.BlockSpec((tm, D_MODEL), row), resident(wgu), resident(wd),
                  pl.BlockSpec((1, D_MODEL), full), pl.BlockSpec((1, D_MODEL), full)],
        out_specs=pl.BlockSpec((tm, D_MODEL), row),
        out_shape=jax.ShapeDtypeStruct((m, D_MODEL), F32),
        compiler_params=_cparams("parallel"),
        name="ffn",
    )(x, wgu, wd, g2, b2)


def _blockdiag_pairs(w):
    nb, bs, _ = w.shape
    w = w.reshape(nb // 2, 2, bs, bs)
    z = jnp.zeros((nb // 2, bs, bs), w.dtype)
    top = jnp.concatenate([w[:, 0], z], axis=2)
    bot = jnp.concatenate([z, w[:, 1]], axis=2)
    return jnp.concatenate([top, bot], axis=1)


def _row(p):
    return p.reshape(1, -1)


def kernel(x_prompt, x_sample, cache_k, cache_v, state_conv, state_rglru, state_shift, state_wkv, w_in, attn_sinks, conv_w, conv_b, rg_wa, rg_ba, rg_wx, rg_bx, rg_lambda, rw_mu, rw_w0, rw_wup, rw_a0, rw_aup, rw_gup, rw_kk, rw_ka, rw_rk, rw_lnw, rw_lnb, w_out, ln1_g, ln1_b, w_gu, w_down, ln2_g, ln2_b):
    depth = w_in.shape[0]
    bp, seq, d = x_prompt.shape
    ns, ts, _ = x_sample.shape
    wb = cache_k.shape[2]
    alpha = (2 * depth) ** 0.25
    dff = w_down.shape[1]
    pad = SHIFT_PAD - SHIFT_W

    xp = x_prompt.reshape(bp * seq, d)
    xs = jnp.swapaxes(x_sample, 0, 1).reshape(ts * ns, d)
    st_p = [[] for _ in range(6)]
    st_s = [[] for _ in range(6)]
    head_e, head_et = _head_indicator()
    w_in_bf = w_in.astype(BF16)
    w_gu_bf = w_gu.astype(BF16)
    w_down_bf = w_down.astype(BF16)

    for l in range(depth):
        wl = w_in_bf[l]
        w_all = wl[:, :OFF_RW]
        w_rw = jnp.pad(wl[:, OFF_RW:OFF_GATE], ((0, 0), (0, pad)))
        w_gate = wl[:, OFF_GATE:]
        wa2 = _blockdiag_pairs(rg_wa[l]).astype(BF16)
        wx2 = _blockdiag_pairs(rg_wx[l]).astype(BF16)
        rg_args = (conv_w[l], _row(conv_b[l]), wa2, _row(rg_ba[l]), wx2, _row(rg_bx[l]), _row(rg_lambda[l]))
        zw = jnp.zeros((LORA_W, d), F32)
        wwa = jnp.concatenate([jnp.concatenate([rw_wup[l], zw], axis=1),
                               jnp.concatenate([zw, rw_aup[l]], axis=1)], axis=0).astype(BF16)
        gup = jnp.pad(rw_gup[l], ((0, 2 * LANES - LORA_G), (0, 0))).astype(BF16)
        prep_params = (_row(jnp.pad(rw_mu[l], (0, pad))), _row(rw_w0[l]), _row(rw_a0[l]), _row(rw_kk[l]),
                       _row(rw_ka[l]), wwa, gup, head_e, head_et)
        mix_params = (_row(rw_lnw[l]), _row(rw_lnb[l]), _row(rw_rk[l]), w_out[l].astype(BF16),
                      _row(ln1_g[l]), _row(ln1_b[l]), head_e, head_et)
        ffn_params = (w_gu_bf[l], w_down_bf[l], _row(ln2_g[l]), _row(ln2_b[l]))
        sinks = attn_sinks[l]

        qkv, ob, rnn_tail = _proj_rglru_prompt(xp, w_all, bp, seq, *rg_args)
        oa = _attn_prompt(qkv, sinks, bp, seq)
        prep, sh_p = _rwkv_front(xp, w_rw, jnp.zeros((bp, 8, SHIFT_PAD), F32), bp, 1, prep_params)
        y, hfin = _wkv_prompt(prep, bp, seq)
        x1 = _mix_out(xp, oa, ob, y, prep, w_gate, *mix_params, alpha)
        xp = _ffn(x1, *ffn_params, alpha)
        qkv3 = qkv.reshape(bp, seq, OFF_RNN)
        st_p[0].append(qkv3[:, seq - wb:, OFF_K:OFF_V].reshape(bp, wb, N_KV_HEADS, HEAD_DIM))
        st_p[1].append(qkv3[:, seq - wb:, OFF_V:OFF_RNN].reshape(bp, wb, N_KV_HEADS, HEAD_DIM))
        st_p[2].append(rnn_tail[:, 8 - (CONV_W - 1):])
        st_p[3].append(ob.reshape(bp, seq, d)[:, seq - 1])
        st_p[4].append(sh_p[:, 7, :SHIFT_W])
        st_p[5].append(_unpair_state(hfin))

        qkv, rnn = _in_proj(xs, w_all)
        qkv_n = jnp.swapaxes(qkv.reshape(ts, ns, OFF_RNN), 0, 1)
        oa_n, new_k, new_v = _attn_sample(qkv_n[:, :, :OFF_K], qkv_n[:, :, OFF_K:OFF_V], qkv_n[:, :, OFF_V:],
                                          cache_k[l].reshape(ns, wb, KV_W), cache_v[l].reshape(ns, wb, KV_W),
                                          sinks)
        oa = jnp.swapaxes(oa_n, 0, 1).reshape(ts * ns, d)
        rnn3 = rnn.reshape(ts, ns, d)
        ob3 = _rglru_sample(rnn3, jnp.swapaxes(state_conv[l], 0, 1), state_rglru[l], *rg_args)
        ob = ob3.reshape(ts * ns, d)
        shift = jnp.pad(state_shift[l], ((0, 0), (0, pad)))[None]
        prep, sh_s = _rwkv_front(xs, w_rw, shift, 1, ns, prep_params)
        vecs = jnp.swapaxes(prep.reshape(_N_PREP_OUT, ts, ns, d), 2, 3).reshape(
            _N_PREP_OUT, ts, N_HEADS, HEAD_DIM, ns)
        s0 = jnp.transpose(state_wkv[l], (1, 2, 3, 0))
        y4, s_new = _wkv_sample(vecs, s0)
        y = jnp.swapaxes(y4.reshape(ts, d, ns), 1, 2).reshape(ts * ns, d)
        x1 = _mix_out(xs, oa, ob, y, prep, w_gate, *mix_params, alpha)
        xs = _ffn(x1, *ffn_params, alpha)
        st_s[0].append(new_k.reshape(ns, wb, N_KV_HEADS, HEAD_DIM))
        st_s[1].append(new_v.reshape(ns, wb, N_KV_HEADS, HEAD_DIM))
        st_s[2].append(jnp.swapaxes(rnn3[ts - (CONV_W - 1):], 0, 1))
        st_s[3].append(ob3[ts - 1])
        st_s[4].append(sh_s[0, :, :SHIFT_W])
        st_s[5].append(jnp.transpose(s_new, (3, 0, 1, 2)))

    yp = xp.reshape(bp, seq, d)
    ys = jnp.swapaxes(xs.reshape(ts, ns, d), 0, 1)
    return (yp, ys, *(jnp.stack(s) for s in st_p), *(jnp.stack(s) for s in st_s))
```

```python
import functools
import math

import jax
import jax.numpy as jnp
from jax import lax
from jax.experimental import pallas as pl
from jax.experimental.pallas import tpu as pltpu

F32 = jnp.float32
BF16 = jnp.bfloat16

D_MODEL = 1024
HEAD_DIM = 64
N_HEADS = 16
N_KV_HEADS = 4
KV_W = N_KV_HEADS * HEAD_DIM
WINDOW = 128
CONV_W = 4
RG_C = 8.0
LORA_W = 64
LORA_A = 64
LORA_G = 160
SHIFT_W = 3 * D_MODEL + LORA_W + LORA_A + LORA_G
LANES = 128
SHIFT_PAD = -(-SHIFT_W // LANES) * LANES
GN_EPS = 64e-5
LN_EPS = 1e-5
OFF_K = D_MODEL
OFF_V = OFF_K + KV_W
OFF_RNN = OFF_V + KV_W
OFF_RW = OFF_RNN + D_MODEL
OFF_GATE = OFF_RW + SHIFT_W
WKV_CHUNK = 64
WKV_CHUNKS_PER_TILE = 4
VMEM_LIMIT = 48 * 1024 * 1024


def _cparams(*sem):
    return pltpu.CompilerParams(dimension_semantics=sem, vmem_limit_bytes=VMEM_LIMIT)


def _tile(n, pref):
    if n <= pref:
        return n
    for t in range(pref, 7, -1):
        if n % t == 0 and t % 8 == 0:
            return t
    return n


def _dot(a, b):
    return jnp.dot(a.astype(BF16), b.astype(BF16), preferred_element_type=F32)


def _split_terms(x, n):
    terms = []
    rem = x
    for _ in range(n):
        hi = rem.astype(BF16)
        terms.append(hi)
        rem = rem - hi.astype(F32)
    return terms


def _dot_rhs_split(w_bf16, x, n):
    return jnp.dot(jnp.concatenate([w_bf16] * n, axis=1), jnp.concatenate(_split_terms(x, n), axis=0),
                   preferred_element_type=F32)


def _sigmoid(x):
    return 0.5 * jnp.tanh(0.5 * x) + 0.5


def _softplus(x):
    return jnp.maximum(x, 0.0) + jnp.log(1.0 + jnp.exp(-jnp.abs(x)))


def _layer_norm(z, g, b):
    mu = jnp.mean(z, axis=-1, keepdims=True)
    zc = z - mu
    var = jnp.mean(zc * zc, axis=-1, keepdims=True)
    return zc * lax.rsqrt(var + LN_EPS) * g + b


def _head_indicator():
    c = lax.broadcasted_iota(jnp.int32, (D_MODEL, LANES), 0) // HEAD_DIM
    h = lax.broadcasted_iota(jnp.int32, (D_MODEL, LANES), 1)
    e = jnp.where(c == h, 1.0, 0.0).astype(BF16)
    ct = lax.broadcasted_iota(jnp.int32, (LANES, D_MODEL), 1) // HEAD_DIM
    ht = lax.broadcasted_iota(jnp.int32, (LANES, D_MODEL), 0)
    et = jnp.where(ct == ht, 1.0, 0.0).astype(BF16)
    return e, jnp.concatenate([et, et], axis=0)


def _head_sum(x, e, et2, scale=None):
    s = jnp.dot(x.astype(BF16), e, preferred_element_type=F32)
    if scale is not None:
        s = s * scale
    hi = s.astype(BF16)
    lo = (s - hi.astype(F32)).astype(BF16)
    return jnp.dot(jnp.concatenate([hi, lo], axis=1), et2, preferred_element_type=F32)


_PROJ_WIDTHS = (OFF_RNN, D_MODEL)


def _in_proj_kernel(x_ref, w_ref, *o_refs):
    xb = x_ref[...].astype(BF16)
    off = 0
    for o_ref in o_refs:
        n = o_ref.shape[1]
        o_ref[...] = jnp.dot(xb, w_ref[:, off:off + n], preferred_element_type=F32)
        off += n


def _in_proj(x, w):
    m, k = x.shape
    tm = _tile(m, 256)
    return pl.pallas_call(
        _in_proj_kernel,
        grid=(m // tm,),
        in_specs=[pl.BlockSpec((tm, k), lambda i: (i, 0)),
                  pl.BlockSpec(w.shape, lambda i: (0, 0), pipeline_mode=pl.Buffered(1))],
        out_specs=[pl.BlockSpec((tm, n), lambda i: (i, 0)) for n in _PROJ_WIDTHS],
        out_shape=[jax.ShapeDtypeStruct((m, n), F32) for n in _PROJ_WIDTHS],
        compiler_params=_cparams("parallel"),
        name="in_proj",
    )(x, w)


_LOG2E = math.log2(math.e)


def _alibi_slope(h):
    return 2.0 ** (-8.0 * (h + 1) / N_HEADS)


def _attn_prompt_bias():
    w = WINDOW
    rel = w + jnp.arange(w)[:, None] - jnp.arange(2 * w)[None, :]
    band = (rel >= 0) & (rel < WINDOW)
    has_prev = jnp.arange(2)[:, None, None] > 0
    valid = band[None] & (has_prev | (jnp.arange(2 * w) >= w)[None, None, :])
    slopes = jnp.array([_alibi_slope(h) for h in range(N_HEADS)], F32)
    bias = -slopes[None, :, None, None] * rel.astype(F32)[None, None]
    return jnp.where(valid[:, None], bias * _LOG2E, -jnp.inf)


_ATTN_BLOCKS_PER_STEP = 2


def _attn_prompt_kernel(sink_ref, bias_ref, q_ref, kc_ref, vc_ref, kp_ref, vp_ref, o_ref):
    j = pl.program_id(1)
    w = WINDOW
    nblk = q_ref.shape[0] // w
    q = q_ref[...] * (HEAD_DIM ** -0.5 * _LOG2E)
    kall = jnp.concatenate([kp_ref[...], kc_ref[...]], axis=0).astype(BF16)
    vall = jnp.concatenate([vp_ref[...], vc_ref[...]], axis=0).astype(BF16)
    first = jnp.minimum(j, 1)
    gsz = N_HEADS // N_KV_HEADS
    items = [(n, h) for n in range(nblk) for h in range(N_HEADS)]
    kg = {(n, g): kall[n * w:(n + 2) * w, g * HEAD_DIM:(g + 1) * HEAD_DIM]
          for n in range(nblk) for g in range(N_KV_HEADS)}
    vg = {(n, g): vall[n * w:(n + 2) * w, g * HEAD_DIM:(g + 1) * HEAD_DIM]
          for n in range(nblk) for g in range(N_KV_HEADS)}
    qh = [q[n * w:(n + 1) * w, h * HEAD_DIM:(h + 1) * HEAD_DIM].astype(BF16) for n, h in items]
    bias = [bias_ref[first, h] if n == 0 else bias_ref[1, h] for n, h in items]
    s = [lax.dot_general(qh[i], kg[(n, h // gsz)], (((1,), (1,)), ((), ())), preferred_element_type=F32)
         + bias[i] for i, (n, h) in enumerate(items)]
    sink = [sink_ref[h] * _LOG2E for n, h in items]
    m = [jnp.maximum(jnp.max(s[i], axis=-1, keepdims=True), sink[i]) for i in range(len(items))]
    e = [jnp.exp2(s[i] - m[i]) for i in range(len(items))]
    den = [jnp.sum(e[i], axis=-1, keepdims=True) + jnp.exp2(sink[i] - m[i]) for i in range(len(items))]
    p = [(e[i] / den[i]).astype(BF16) for i in range(len(items))]
    outs = [jnp.dot(p[i], vg[(n, h // gsz)], preferred_element_type=F32) for i, (n, h) in enumerate(items)]
    o_ref[...] = jnp.concatenate(
        [jnp.concatenate(outs[n * N_HEADS:(n + 1) * N_HEADS], axis=1) for n in range(nblk)], axis=0)


def _attn_prompt(qkv, sinks, nb_seq, seq):
    w = WINDOW
    nblk = _ATTN_BLOCKS_PER_STEP if seq % (_ATTN_BLOCKS_PER_STEP * w) == 0 else 1
    tq = nblk * w
    nb = seq // tq
    kcol = OFF_K // KV_W
    vcol = OFF_V // KV_W
    bias = _attn_prompt_bias()

    def cur(c):
        return lambda b, j: (b * nb + j, c)

    def prev(c):
        return lambda b, j: (jnp.maximum((b * nb + j) * nblk - 1, 0), c)

    return pl.pallas_call(
        _attn_prompt_kernel,
        grid=(nb_seq, nb),
        in_specs=[pl.BlockSpec(memory_space=pltpu.SMEM),
                  pl.BlockSpec(bias.shape, lambda b, j: (0, 0, 0, 0), pipeline_mode=pl.Buffered(1)),
                  pl.BlockSpec((tq, D_MODEL), cur(0)),
                  pl.BlockSpec((tq, KV_W), cur(kcol)),
                  pl.BlockSpec((tq, KV_W), cur(vcol)),
                  pl.BlockSpec((w, KV_W), prev(kcol)),
                  pl.BlockSpec((w, KV_W), prev(vcol))],
        out_specs=pl.BlockSpec((tq, D_MODEL), cur(0)),
        out_shape=jax.ShapeDtypeStruct((nb_seq * seq, D_MODEL), F32),
        compiler_params=_cparams("parallel", "parallel"),
        name="attn_prompt",
    )(sinks, bias, qkv, qkv, qkv, qkv, qkv)


def _attn_sample_kernel(sink_ref, q_ref, kn_ref, vn_ref, ck_ref, cv_ref, o_ref, nk_ref, nv_ref, *, t):
    sn = q_ref.shape[0]
    wb = ck_ref.shape[1]
    gsz = N_HEADS // N_KV_HEADS
    rows = gsz * t
    q = q_ref[...]
    kn = kn_ref[...]
    vn = vn_ref[...]
    ck = ck_ref[...]
    cv = cv_ref[...]
    tq1 = lax.broadcasted_iota(jnp.int32, (rows, wb), 0) % t
    j1 = lax.broadcasted_iota(jnp.int32, (rows, wb), 1)
    rel1 = wb + tq1 - j1
    valid1 = (rel1 >= 0) & (rel1 < WINDOW)
    tq2 = lax.broadcasted_iota(jnp.int32, (rows, t), 0) % t
    s2i = lax.broadcasted_iota(jnp.int32, (rows, t), 1)
    rel2 = tq2 - s2i
    valid2 = (rel2 >= 0) & (rel2 < WINDOW)
    hrow = lax.broadcasted_iota(jnp.int32, (rows, 1), 0) // t
    outs = []
    for g in range(N_KV_HEADS):
        slope = jnp.zeros((rows, 1), F32)
        sink = jnp.zeros((rows, 1), F32)
        for hh in range(gsz):
            h = g * gsz + hh
            slope = jnp.where(hrow == hh, _alibi_slope(h), slope)
            sink = jnp.where(hrow == hh, sink_ref[h], sink)
        qg = jnp.concatenate([q[:, :, (g * gsz + hh) * HEAD_DIM:(g * gsz + hh + 1) * HEAD_DIM]
                              for hh in range(gsz)], axis=1).astype(BF16)
        lo, hi = g * HEAD_DIM, (g + 1) * HEAD_DIM
        ckg = ck[:, :, lo:hi].astype(BF16)
        cvg = cv[:, :, lo:hi].astype(BF16)
        kng = kn[:, :, lo:hi].astype(BF16)
        vng = vn[:, :, lo:hi].astype(BF16)
        nt = (((2,), (2,)), ((0,), (0,)))
        nn = (((2,), (1,)), ((0,), (0,)))
        s1 = lax.dot_general(qg, ckg, nt, preferred_element_type=F32) * (HEAD_DIM ** -0.5)
        s2 = lax.dot_general(qg, kng, nt, preferred_element_type=F32) * (HEAD_DIM ** -0.5)
        s1 = jnp.where(valid1[None], s1 - (slope * rel1.astype(F32))[None], -jnp.inf)
        s2 = jnp.where(valid2[None], s2 - (slope * rel2.astype(F32))[None], -jnp.inf)
        m = jnp.maximum(jnp.maximum(jnp.max(s1, axis=-1, keepdims=True),
                                    jnp.max(s2, axis=-1, keepdims=True)), sink[None])
        e1 = jnp.exp(s1 - m)
        e2 = jnp.exp(s2 - m)
        den = (jnp.sum(e1, axis=-1, keepdims=True) + jnp.sum(e2, axis=-1, keepdims=True)
               + jnp.exp(sink[None] - m))
        p1 = (e1 / den).astype(BF16)
        p2 = (e2 / den).astype(BF16)
        og = (lax.dot_general(p1, cvg, nn, preferred_element_type=F32)
              + lax.dot_general(p2, vng, nn, preferred_element_type=F32))
        outs.extend(og[:, hh * t:(hh + 1) * t, :] for hh in range(gsz))
    o_ref[...] = jnp.concatenate(outs, axis=2)
    nk_ref[...] = jnp.concatenate([ck[:, t:, :], kn], axis=1)
    nv_ref[...] = jnp.concatenate([cv[:, t:, :], vn], axis=1)


def _attn_sample(q, kn, vn, ck, cv, sinks):
    n, t, _ = q.shape
    wb = ck.shape[1]
    sn = _tile(n, 16)
    blk = lambda d1, d2: pl.BlockSpec((sn, d1, d2), lambda i: (i, 0, 0))
    return pl.pallas_call(
        functools.partial(_attn_sample_kernel, t=t),
        grid=(n // sn,),
        in_specs=[pl.BlockSpec(memory_space=pltpu.SMEM),
                  blk(t, D_MODEL), blk(t, KV_W), blk(t, KV_W), blk(wb, KV_W), blk(wb, KV_W)],
        out_specs=[blk(t, D_MODEL), blk(wb, KV_W), blk(wb, KV_W)],
        out_shape=[jax.ShapeDtypeStruct((n, t, D_MODEL), F32),
                   jax.ShapeDtypeStruct((n, wb, KV_W), F32),
                   jax.ShapeDtypeStruct((n, wb, KV_W), F32)],
        compiler_params=_cparams("parallel"),
        name="attn_sample",
    )(sinks, q, kn, vn, ck, cv)


def _rglru_gates(xc, wa_ref, ba, wx_ref, bx, lam):
    xb = xc.astype(BF16)
    npair = D_MODEL // LANES
    ga = jnp.concatenate([jnp.dot(xb[:, j * LANES:(j + 1) * LANES], wa_ref[j], preferred_element_type=F32)
                          for j in range(npair)], axis=1)
    gx = jnp.concatenate([jnp.dot(xb[:, j * LANES:(j + 1) * LANES], wx_ref[j], preferred_element_type=F32)
                          for j in range(npair)], axis=1)
    r = _sigmoid(ga + ba)
    ig = _sigmoid(gx + bx)
    z = r * (RG_C * _softplus(-lam))
    a = jnp.exp(-z)
    u = jnp.sqrt(jnp.tanh(z) * (a * a + 1.0)) * (ig * xc)
    return a, u


def _proj_rglru_prompt_kernel(xin_ref, w_ref, cw_ref, cb_ref, wa_ref, ba_ref, wx_ref, bx_ref, lam_ref,
                              qkv_ref, h_ref, tail_out_ref, tail_ref, hc_ref):
    i = pl.program_id(1)
    tc = xin_ref.shape[0]

    @pl.when(i == 0)
    def _():
        tail_ref[...] = jnp.zeros_like(tail_ref)
        hc_ref[...] = jnp.zeros_like(hc_ref)

    xb = xin_ref[...].astype(BF16)
    qkv_ref[...] = jnp.dot(xb, w_ref[:, :OFF_RNN], preferred_element_type=F32)
    x = jnp.dot(xb, w_ref[:, OFF_RNN:], preferred_element_type=F32)
    xp = jnp.concatenate([tail_ref[...], x], axis=0)
    cw = cw_ref[...]
    xc = cb_ref[...] + x * cw[CONV_W - 1:CONV_W]
    for d in range(1, CONV_W):
        xc = xc + xp[8 - d:8 - d + tc] * cw[CONV_W - 1 - d:CONV_W - d]
    a, u = _rglru_gates(xc, wa_ref, ba_ref[...], wx_ref, bx_ref[...], lam_ref[...])
    ng = tc // 8
    acc_a = a.reshape(ng, 8, D_MODEL)
    acc_h = u.reshape(ng, 8, D_MODEL)
    sub = lax.broadcasted_iota(jnp.int32, (1, 8, D_MODEL), 1)
    for d in (1, 2, 4):
        keep = sub >= d
        sh_a = jnp.where(keep, pltpu.roll(acc_a, d, axis=1), 1.0)
        sh_h = jnp.where(keep, pltpu.roll(acc_h, d, axis=1), 0.0)
        acc_h = acc_h + acc_a * sh_h
        acc_a = acc_a * sh_a
    carry = hc_ref[...]
    for g in range(ng):
        hg = acc_h[g] + acc_a[g] * carry
        h_ref[g * 8:(g + 1) * 8, :] = hg
        carry = hg[7:8]
    hc_ref[...] = carry
    tail_ref[...] = x[tc - 8:tc]

    @pl.when(i == pl.num_programs(1) - 1)
    def _():
        tail_out_ref[0] = x[tc - 8:tc]


def _proj_rglru_prompt(x, w, nb_seq, seq, cw, cb, wa2, ba, wx2, bx, lam):
    tc = _tile(seq, 256)
    nt = seq // tc
    row = lambda b, i: (b * nt + i, 0)
    full2 = lambda b, i: (0, 0)
    full3 = lambda b, i: (0, 0, 0)
    return pl.pallas_call(
        _proj_rglru_prompt_kernel,
        grid=(nb_seq, nt),
        in_specs=[pl.BlockSpec((tc, D_MODEL), row),
                  pl.BlockSpec(w.shape, full2, pipeline_mode=pl.Buffered(1)),
                  pl.BlockSpec((CONV_W, D_MODEL), full2), pl.BlockSpec((1, D_MODEL), full2),
                  pl.BlockSpec(wa2.shape, full3), pl.BlockSpec((1, D_MODEL), full2),
                  pl.BlockSpec(wx2.shape, full3), pl.BlockSpec((1, D_MODEL), full2),
                  pl.BlockSpec((1, D_MODEL), full2)],
        out_specs=[pl.BlockSpec((tc, OFF_RNN), row), pl.BlockSpec((tc, D_MODEL), row),
                   pl.BlockSpec((1, 8, D_MODEL), lambda b, i: (b, 0, 0))],
        out_shape=[jax.ShapeDtypeStruct((nb_seq * seq, OFF_RNN), F32),
                   jax.ShapeDtypeStruct((nb_seq * seq, D_MODEL), F32),
                   jax.ShapeDtypeStruct((nb_seq, 8, D_MODEL), F32)],
        scratch_shapes=[pltpu.VMEM((8, D_MODEL), F32), pltpu.VMEM((1, D_MODEL), F32)],
        compiler_params=_cparams("parallel", "arbitrary"),
        name="proj_rglru_prompt",
    )(x, w, cw, cb, wa2, ba, wx2, bx, lam)


def _rglru_sample_kernel(x_ref, cs_ref, h0_ref, cw_ref, cb_ref, wa_ref, ba_ref, wx_ref, bx_ref, lam_ref,
                         h_ref):
    t, sn, _ = x_ref.shape
    cw = cw_ref[...]
    nst = CONV_W - 1

    def slab(s):
        return x_ref[s] if s >= 0 else cs_ref[nst + s]

    xcs = []
    for s in range(t):
        xc = cb_ref[...] + slab(s) * cw[CONV_W - 1:CONV_W]
        for d in range(1, CONV_W):
            xc = xc + slab(s - d) * cw[CONV_W - 1 - d:CONV_W - d]
        xcs.append(xc)
    xc_all = jnp.concatenate(xcs, axis=0)
    a, u = _rglru_gates(xc_all, wa_ref, ba_ref[...], wx_ref, bx_ref[...], lam_ref[...])
    h = h0_ref[...]
    for s in range(t):
        h = a[s * sn:(s + 1) * sn] * h + u[s * sn:(s + 1) * sn]
        h_ref[s] = h


def _rglru_sample(x, cs, h0, cw, cb, wa2, ba, wx2, bx, lam):
    t, n, _ = x.shape
    sn = _tile(n, 32)
    full2 = lambda i: (0, 0)
    full3 = lambda i: (0, 0, 0)
    return pl.pallas_call(
        _rglru_sample_kernel,
        grid=(n // sn,),
        in_specs=[pl.BlockSpec((t, sn, D_MODEL), lambda i: (0, i, 0)),
                  pl.BlockSpec((CONV_W - 1, sn, D_MODEL), lambda i: (0, i, 0)),
                  pl.BlockSpec((sn, D_MODEL), lambda i: (i, 0)),
                  pl.BlockSpec((CONV_W, D_MODEL), full2), pl.BlockSpec((1, D_MODEL), full2),
                  pl.BlockSpec(wa2.shape, full3), pl.BlockSpec((1, D_MODEL), full2),
                  pl.BlockSpec(wx2.shape, full3), pl.BlockSpec((1, D_MODEL), full2),
                  pl.BlockSpec((1, D_MODEL), full2)],
        out_specs=pl.BlockSpec((t, sn, D_MODEL), lambda i: (0, i, 0)),
        out_shape=jax.ShapeDtypeStruct((t, n, D_MODEL), F32),
        compiler_params=_cparams("parallel"),
        name="rglru_sample",
    )(x, cs, h0, cw, cb, wa2, ba, wx2, bx, lam)


def _rwkv_prep_math(pc, prev, mu, w0, a0, kkp, ka, wwa, gup, e, et):
    d = D_MODEL
    ps = pc + mu * (prev - pc)
    r = ps[:, 0:d]
    k = ps[:, d:2 * d]
    v = ps[:, 2 * d:3 * d]
    l01 = ps[:, 3 * d:3 * d + LANES]
    lane = lax.broadcasted_iota(jnp.int32, (1, LANES), 1)
    t01 = jnp.where(lane < LORA_W, jnp.tanh(l01), l01)
    wa = _dot(t01, wwa)
    g = _dot(_sigmoid(ps[:, 3 * d + LANES:]), gup)
    logw = -math.exp(-0.5) * _sigmoid(w0 + wa[:, :d])
    a = _sigmoid(a0 + wa[:, d:])
    kk = k * kkp
    kkn = kk * lax.rsqrt(jnp.maximum(_head_sum(kk * kk, e, et), 1e-24))
    kmod = k * (1.0 + (a - 1.0) * ka)
    return r, kmod, v, logw, kkn, kkn * a, g


_P_R, _P_K, _P_V, _P_LW, _P_KK, _P_B, _P_G = range(7)
_N_PREP_OUT = 7


def _plane_spec(plane, rows, index):
    return pl.BlockSpec((None, rows, D_MODEL), lambda *g: (plane, index(*g), 0))


def _rwkv_front_kernel(x_ref, w_ref, sh0_ref, mu_ref, w0_ref, a0_ref, kkp_ref, ka_ref, wwa_ref, gup_ref,
                       e_ref, et_ref, out_ref, sh_out_ref, carry_ref, *, shift_rows):
    i = pl.program_id(1)
    tm = x_ref.shape[0]
    ncar = carry_ref.shape[0]

    @pl.when(i == 0)
    def _():
        carry_ref[...] = sh0_ref[0]

    pc = jnp.dot(x_ref[...].astype(BF16), w_ref[...], preferred_element_type=F32)
    prev = jnp.concatenate([carry_ref[ncar - shift_rows:, :], pc[:tm - shift_rows]], axis=0)
    outs = _rwkv_prep_math(pc, prev, mu_ref[...], w0_ref[...], a0_ref[...], kkp_ref[...], ka_ref[...],
                           wwa_ref[...], gup_ref[...], e_ref[...], et_ref[...])
    for j, o in enumerate(outs):
        out_ref[j] = o
    carry_ref[...] = pc[tm - ncar:]

    @pl.when(i == pl.num_programs(1) - 1)
    def _():
        sh_out_ref[0] = pc[tm - ncar:]


def _rwkv_front(x, w_rw, shift0, n_groups, shift_rows, params):
    rows = x.shape[0] // n_groups
    ncar = shift0.shape[1]
    tm = _tile(rows, 256)
    assert tm >= ncar >= shift_rows and tm % shift_rows == 0
    nt = rows // tm
    row = lambda g, i: (g * nt + i, 0)
    full = lambda g, i: (0, 0)
    grp = pl.BlockSpec((1, ncar, SHIFT_PAD), lambda g, i: (g, 0, 0))
    return pl.pallas_call(
        functools.partial(_rwkv_front_kernel, shift_rows=shift_rows),
        grid=(n_groups, nt),
        in_specs=[pl.BlockSpec((tm, D_MODEL), row),
                  pl.BlockSpec(w_rw.shape, full, pipeline_mode=pl.Buffered(1)), grp]
                 + [pl.BlockSpec(p.shape, full) for p in params],
        out_specs=[pl.BlockSpec((_N_PREP_OUT, tm, D_MODEL), lambda g, i: (0, g * nt + i, 0)), grp],
        out_shape=[jax.ShapeDtypeStruct((_N_PREP_OUT, n_groups * rows, D_MODEL), F32),
                   jax.ShapeDtypeStruct((n_groups, ncar, SHIFT_PAD), F32)],
        scratch_shapes=[pltpu.VMEM((ncar, SHIFT_PAD), F32)],
        compiler_params=_cparams("parallel", "arbitrary"),
        name="rwkv_front",
    )(x, w_rw, shift0, *params)


_PAIR_W = 2 * HEAD_DIM
_N_PAIRS = D_MODEL // _PAIR_W


def _dot_tn(a, b):
    return lax.dot_general(a.astype(BF16), b.astype(BF16), (((0,), (0,)), ((), ())),
                           preferred_element_type=F32)


def _dot_nt(a, b):
    return lax.dot_general(a.astype(BF16), b.astype(BF16), (((1,), (1,)), ((), ())),
                           preferred_element_type=F32)


def _wkv_prompt_kernel(r_ref, k_ref, v_ref, lw_ref, kk_ref, b_ref, y_ref, s_ref, h_ref):
    i = pl.program_id(1)
    rows = r_ref.shape[0]
    c = WKV_CHUNK
    nck = rows // c
    hd = HEAD_DIM
    pw = _PAIR_W

    @pl.when(i == 0)
    def _():
        h_ref[...] = jnp.zeros_like(h_ref)

    row_t = lax.broadcasted_iota(jnp.int32, (rows, rows), 0)
    col_t = lax.broadcasted_iota(jnp.int32, (rows, rows), 1)
    tri = (row_t >= col_t) & (row_t // c == col_t // c)
    lw = lw_ref[...]
    cum = _dot_rhs_split(jnp.where(tri, 1.0, 0.0).astype(BF16), lw, 3)
    last = jnp.concatenate([jnp.broadcast_to(cum[(ci + 1) * c - 1:(ci + 1) * c], (c, D_MODEL))
                            for ci in range(nck)], axis=0)
    g_inv = jnp.exp(-cum)
    g_end = jnp.exp(last - cum)
    kk = kk_ref[...]
    b = b_ref[...]
    k = k_ref[...]
    at_all = -(kk * jnp.exp(cum - lw))
    rt_all = r_ref[...] * jnp.exp(cum)
    bt_all = b * g_inv
    kt_all = k * g_inv
    bh_all = b * g_end
    kh_all = k * g_end
    gc_all = jnp.exp(last)

    rowp = lax.broadcasted_iota(jnp.int32, (c, pw), 0)
    colp = lax.broadcasted_iota(jnp.int32, (c, pw), 1) % hd
    strict = rowp > colp
    incl = rowp >= colp
    eye_p = jnp.where(rowp == colp, 1.0, 0.0)
    left = lax.broadcasted_iota(jnp.int32, (1, pw), 1) < hd
    r2 = lax.broadcasted_iota(jnp.int32, (pw, pw), 0)
    c2 = lax.broadcasted_iota(jnp.int32, (pw, pw), 1)
    same_head = (r2 < hd) == (c2 < hd)
    eye2 = r2 == c2

    def bd(w):
        return jnp.concatenate([jnp.where(left, w, 0.0), jnp.where(left, 0.0, w)], axis=0)

    pairs = range(_N_PAIRS)
    items = [(ci, p) for ci in range(nck) for p in pairs]
    idx = range(len(items))
    sub = [(slice(ci * c, (ci + 1) * c), slice(p * pw, (p + 1) * pw)) for ci, p in items]
    v_all = v_ref[...]
    at = [at_all[s] for s in sub]
    rt = [rt_all[s] for s in sub]
    v = [v_all[s] for s in sub]
    amat = [_dot_nt(jnp.concatenate([at[j], rt[j]], axis=0),
                    jnp.concatenate([bd(bt_all[sub[j]]), bd(kt_all[sub[j]])], axis=0))
            for j in idx]
    a_ab = [jnp.where(strict, a[:c, :pw], 0.0) for a in amat]
    a_ak = [jnp.where(strict, a[:c, pw:], 0.0) for a in amat]
    a_rb = [jnp.where(incl, a[c:, :pw], 0.0) for a in amat]
    a_rk = [jnp.where(incl, a[c:, pw:], 0.0) for a in amat]
    tinv = [eye_p + a for a in a_ab]
    npow = [_dot(a, bd(a)) for a in a_ab]
    span = 4
    while span < c:
        both = [_dot(jnp.concatenate([tinv[j], npow[j]], axis=0), bd(npow[j])) for j in idx]
        tinv = [tinv[j] + both[j][:c] for j in idx]
        npow = [x[c:] for x in both]
        span *= 2
    tinv = [tinv[j] + _dot(tinv[j], bd(npow[j])) for j in idx]
    w12 = [_dot(jnp.concatenate([a_ak[j], a_rk[j]], axis=0), bd(v[j])) for j in idx]
    pq = [_dot(tinv[j], jnp.concatenate([bd(at[j]), bd(w12[j][:c])], axis=1)) for j in idx]
    y10 = [jnp.concatenate([rt[j], w12[j][c:]], axis=1)
           + _dot(a_rb[j], jnp.concatenate([bd(pq[j][:, :pw]), bd(pq[j][:, pw:])], axis=1))
           for j in idx]
    mn = [_dot_tn(bh_all[sub[j]], pq[j]) for j in idx]
    kv = [_dot_tn(kh_all[sub[j]], v[j]) for j in idx]
    m = [jnp.where(same_head, mn[j][:, :pw], 0.0)
         + jnp.where(eye2, gc_all[sub[j][0], sub[j][1]][:1], 0.0) for j in idx]
    n = [jnp.where(same_head, mn[j][:, pw:] + kv[j], 0.0) for j in idx]
    h = [h_ref[p] for p in pairs]
    y_rows = []
    for ci in range(nck):
        js = [ci * _N_PAIRS + p for p in pairs]
        my = [_dot(jnp.concatenate([m[j], y10[j][:, :pw]], axis=0), h[p]) for p, j in zip(pairs, js)]
        h = [my[p][:pw] + n[j] for p, j in zip(pairs, js)]
        y_rows.append(jnp.concatenate([my[p][pw:] + y10[j][:, pw:] for p, j in zip(pairs, js)], axis=1))
    for p in pairs:
        h_ref[p] = h[p]
    y_ref[...] = jnp.concatenate(y_rows, axis=0)

    @pl.when(i == pl.num_programs(1) - 1)
    def _():
        for p in pairs:
            s_ref[0, p] = h[p]


def _wkv_prompt(prep, nb_seq, seq):
    c = WKV_CHUNK * WKV_CHUNKS_PER_TILE
    assert WKV_CHUNK == HEAD_DIM and seq % c == 0
    nc = seq // c
    row = lambda bb, i: (bb * nc + i, 0)
    planes = (_P_R, _P_K, _P_V, _P_LW, _P_KK, _P_B)
    return pl.pallas_call(
        _wkv_prompt_kernel,
        grid=(nb_seq, nc),
        in_specs=[_plane_spec(p, c, lambda bb, i: bb * nc + i) for p in planes],
        out_specs=[pl.BlockSpec((c, D_MODEL), row),
                   pl.BlockSpec((1, _N_PAIRS, _PAIR_W, _PAIR_W), lambda bb, i: (bb, 0, 0, 0))],
        out_shape=[jax.ShapeDtypeStruct((nb_seq * seq, D_MODEL), F32),
                   jax.ShapeDtypeStruct((nb_seq, _N_PAIRS, _PAIR_W, _PAIR_W), F32)],
        scratch_shapes=[pltpu.VMEM((_N_PAIRS, _PAIR_W, _PAIR_W), F32)],
        compiler_params=_cparams("parallel", "arbitrary"),
        name="wkv_prompt",
    )(*([prep] * len(planes)))


def _unpair_state(hb):
    n = hb.shape[0]
    h6 = hb.reshape(n, _N_PAIRS, 2, HEAD_DIM, 2, HEAD_DIM)
    blocks = jnp.stack([h6[:, :, 0, :, 0, :], h6[:, :, 1, :, 1, :]], axis=2)
    return jnp.swapaxes(blocks, -1, -2).reshape(n, N_HEADS, HEAD_DIM, HEAD_DIM)


_WKV_VROWS = 4


def _wkv_sample_kernel(r_ref, lw_ref, k_ref, v_ref, kk_ref, b_ref, s0_ref, y_ref, s_ref, w_ref):
    t = r_ref.shape[0]
    vr = _WKV_VROWS
    w_ref[...] = jnp.exp(lw_ref[...])
    for v0 in range(0, HEAD_DIM, vr):
        s = [s0_ref[0, v0 + j] for j in range(vr)]
        for st in range(t):
            kk, w, b, k, r = kk_ref[st], w_ref[st], b_ref[st], k_ref[st], r_ref[st]
            vrow = v_ref[st, v0:v0 + vr, :]
            ys = []
            for j in range(vr):
                sa = -jnp.sum(s[j] * kk, axis=0, keepdims=True)
                s[j] = s[j] * w + sa * b + vrow[j:j + 1] * k
                ys.append(jnp.sum(s[j] * r, axis=0, keepdims=True))
            y_ref[st, v0:v0 + vr, :] = jnp.concatenate(ys, axis=0)
        for j in range(vr):
            s_ref[0, v0 + j] = s[j]


def _wkv_sample(vecs, s0):
    _, t, nh, hd, n = vecs.shape
    vec = lambda p: pl.BlockSpec((None, t, None, hd, n), lambda h: (p, 0, h, 0, 0))
    st = pl.BlockSpec((1, hd, hd, n), lambda h: (h, 0, 0, 0))
    planes = (_P_R, _P_LW, _P_K, _P_V, _P_KK, _P_B)
    return pl.pallas_call(
        _wkv_sample_kernel,
        grid=(nh,),
        in_specs=[vec(p) for p in planes] + [st],
        out_specs=[pl.BlockSpec((t, None, hd, n), lambda h: (0, h, 0, 0)), st],
        out_shape=[jax.ShapeDtypeStruct((t, nh, hd, n), F32),
                   jax.ShapeDtypeStruct((nh, hd, hd, n), F32)],
        scratch_shapes=[pltpu.VMEM((t, hd, n), F32)],
        compiler_params=_cparams("parallel"),
        name="wkv_sample",
    )(*([vecs] * len(planes)), s0)


def _mix_out_kernel(x_ref, oa_ref, ob_ref, y_ref, r_ref, k_ref, v_ref, g_ref, wgate_ref,
                    lnw_ref, lnb_ref, rk_ref, wout_ref, g1_ref, b1_ref, e_ref, et_ref, o_ref, *, alpha):
    d = D_MODEL
    x = x_ref[...]
    gate = jnp.dot(x.astype(BF16), wgate_ref[...], preferred_element_type=F32)
    e, et = e_ref[...], et_ref[...]
    y = y_ref[...]
    mu = _head_sum(y, e, et, 1.0 / HEAD_DIM)
    yc = y - mu
    var = _head_sum(yc * yc, e, et, 1.0 / HEAD_DIM)
    yn = yc * lax.rsqrt(var + GN_EPS) * lnw_ref[...] + lnb_ref[...]
    v = v_ref[...]
    bonus = _head_sum(r_ref[...] * k_ref[...] * rk_ref[...], e, et) * v
    oc = (yn + bonus) * g_ref[...]
    mixed = (_sigmoid(gate[:, 0:d]) * oa_ref[...] + _sigmoid(gate[:, d:2 * d]) * ob_ref[...]
             + _sigmoid(gate[:, 2 * d:3 * d]) * oc)
    z = alpha * x + jnp.dot(mixed.astype(BF16), wout_ref[...], preferred_element_type=F32)
    o_ref[...] = _layer_norm(z, g1_ref[...], b1_ref[...])


def _mix_out(x, oa, ob, y, prep, wgate, lnw, lnb, rk, wout, g1, b1, e, et, alpha):
    m = x.shape[0]
    tm = _tile(m, 256)
    row = lambda i: (i, 0)
    full = lambda i: (0, 0)
    rows = pl.BlockSpec((tm, D_MODEL), row)
    vec = pl.BlockSpec((1, D_MODEL), full)
    resident = lambda w: pl.BlockSpec(w.shape, full, pipeline_mode=pl.Buffered(1))
    planes = (_P_R, _P_K, _P_V, _P_G)
    return pl.pallas_call(
        functools.partial(_mix_out_kernel, alpha=alpha),
        grid=(m // tm,),
        in_specs=[rows] * 4 + [_plane_spec(p, tm, lambda i: i) for p in planes]
                 + [resident(wgate), vec, vec, vec, resident(wout), vec, vec,
                    pl.BlockSpec(e.shape, full), pl.BlockSpec(et.shape, full)],
        out_specs=rows,
        out_shape=jax.ShapeDtypeStruct((m, D_MODEL), F32),
        compiler_params=_cparams("parallel"),
        name="mix_out",
    )(x, oa, ob, y, *([prep] * len(planes)), wgate, lnw, lnb, rk, wout, g1, b1, e, et)


def _ffn_kernel(x_ref, wgu_ref, wd_ref, g2_ref, b2_ref, o_ref, *, alpha):
    dff = wd_ref.shape[0]
    x = x_ref[...]
    xb = x.astype(BF16)
    gt = jnp.dot(xb, wgu_ref[:, :dff], preferred_element_type=F32)
    up = jnp.dot(xb, wgu_ref[:, dff:], preferred_element_type=F32)
    ff = jnp.dot((gt * _sigmoid(gt) * up).astype(BF16), wd_ref[...], preferred_element_type=F32)
    o_ref[...] = _layer_norm(alpha * x + ff, g2_ref[...], b2_ref[...])


def _ffn(x, wgu, wd, g2, b2, alpha):
    m = x.shape[0]
    tm = _tile(m, 256)
    row = lambda i: (i, 0)
    full = lambda i: (0, 0)
    resident = lambda w: pl.BlockSpec(w.shape, full, pipeline_mode=pl.Buffered(1))
    return pl.pallas_call(
        functools.partial(_ffn_kernel, alpha=alpha),
        grid=(m // tm,),
        in_specs=[pl.BlockSpec((tm, D_MODEL), row), resident(wgu), resident(wd),
                  pl.BlockSpec((1, D_MODEL), full), pl.BlockSpec((1, D_MODEL), full)],
        out_specs=pl.BlockSpec((tm, D_MODEL), row),
        out_shape=jax.ShapeDtypeStruct((m, D_MODEL), F32),
        compiler_params=_cparams("parallel"),
        name="ffn",
    )(x, wgu, wd, g2, b2)


def _blockdiag_pairs(w):
    nb, bs, _ = w.shape
    w = w.reshape(nb // 2, 2, bs, bs)
    z = jnp.zeros((nb // 2, bs, bs), w.dtype)
    top = jnp.concatenate([w[:, 0], z], axis=2)
    bot = jnp.concatenate([z, w[:, 1]], axis=2)
    return jnp.concatenate([top, bot], axis=1)


def _row(p):
    return p.reshape(1, -1)


def kernel(x_prompt, x_sample, cache_k, cache_v, state_conv, state_rglru, state_shift, state_wkv, w_in, attn_sinks, conv_w, conv_b, rg_wa, rg_ba, rg_wx, rg_bx, rg_lambda, rw_mu, rw_w0, rw_wup, rw_a0, rw_aup, rw_gup, rw_kk, rw_ka, rw_rk, rw_lnw, rw_lnb, w_out, ln1_g, ln1_b, w_gu, w_down, ln2_g, ln2_b):
    depth = w_in.shape[0]
    bp, seq, d = x_prompt.shape
    ns, ts, _ = x_sample.shape
    wb = cache_k.shape[2]
    alpha = (2 * depth) ** 0.25
    dff = w_down.shape[1]
    pad = SHIFT_PAD - SHIFT_W

    xp = x_prompt.reshape(bp * seq, d)
    xs = jnp.swapaxes(x_sample, 0, 1).reshape(ts * ns, d)
    st_p = [[] for _ in range(6)]
    st_s = [[] for _ in range(6)]
    head_e, head_et = _head_indicator()
    w_in_bf = w_in.astype(BF16)
    w_gu_bf = w_gu.astype(BF16)
    w_down_bf = w_down.astype(BF16)

    for l in range(depth):
        wl = w_in_bf[l]
        w_all = wl[:, :OFF_RW]
        w_rw = jnp.pad(wl[:, OFF_RW:OFF_GATE], ((0, 0), (0, pad)))
        w_gate = wl[:, OFF_GATE:]
        wa2 = _blockdiag_pairs(rg_wa[l]).astype(BF16)
        wx2 = _blockdiag_pairs(rg_wx[l]).astype(BF16)
        rg_args = (conv_w[l], _row(conv_b[l]), wa2, _row(rg_ba[l]), wx2, _row(rg_bx[l]), _row(rg_lambda[l]))
        zw = jnp.zeros((LORA_W, d), F32)
        wwa = jnp.concatenate([jnp.concatenate([rw_wup[l], zw], axis=1),
                               jnp.concatenate([zw, rw_aup[l]], axis=1)], axis=0).astype(BF16)
        gup = jnp.pad(rw_gup[l], ((0, 2 * LANES - LORA_G), (0, 0))).astype(BF16)
        prep_params = (_row(jnp.pad(rw_mu[l], (0, pad))), _row(rw_w0[l]), _row(rw_a0[l]), _row(rw_kk[l]),
                       _row(rw_ka[l]), wwa, gup, head_e, head_et)
        mix_params = (_row(rw_lnw[l]), _row(rw_lnb[l]), _row(rw_rk[l]), w_out[l].astype(BF16),
                      _row(ln1_g[l]), _row(ln1_b[l]), head_e, head_et)
        ffn_params = (w_gu_bf[l], w_down_bf[l], _row(ln2_g[l]), _row(ln2_b[l]))
        sinks = attn_sinks[l]

        qkv, ob, rnn_tail = _proj_rglru_prompt(xp, w_all, bp, seq, *rg_args)
        oa = _attn_prompt(qkv, sinks, bp, seq)
        prep, sh_p = _rwkv_front(xp, w_rw, jnp.zeros((bp, 8, SHIFT_PAD), F32), bp, 1, prep_params)
        y, hfin = _wkv_prompt(prep, bp, seq)
        x1 = _mix_out(xp, oa, ob, y, prep, w_gate, *mix_params, alpha)
        xp = _ffn(x1, *ffn_params, alpha)
        qkv3 = qkv.reshape(bp, seq, OFF_RNN)
        st_p[0].append(qkv3[:, seq - wb:, OFF_K:OFF_V].reshape(bp, wb, N_KV_HEADS, HEAD_DIM))
        st_p[1].append(qkv3[:, seq - wb:, OFF_V:OFF_RNN].reshape(bp, wb, N_KV_HEADS, HEAD_DIM))
        st_p[2].append(rnn_tail[:, 8 - (CONV_W - 1):])
        st_p[3].append(ob.reshape(bp, seq, d)[:, seq - 1])
        st_p[4].append(sh_p[:, 7, :SHIFT_W])
        st_p[5].append(_unpair_state(hfin))

        qkv, rnn = _in_proj(xs, w_all)
        qkv_n = jnp.swapaxes(qkv.reshape(ts, ns, OFF_RNN), 0, 1)
        oa_n, new_k, new_v = _attn_sample(qkv_n[:, :, :OFF_K], qkv_n[:, :, OFF_K:OFF_V], qkv_n[:, :, OFF_V:],
                                          cache_k[l].reshape(ns, wb, KV_W), cache_v[l].reshape(ns, wb, KV_W),
                                          sinks)
        oa = jnp.swapaxes(oa_n, 0, 1).reshape(ts * ns, d)
        rnn3 = rnn.reshape(ts, ns, d)
        ob3 = _rglru_sample(rnn3, jnp.swapaxes(state_conv[l], 0, 1), state_rglru[l], *rg_args)
        ob = ob3.reshape(ts * ns, d)
        shift = jnp.pad(state_shift[l], ((0, 0), (0, pad)))[None]
        prep, sh_s = _rwkv_front(xs, w_rw, shift, 1, ns, prep_params)
        vecs = jnp.swapaxes(prep.reshape(_N_PREP_OUT, ts, ns, d), 2, 3).reshape(
            _N_PREP_OUT, ts, N_HEADS, HEAD_DIM, ns)
        s0 = jnp.transpose(state_wkv[l], (1, 2, 3, 0))
        y4, s_new = _wkv_sample(vecs, s0)
        y = jnp.swapaxes(y4.reshape(ts, d, ns), 1, 2).reshape(ts * ns, d)
        x1 = _mix_out(xs, oa, ob, y, prep, w_gate, *mix_params, alpha)
        xs = _ffn(x1, *ffn_params, alpha)
        st_s[0].append(new_k.reshape(ns, wb, N_KV_HEADS, HEAD_DIM))
        st_s[1].append(new_v.reshape(ns, wb, N_KV_HEADS, HEAD_DIM))
        st_s[2].append(jnp.swapaxes(rnn3[ts - (CONV_W - 1):], 0, 1))
        st_s[3].append(ob3[ts - 1])
        st_s[4].append(sh_s[0, :, :SHIFT_W])
        st_s[5].append(jnp.transpose(s_new, (3, 0, 1, 2)))

    yp = xp.reshape(bp, seq, d)
    ys = jnp.swapaxes(xs.reshape(ts, ns, d), 0, 1)
    return (yp, ys, *(jnp.stack(s) for s in st_p), *(jnp.stack(s) for s in st_s))
```
